```python
import jax
import jax.numpy as jnp
from jax import lax
import numpy as np


D_MODEL = 1024
BATCH = 8
SEQ = 4096
DEPTH = 4

GRID_W = 64
CTX_LEN = 256
N_MIXERS = 2
EPS = 1e-6
F32 = jnp.float32

NA_HEADS = 16
NA_HEAD_DIM = D_MODEL // NA_HEADS
NA_KR = 8
NA_KC = 16

ML_HEADS = 8
ML_DV = D_MODEL // ML_HEADS
ML_DQK = ML_DV // 2
ML_CHUNK = 64
ML_GATE_CAP = 15.0
ML_FGATE_BIAS = 3.0
ML_IN = 2 * ML_HEADS * ML_DQK + ML_HEADS * ML_DV + D_MODEL + 4 * ML_HEADS
ROPE_THETA = 10000.0

N_EXPERTS = 32
TOP_K = 4
D_FF = D_MODEL
SWIGLU_LIMIT = 7.0
SWIGLU_ALPHA = 1.702

kernel_name = 'hybrid_na_mlstm_moe_dit'


def rmsnorm(z, gain):
    zf = z.astype(F32)
    return (zf * lax.rsqrt(jnp.mean(zf * zf, axis=-1, keepdims=True) + EPS)).astype(z.dtype) * gain


def modulate(h, shift, scale):
    return h * (1 + scale) + shift


def axial_rope(z, row, col):
    half = z.shape[-1] // 2

    def rope_1d(u, pos):
        n = u.shape[-1] // 2
        inv = ROPE_THETA ** (-jnp.arange(n, dtype=F32) / n)
        ang = pos.astype(F32)[:, None] * inv
        cos, sin = jnp.cos(ang), jnp.sin(ang)
        u1, u2 = u[..., :n], u[..., n:]
        return jnp.concatenate([u1 * cos - u2 * sin, u1 * sin + u2 * cos], axis=-1)

    return jnp.concatenate([rope_1d(z[..., :half], row), rope_1d(z[..., half:], col)], axis=-1)


def na_mixer(h_lat, h_ctx, w_qkv, q_gain, k_gain, rpb, w_o, ctx_out):
    bsz, seq, _ = h_lat.shape
    rows = seq // GRID_W
    kr = min(NA_KR, rows)

    def split_heads(h):
        t = h.shape[1]
        qkv = (h @ w_qkv).reshape(bsz, t, 3, NA_HEADS, NA_HEAD_DIM)
        qkv = jnp.transpose(qkv, (2, 0, 3, 1, 4))
        q = rmsnorm(qkv[0], q_gain) * (NA_HEAD_DIM ** -0.5)
        k = rmsnorm(qkv[1], k_gain)
        return q, k, qkv[2]

    q_l, k_l, v_l = split_heads(h_lat)
    q_c, k_c, v_c = split_heads(h_ctx)

    def grid(t):
        return t.reshape(bsz, NA_HEADS, rows, GRID_W, NA_HEAD_DIM)

    k_g, v_g = grid(k_l), grid(v_l)
    q_rows = jnp.moveaxis(grid(q_l), 2, 0)
    row_start = jnp.clip(jnp.arange(rows) - kr // 2, 0, rows - kr)
    cols = jnp.arange(GRID_W)
    col_idx = jnp.clip(cols - NA_KC // 2, 0, GRID_W - NA_KC)[:, None] + jnp.arange(NA_KC)
    rpb_cols = rpb[:, :, col_idx - cols[:, None] + NA_KC - 1]
    n_win = kr * NA_KC

    def row_block(args):
        r, q_r = args
        r0 = row_start[r]
        k_win = lax.dynamic_slice_in_dim(k_g, r0, kr, axis=2)[:, :, :, col_idx]
        v_win = lax.dynamic_slice_in_dim(v_g, r0, kr, axis=2)[:, :, :, col_idx]
        bias = jnp.transpose(rpb_cols[:, r0 + jnp.arange(kr) - r + NA_KR - 1], (0, 2, 1, 3))
        s_win = jnp.einsum('bhqd,bhiqjd->bhqij', q_r, k_win).astype(F32) + bias.astype(F32)
        s_ctx = jnp.einsum('bhqd,bhkd->bhqk', q_r, k_c).astype(F32)
        s = jnp.concatenate([s_win.reshape(bsz, NA_HEADS, GRID_W, n_win), s_ctx], axis=-1)
        p = jax.nn.softmax(s, axis=-1).astype(v_g.dtype)
        p_win = p[..., :n_win].reshape(bsz, NA_HEADS, GRID_W, kr, NA_KC)
        return (jnp.einsum('bhqij,bhiqjd->bhqd', p_win, v_win)
                + jnp.einsum('bhqk,bhkd->bhqd', p[..., n_win:], v_c))

    o = lax.map(row_block, (jnp.arange(rows), q_rows))
    y_lat = jnp.transpose(o, (1, 0, 3, 2, 4)).reshape(bsz, seq, D_MODEL) @ w_o
    if not ctx_out:
        return y_lat, None
    s_cc = jnp.einsum('bhqd,bhkd->bhqk', q_c, k_c).astype(F32)
    o_c = jnp.einsum('bhqk,bhkd->bhqd', jax.nn.softmax(s_cc, axis=-1).astype(v_c.dtype), v_c)
    y_ctx = jnp.transpose(o_c, (0, 2, 1, 3)).reshape(bsz, h_ctx.shape[1], D_MODEL) @ w_o
    return y_lat, y_ctx


def mlstm_chunkwise(q, k, v, i_pre, log_f):
    bsz, nh, t, dk = q.shape
    dv = v.shape[-1]
    nc = t // ML_CHUNK
    q = q.reshape(bsz, nh, nc, ML_CHUNK, dk) * (dk ** -0.5)
    k = k.reshape(bsz, nh, nc, ML_CHUNK, dk)
    v = v.reshape(bsz, nh, nc, ML_CHUNK, dv)
    ig = i_pre.reshape(bsz, nh, nc, ML_CHUNK)
    cum_f = jnp.cumsum(log_f.reshape(bsz, nh, nc, ML_CHUNK), axis=-1)
    cum_last = cum_f[..., -1]
    w_end = cum_last[..., None] - cum_f + ig
    m_loc = jnp.max(w_end, axis=-1)
    e_end = jnp.exp(w_end - m_loc[..., None])
    d_c = jnp.einsum('bhcs,bhcsv,bhcsk->cbhvk', e_end, v, k)
    d_n = jnp.einsum('bhcs,bhcsk->cbhk', e_end, k)

    def step(carry, inp):
        c_st, n_st, m_st = carry
        dc_c, dn_c, cl_c, ml_c = inp
        m_new = jnp.maximum(cl_c + m_st, ml_c)
        a = jnp.exp(cl_c + m_st - m_new)
        g = jnp.exp(ml_c - m_new)
        c_new = a[..., None, None] * c_st + g[..., None, None] * dc_c
        n_new = a[..., None] * n_st + g[..., None] * dn_c
        return (c_new, n_new, m_new), (c_st, n_st, m_st)

    init = (jnp.zeros((bsz, nh, dv, dk), F32), jnp.zeros((bsz, nh, dk), F32), jnp.zeros((bsz, nh), F32))
    _, (c0, n0, m0) = lax.scan(step, init, (d_c, d_n, jnp.moveaxis(cum_last, 2, 0), jnp.moveaxis(m_loc, 2, 0)))
    log_inter = cum_f + jnp.moveaxis(m0, 0, 2)[..., None]
    causal = jnp.tril(jnp.ones((ML_CHUNK, ML_CHUNK), dtype=bool))
    d_mat = jnp.where(causal, cum_f[..., :, None] - cum_f[..., None, :] + ig[..., None, :], -jnp.inf)
    m_comb = jnp.maximum(log_inter, jnp.max(d_mat, axis=-1))
    w_inter = jnp.exp(log_inter - m_comb)
    s = jnp.einsum('bhctk,bhcsk->bhcts', q, k) * jnp.exp(d_mat - m_comb[..., None])
    num = (jnp.einsum('bhcts,bhcsv->bhctv', s, v)
           + w_inter[..., None] * jnp.einsum('cbhvk,bhctk->bhctv', c0, q))
    den = jnp.sum(s, axis=-1) + w_inter * jnp.einsum('cbhk,bhctk->bhct', n0, q)
    h_out = num / jnp.maximum(jnp.abs(den), jnp.exp(-m_comb))[..., None]
    return h_out.reshape(bsz, nh, t, dv)


def mlstm_mixer(h_lat, h_ctx, w_in, b_gate, mh_gain, w_o, ctx_out):
    bsz, seq, _ = h_lat.shape
    n_ctx = h_ctx.shape[1]
    t = n_ctx + seq
    proj = jnp.concatenate([h_ctx, h_lat], axis=1) @ w_in
    nq = ML_HEADS * ML_DQK
    nv = ML_HEADS * ML_DV
    q, k, v, og, gate_pre = jnp.split(proj, [nq, 2 * nq, 2 * nq + nv, 2 * nq + nv + D_MODEL], axis=-1)

    def heads(z, d):
        return jnp.transpose(z.reshape(bsz, t, ML_HEADS, d), (0, 2, 1, 3)).astype(F32)

    q, k, v = heads(q, ML_DQK), heads(k, ML_DQK), heads(v, ML_DV)
    pos = jnp.arange(seq)

    def rot(z):
        return jnp.concatenate([z[:, :, :n_ctx], axial_rope(z[:, :, n_ctx:], pos // GRID_W, pos % GRID_W)], axis=2)

    q, k = rot(q), rot(k)
    gates = ML_GATE_CAP * jnp.tanh((gate_pre.astype(F32) + b_gate) / ML_GATE_CAP)
    gates = jnp.transpose(gates.reshape(bsz, t, 4, ML_HEADS), (2, 0, 3, 1))
    i_fw, f_fw, i_bw, f_bw = gates[0], gates[1], gates[2], gates[3]

    def flip(z):
        return jnp.concatenate([jnp.flip(z[:, :, :n_ctx], axis=2), jnp.flip(z[:, :, n_ctx:], axis=2)], axis=2)

    h_fw = mlstm_chunkwise(q, k, v, i_fw, jax.nn.log_sigmoid(f_fw))
    h_bw = flip(mlstm_chunkwise(flip(q), flip(k), flip(v), flip(i_bw), flip(jax.nn.log_sigmoid(f_bw))))
    h_sum = h_fw + h_bw
    h_n = h_sum * lax.rsqrt(jnp.mean(h_sum * h_sum, axis=-1, keepdims=True) + EPS)
    h_n = (jnp.transpose(h_n, (0, 2, 1, 3)).reshape(bsz, t, nv) * mh_gain).astype(h_lat.dtype)
    out = h_n * jax.nn.sigmoid(og)
    y_lat = out[:, n_ctx:] @ w_o
    if not ctx_out:
        return y_lat, None
    return y_lat, out[:, :n_ctx] @ w_o


def moe_ffn(h, w_router, b_router, w1, b1, w2, b2):
    n = h.shape[0]
    logits = (h @ w_router + b_router).astype(F32)
    top_val, top_idx = lax.top_k(logits, TOP_K)
    weights = jax.nn.softmax(top_val, axis=-1)
    gate = jnp.einsum('nk,nke->en', weights, jax.nn.one_hot(top_idx, N_EXPERTS, dtype=F32))
    acc = jnp.zeros((n, D_MODEL), F32)
    for e in range(N_EXPERTS):
        u = h @ w1[e] + b1[e]
        glu = jnp.minimum(u[:, :D_FF], SWIGLU_LIMIT)
        lin = jnp.clip(u[:, D_FF:], -SWIGLU_LIMIT, SWIGLU_LIMIT)
        act = glu * jax.nn.sigmoid(SWIGLU_ALPHA * glu) * (lin + 1)
        acc = acc + gate[e][:, None] * (act @ w2[e] + b2[e]).astype(F32)
    return acc.astype(h.dtype)


def setup_inputs(seed: int = 0) -> dict:
    key = jax.random.key(seed)
    ks = jax.random.split(key, 24)
    d = D_MODEL
    n_na = (DEPTH + N_MIXERS - 1) // N_MIXERS
    n_ml = DEPTH // N_MIXERS

    def nrm(k, shape, scale):
        return scale * jax.random.normal(k, shape, F32)

    gate_offsets = jnp.repeat(jnp.array([0.0, ML_FGATE_BIAS, 0.0, ML_FGATE_BIAS], F32), ML_HEADS)
    return {
        'x': nrm(ks[0], (BATCH, SEQ, d), 1.0),
        'c': nrm(ks[1], (BATCH, d), 1.0),
        'ctx': nrm(ks[2], (BATCH, CTX_LEN, d), 1.0),
        'c_ctx': nrm(ks[3], (d,), 1.0),
        'w_ada': nrm(ks[4], (DEPTH, d, 6 * d), 0.5 * d ** -0.5),
        'b_ada': nrm(ks[5], (DEPTH, 6 * d), 0.02),
        'g_norm_mix': 1.0 + nrm(ks[6], (DEPTH, d), 0.05),
        'g_norm_ffn': 1.0 + nrm(ks[7], (DEPTH, d), 0.05),
        'na_w_qkv': nrm(ks[8], (n_na, d, 3 * d), d ** -0.5),
        'na_q_gain': 1.0 + nrm(ks[9], (n_na, NA_HEAD_DIM), 0.05),
        'na_k_gain': 1.0 + nrm(ks[10], (n_na, NA_HEAD_DIM), 0.05),
        'na_rpb': nrm(ks[11], (n_na, NA_HEADS, 2 * NA_KR - 1, 2 * NA_KC - 1), 0.5),
        'na_w_o': nrm(ks[12], (n_na, d, d), d ** -0.5),
        'ml_w_in': nrm(ks[13], (n_ml, d, ML_IN), d ** -0.5),
        'ml_b_gate': gate_offsets + nrm(ks[14], (n_ml, 4 * ML_HEADS), 0.1),
        'ml_mh_gain': 1.0 + nrm(ks[15], (n_ml, ML_HEADS * ML_DV), 0.05),
        'ml_w_o': nrm(ks[16], (n_ml, ML_HEADS * ML_DV, d), (ML_HEADS * ML_DV) ** -0.5),
        'moe_w_router': nrm(ks[17], (DEPTH, d, N_EXPERTS), d ** -0.5),
        'moe_b_router': nrm(ks[18], (DEPTH, N_EXPERTS), 0.01),
        'moe_w1': nrm(ks[19], (DEPTH, N_EXPERTS, d, 2 * D_FF), d ** -0.5),
        'moe_b1': nrm(ks[20], (DEPTH, N_EXPERTS, 2 * D_FF), 0.02),
        'moe_w2': nrm(ks[21], (DEPTH, N_EXPERTS, D_FF, d), D_FF ** -0.5),
        'moe_b2': nrm(ks[22], (DEPTH, N_EXPERTS, d), 0.02),
    }


def reference(x, c, ctx, c_ctx, w_ada, b_ada, g_norm_mix, g_norm_ffn,
              na_w_qkv, na_q_gain, na_k_gain, na_rpb, na_w_o,
              ml_w_in, ml_b_gate, ml_mh_gain, ml_w_o,
              moe_w_router, moe_b_router, moe_w1, moe_b1, moe_w2, moe_b2):
    bsz = x.shape[0]
    n_ctx_tok = bsz * ctx.shape[1]
    for i in range(DEPTH):
        last = i == DEPTH - 1
        j = i // N_MIXERS
        mod_lat = jax.nn.silu(c) @ w_ada[i] + b_ada[i]
        mod_ctx = jax.nn.silu(c_ctx) @ w_ada[i] + b_ada[i]
        sh1_l, sc1_l, g1_l, sh2_l, sc2_l, g2_l = jnp.split(mod_lat[:, None, :], 6, axis=-1)
        sh1_c, sc1_c, g1_c, sh2_c, sc2_c, g2_c = jnp.split(mod_ctx, 6, axis=-1)

        h_l = modulate(rmsnorm(x, g_norm_mix[i]), sh1_l, sc1_l)
        h_c = modulate(rmsnorm(ctx, g_norm_mix[i]), sh1_c, sc1_c)
        if i % N_MIXERS == 0:
            y_l, y_c = na_mixer(h_l, h_c, na_w_qkv[j], na_q_gain[j], na_k_gain[j], na_rpb[j], na_w_o[j],
                                ctx_out=not last)
        else:
            y_l, y_c = mlstm_mixer(h_l, h_c, ml_w_in[j], ml_b_gate[j], ml_mh_gain[j], ml_w_o[j],
                                   ctx_out=not last)
        x = x + g1_l * y_l

        h_l = modulate(rmsnorm(x, g_norm_ffn[i]), sh2_l, sc2_l)
        if last:
            f_l = moe_ffn(h_l.reshape(-1, D_MODEL), moe_w_router[i], moe_b_router[i],
                          moe_w1[i], moe_b1[i], moe_w2[i], moe_b2[i])
            x = x + g2_l * f_l.reshape(x.shape)
        else:
            ctx = ctx + g1_c * y_c
            h_c = modulate(rmsnorm(ctx, g_norm_ffn[i]), sh2_c, sc2_c)
            h_all = jnp.concatenate([h_c.reshape(-1, D_MODEL), h_l.reshape(-1, D_MODEL)], axis=0)
            f_all = moe_ffn(h_all, moe_w_router[i], moe_b_router[i],
                            moe_w1[i], moe_b1[i], moe_w2[i], moe_b2[i])
            ctx = ctx + g2_c * f_all[:n_ctx_tok].reshape(ctx.shape)
            x = x + g2_l * f_all[n_ctx_tok:].reshape(x.shape)
    return x
```

```python
import functools
import math

import numpy as np
import jax
import jax.numpy as jnp
from jax import lax
from jax.experimental import pallas as pl
from jax.experimental.pallas import tpu as pltpu

F32 = jnp.float32
BF16 = jnp.bfloat16
I32 = jnp.int32

EPS = 1e-6
GRID_W = 64
NA_HEADS = 16
NA_HEAD_DIM = 64
NA_KR = 8
NA_KC = 16
NA_QROWS = 4
NA_KROWS = NA_QROWS + NA_KR - 1
ML_HEADS = 8
ML_DV = 128
ML_DQK = 64
ML_CHUNK = 256
ML_GATE_CAP = 15.0
ROPE_THETA = 10000.0
N_EXPERTS = 32
TOP_K = 4
SWIGLU_LIMIT = 7.0
SWIGLU_ALPHA = 1.702

TM = 256
LANES = 128
NEG = -1e30
VMEM_LIMIT = 48 * 1024 * 1024


def _dot(a, b):
    return jnp.dot(a, b, preferred_element_type=F32)


def _dot_nt(a, b):
    return lax.dot_general(a, b, (((1,), (1,)), ((), ())), preferred_element_type=F32)


def _dot_tn(a, b):
    return lax.dot_general(a, b, (((0,), (0,)), ((), ())), preferred_element_type=F32)


def _split3(x):
    hi = x.astype(BF16)
    r1 = x - hi.astype(F32)
    mid = r1.astype(BF16)
    lo = (r1 - mid.astype(F32)).astype(BF16)
    return hi, mid, lo


def _params(*sem):
    return pltpu.CompilerParams(dimension_semantics=sem, vmem_limit_bytes=VMEM_LIMIT)


def _norm_mod(x, gain, shift, scale):
    r = lax.rsqrt(jnp.mean(x * x, axis=-1, keepdims=True) + EPS)
    return (x * r * gain) * (1.0 + scale) + shift


def _mod_spec_d(n_batch, which, d):
    return pl.BlockSpec((None, None, 1, d), lambda b, t: (jnp.where(t == 0, n_batch, b), which, 0, 0))


def _ada_kernel(c_ref, w_ref, b_ref, o_ref):
    c = c_ref[...]
    s = c * jax.nn.sigmoid(c)
    o_ref[0] = _dot(s.astype(BF16), w_ref[0].astype(BF16)) + b_ref[0]


def _ada(cc, w_ada, b_ada):
    depth, d, n = w_ada.shape
    bn = 1536
    return pl.pallas_call(
        _ada_kernel,
        grid=(depth, n // bn),
        in_specs=[pl.BlockSpec((cc.shape[0], d), lambda l, j: (0, 0)),
                  pl.BlockSpec((1, d, bn), lambda l, j: (l, 0, j)),
                  pl.BlockSpec((1, 1, bn), lambda l, j: (l, 0, j))],
        out_specs=pl.BlockSpec((1, cc.shape[0], bn), lambda l, j: (l, 0, j)),
        out_shape=jax.ShapeDtypeStruct((depth, cc.shape[0], n), F32),
        compiler_params=_params("arbitrary", "arbitrary"),
        name="ada",
    )(cc, w_ada, b_ada.reshape(depth, 1, n))


def _na_qkv_kernel(x_ref, g_ref, sh_ref, sc_ref, w_ref, o_ref):
    h = _norm_mod(x_ref[0], g_ref[...], sh_ref[...], sc_ref[...])
    o_ref[0] = _dot(h.astype(BF16), w_ref[...]).astype(BF16)


def _na_qkv(s, gain, mod, w):
    bsz, t, d = s.shape
    n = w.shape[1]
    return pl.pallas_call(
        _na_qkv_kernel,
        grid=(bsz, t // TM),
        in_specs=[pl.BlockSpec((1, TM, d), lambda b, i: (b, i, 0)),
                  pl.BlockSpec((1, d), lambda b, i: (0, 0)),
                  _mod_spec_d(bsz, 0, d), _mod_spec_d(bsz, 1, d),
                  pl.BlockSpec((d, n), lambda b, i: (0, 0))],
        out_specs=pl.BlockSpec((1, TM, n), lambda b, i: (b, i, 0)),
        out_shape=jax.ShapeDtypeStruct((bsz, t, n), BF16),
        compiler_params=_params("arbitrary", "arbitrary"),
        name="na_qkv",
    )(s, gain.reshape(1, d), mod, mod, w)


def _na_bias(rpb, rows):
    nblk = rows // NA_QROWS
    kmax = rows - NA_KROWS
    out = []
    for blk in (0, 1, nblk - 1):
        rs = blk * NA_QROWS
        ks = min(max(rs - NA_KR // 2, 0), kmax)
        qr = rs + np.arange(NA_QROWS)[:, None, None, None]
        qc = np.arange(GRID_W)[None, :, None, None]
        kr = ks + np.arange(NA_KROWS)[None, None, :, None]
        kc = np.arange(GRID_W)[None, None, None, :]
        r0 = np.clip(qr - NA_KR // 2, 0, rows - NA_KR)
        c0 = np.clip(qc - NA_KC // 2, 0, GRID_W - NA_KC)
        valid = (kr >= r0) & (kr < r0 + NA_KR) & (kc >= c0) & (kc < c0 + NA_KC)
        ri = np.clip(kr - qr + NA_KR - 1, 0, 2 * NA_KR - 2)
        ci = np.clip(kc - qc + NA_KC - 1, 0, 2 * NA_KC - 2)
        shape = (NA_QROWS, GRID_W, NA_KROWS, GRID_W)
        valid, ri, ci = (np.broadcast_to(a, shape).reshape(NA_QROWS * GRID_W, NA_KROWS * GRID_W)
                         for a in (valid, ri, ci))
        out.append(jnp.where(valid[None], rpb[:, ri, ci], NEG))
    bias = jnp.stack(out, axis=1)
    nq, nk = bias.shape[-2:]
    return bias.reshape(NA_HEADS // 2, 2, 3, nq, nk).transpose(0, 2, 1, 3, 4).astype(F32)


def _na_attn_kernel(q_ref, k_ref, v_ref, bias_ref, qg_ref, kg_ref, o_ref, kn_ref, *, n_ctx, rows):
    t = pl.program_id(2)
    first = lax.broadcasted_iota(I32, (1, LANES), 1) < NA_HEAD_DIM
    nq = NA_QROWS * GRID_W
    nk = NA_KROWS * GRID_W

    def head_rms(z, gain):
        z2 = z * z
        sa = jnp.sum(jnp.where(first, z2, 0.0), axis=-1, keepdims=True)
        sb = jnp.sum(jnp.where(first, 0.0, z2), axis=-1, keepdims=True)
        r = jnp.where(first, lax.rsqrt(sa / NA_HEAD_DIM + EPS), lax.rsqrt(sb / NA_HEAD_DIM + EPS))
        return z * r * gain

    @pl.when(t == 0)
    def _():
        def body(i, carry):
            off = pl.multiple_of(i * TM, TM)
            kn_ref[pl.ds(off, TM), :] = head_rms(k_ref[0, pl.ds(off, TM), :].astype(F32), kg_ref[...]).astype(BF16)
            return carry
        lax.fori_loop(0, k_ref.shape[1] // TM, body, 0)

    qn = (head_rms(q_ref[0].astype(F32), qg_ref[...]) * (NA_HEAD_DIM ** -0.5)).astype(BF16)
    zero = jnp.zeros_like(qn)
    q_heads = (jnp.where(first, qn, zero), jnp.where(first, zero, qn))
    kc = kn_ref[0:n_ctx, :]
    vc = v_ref[0, 0:n_ctx, :]

    def finish(outs):
        o_ref[0] = jnp.where(first, outs[0], outs[1]).astype(o_ref.dtype)

    @pl.when(t == 0)
    def _():
        outs = []
        for qh in q_heads:
            s = _dot_nt(qh, kc)
            m = jnp.max(s, axis=-1, keepdims=True)
            p = jnp.exp(s - m)
            outs.append(_dot(p.astype(BF16), vc) / jnp.sum(p, axis=-1, keepdims=True))
        finish(outs)

    @pl.when(t > 0)
    def _():
        blk = t - 1
        nblk = rows // NA_QROWS
        ks = jnp.clip(blk * NA_QROWS - NA_KR // 2, 0, rows - NA_KROWS)
        start = pl.multiple_of(n_ctx + ks * GRID_W, GRID_W)
        kw = kn_ref[pl.ds(start, nk), :]
        vw = v_ref[0, pl.ds(start, nk), :]
        kind = jnp.where(blk == 0, 0, jnp.where(blk == nblk - 1, 2, 1))
        outs = []
        for hh, qh in enumerate(q_heads):
            sw = _dot_nt(qh, kw) + bias_ref[0, kind, hh]
            sc = _dot_nt(qh, kc)
            m = jnp.maximum(jnp.max(sw, axis=-1, keepdims=True), jnp.max(sc, axis=-1, keepdims=True))
            pw = jnp.exp(sw - m)
            pc = jnp.exp(sc - m)
            l = jnp.sum(pw, axis=-1, keepdims=True) + jnp.sum(pc, axis=-1, keepdims=True)
            outs.append((_dot(pw.astype(BF16), vw) + _dot(pc.astype(BF16), vc)) / l)
        finish(outs)


def _na_attn(qkv, bias, q_gain, k_gain, n_ctx):
    bsz, t, _ = qkv.shape
    d = NA_HEADS * NA_HEAD_DIM
    rows = (t - n_ctx) // GRID_W
    assert n_ctx == TM and NA_QROWS * GRID_W == TM and rows % NA_QROWS == 0 and rows >= NA_KROWS
    hp = NA_HEADS // 2
    nq, nk = bias.shape[-2:]
    gq = jnp.tile(q_gain, 2).reshape(1, LANES)
    gk = jnp.tile(k_gain, 2).reshape(1, LANES)
    return pl.pallas_call(
        functools.partial(_na_attn_kernel, n_ctx=n_ctx, rows=rows),
        grid=(hp, bsz, t // TM),
        in_specs=[pl.BlockSpec((1, TM, LANES), lambda h, b, i: (b, i, h)),
                  pl.BlockSpec((1, t, LANES), lambda h, b, i: (b, 0, hp + h)),
                  pl.BlockSpec((1, t, LANES), lambda h, b, i: (b, 0, 2 * hp + h)),
                  pl.BlockSpec((1, 3, 2, nq, nk), lambda h, b, i: (h, 0, 0, 0, 0)),
                  pl.BlockSpec((1, LANES), lambda h, b, i: (0, 0)),
                  pl.BlockSpec((1, LANES), lambda h, b, i: (0, 0))],
        out_specs=pl.BlockSpec((1, TM, LANES), lambda h, b, i: (b, i, h)),
        out_shape=jax.ShapeDtypeStruct((bsz, t, d), BF16),
        scratch_shapes=[pltpu.VMEM((t, LANES), BF16)],
        compiler_params=_params("arbitrary", "arbitrary", "arbitrary"),
        name="na_attn",
    )(qkv, qkv, qkv, bias, gq, gk)


def _oproj_kernel(y_ref, w_ref, s_ref, g_ref, o_ref):
    o_ref[0] = s_ref[0] + g_ref[...] * _dot(y_ref[0], w_ref[...])


def _oproj(y, w, s, mod, which):
    bsz, t, d = s.shape
    return pl.pallas_call(
        _oproj_kernel,
        grid=(bsz, t // TM),
        in_specs=[pl.BlockSpec((1, TM, y.shape[-1]), lambda b, i: (b, i, 0)),
                  pl.BlockSpec(w.shape, lambda b, i: (0, 0)),
                  pl.BlockSpec((1, TM, d), lambda b, i: (b, i, 0)),
                  _mod_spec_d(bsz, which, d)],
        out_specs=pl.BlockSpec((1, TM, d), lambda b, i: (b, i, 0)),
        out_shape=jax.ShapeDtypeStruct(s.shape, F32),
        input_output_aliases={2: 0},
        compiler_params=_params("arbitrary", "arbitrary"),
        name="oproj",
    )(y, w, s, mod)


def _rope_tables(n_ctx, seq):
    lane = np.arange(LANES)
    dd = lane % ML_DQK
    grp = dd // (ML_DQK // 2)
    j = dd % (ML_DQK // 2)
    n = ML_DQK // 4
    inv = ROPE_THETA ** (-(j % n).astype(np.float64) / n)
    pos = np.arange(seq)
    p = np.where(grp[None, :] == 0, (pos // GRID_W)[:, None], (pos % GRID_W)[:, None]).astype(np.float64)
    ang = p * inv[None, :]
    cos = np.cos(ang)
    sin = np.where(j[None, :] < n, -np.sin(ang), np.sin(ang))
    cos = np.concatenate([np.ones((n_ctx, LANES)), cos], axis=0)
    sin = np.concatenate([np.zeros((n_ctx, LANES)), sin], axis=0)
    return jnp.asarray(cos, F32), jnp.asarray(sin, F32)


def _ml_proj_kernel(x_ref, g_ref, sh_ref, sc_ref, wm_ref, wg_ref, wgt_ref, bgr_ref, bgc_ref, cos_ref, sin_ref,
                    q_ref, k_ref, v_ref, og_ref, gr_ref, gc_ref):
    h = _norm_mod(x_ref[0], g_ref[...], sh_ref[...], sc_ref[...]).astype(BF16)
    main = _dot(h, wm_ref[...])
    nq = q_ref.shape[-1]
    nv = v_ref.shape[-1]
    cos = cos_ref[...]
    sin = sin_ref[...]
    n = ML_DQK // 4
    lane = lax.broadcasted_iota(I32, (1, LANES), 1)
    low = (lane % (2 * n)) < n

    def rope(z):
        zr = jnp.where(low, pltpu.roll(z, LANES - n, 1), pltpu.roll(z, n, 1))
        return z * cos + zr * sin

    for j in range(nq // LANES):
        sl = slice(j * LANES, (j + 1) * LANES)
        q_ref[0, :, sl] = (rope(main[:, sl]) * (ML_DQK ** -0.5)).astype(BF16)
        k_ref[0, :, sl] = rope(main[:, nq + j * LANES: nq + (j + 1) * LANES]).astype(BF16)
    v_ref[0] = main[:, 2 * nq: 2 * nq + nv].astype(BF16)
    og_ref[0] = main[:, 2 * nq + nv:].astype(BF16)

    nh = ML_HEADS
    ti = lax.broadcasted_iota(I32, (TM, TM), 0)
    si = lax.broadcasted_iota(I32, (TM, TM), 1)
    le = (si <= ti).astype(BF16)
    ge = (si >= ti).astype(BF16)

    def capped(pre):
        return ML_GATE_CAP * jnp.tanh(pre / ML_GATE_CAP)

    def exact_cum(tri, x, nt):
        parts = _split3(x)
        if nt:
            return sum(_dot_nt(p, tri) for p in parts)
        return sum(_dot(tri, p) for p in parts)

    gc = capped(_dot(h, wg_ref[...]) + bgr_ref[...])
    lf = jax.nn.log_sigmoid(gc)
    pre = exact_cum(le, lf, False)
    suf = exact_cum(ge, lf, False)
    cum = jnp.where(lane < 2 * nh, pre, suf)
    r = gc - pltpu.roll(cum, LANES - nh, 1)
    is_i = (lane // nh) % 2 == 0
    gc_ref[0] = jnp.where(is_i, r, cum)[:, :4 * nh]

    gr = capped(_dot_nt(wgt_ref[...], h) + bgc_ref[...])
    lfr = jax.nn.log_sigmoid(gr)
    pre_r = exact_cum(le, lfr, True)
    suf_r = exact_cum(ge, lfr, True)
    gr_ref[0, 0 * nh:1 * nh, :] = gr[0 * nh:1 * nh] - pre_r[1 * nh:2 * nh]
    gr_ref[0, 1 * nh:2 * nh, :] = pre_r[1 * nh:2 * nh]
    gr_ref[0, 2 * nh:3 * nh, :] = gr[2 * nh:3 * nh] - suf_r[3 * nh:4 * nh]
    gr_ref[0, 3 * nh:4 * nh, :] = suf_r[3 * nh:4 * nh]


def _ml_proj(s, gain, mod, w_in, b_gate, n_ctx):
    bsz, t, d = s.shape
    nq = ML_HEADS * ML_DQK
    nv = ML_HEADS * ML_DV
    ng = 4 * ML_HEADS
    assert TM == ML_CHUNK
    wm = w_in[:, :2 * nq + nv + d].astype(BF16)
    wg = w_in[:, 2 * nq + nv + d:]
    wg_pad = jnp.pad(wg, ((0, 0), (0, LANES - ng))).astype(BF16)
    wgt = wg.T.astype(BF16)
    bgr = jnp.pad(b_gate, (0, LANES - ng)).reshape(1, LANES)
    bgc = b_gate.reshape(ng, 1)
    cos, sin = _rope_tables(n_ctx, t - n_ctx)
    tile = lambda n: pl.BlockSpec((1, TM, n), lambda b, i: (b, i, 0))
    const = lambda a: pl.BlockSpec(a.shape, lambda b, i: (0, 0))
    return pl.pallas_call(
        _ml_proj_kernel,
        grid=(bsz, t // TM),
        in_specs=[tile(d), pl.BlockSpec((1, d), lambda b, i: (0, 0)),
                  _mod_spec_d(bsz, 0, d), _mod_spec_d(bsz, 1, d),
                  const(wm), const(wg_pad), const(wgt), const(bgr), const(bgc),
                  pl.BlockSpec((TM, LANES), lambda b, i: (i, 0)),
                  pl.BlockSpec((TM, LANES), lambda b, i: (i, 0))],
        out_specs=[tile(nq), tile(nq), tile(nv), tile(d),
                   pl.BlockSpec((1, ng, TM), lambda b, i: (b, 0, i)),
                   tile(ng)],
        out_shape=[jax.ShapeDtypeStruct((bsz, t, nq), BF16), jax.ShapeDtypeStruct((bsz, t, nq), BF16),
                   jax.ShapeDtypeStruct((bsz, t, nv), BF16), jax.ShapeDtypeStruct((bsz, t, d), BF16),
                   jax.ShapeDtypeStruct((bsz, ng, t), F32), jax.ShapeDtypeStruct((bsz, t, ng), F32)],
        compiler_params=_params("arbitrary", "arbitrary"),
        name="ml_proj",
    )(s, gain.reshape(1, d), mod, mod, wm, wg_pad, wgt, bgr, bgc, cos, sin)


def _mlstm_kernel(q_ref, k_ref, v_ref, og_ref, gr_ref, gc_ref, mg_ref, o_ref, hf_ref, hb_ref, c_ref):
    L = ML_CHUNK
    nchunk = q_ref.shape[1] // L
    first = lax.broadcasted_iota(I32, (1, LANES), 1) < ML_DQK
    ti = lax.broadcasted_iota(I32, (L, L), 0)
    si = lax.broadcasted_iota(I32, (L, L), 1)
    masks = (si <= ti, si >= ti)
    ones_col = (lax.broadcasted_iota(I32, (L, ML_DV), 1) == 0).astype(BF16)

    c_ref[...] = jnp.zeros_like(c_ref)

    def chunk(c, a, bw, m0):
        off = pl.multiple_of(c * L, L)
        sel = first if a == 0 else jnp.logical_not(first)
        q = q_ref[0, pl.ds(off, L), :]
        k = k_ref[0, pl.ds(off, L), :]
        qa = jnp.where(sel, q, jnp.zeros_like(q))
        ka = jnp.where(sel, k, jnp.zeros_like(k))
        v = v_ref[0, pl.ds(off, L), a * ML_DV:(a + 1) * ML_DV]
        w = 2 if bw else 0
        r_row = gr_ref[0, 0, 2 * w + a: 2 * w + a + 1, pl.ds(off, L)]
        cum_row = gr_ref[0, 0, 2 * (w + 1) + a: 2 * (w + 1) + a + 1, pl.ds(off, L)]
        r_col = gc_ref[0, 0, pl.ds(off, L), 2 * w + a: 2 * w + a + 1]
        cum_col = gc_ref[0, 0, pl.ds(off, L), 2 * (w + 1) + a: 2 * (w + 1) + a + 1]
        tot = cum_row[:, 0:1] if bw else cum_row[:, L - 1:L]
        mask = masks[1 if bw else 0]
        rb = jnp.broadcast_to(r_row, (L, L))
        mx = jnp.max(jnp.where(mask, rb, NEG), axis=1, keepdims=True)
        mm = jnp.maximum(m0, mx)
        e = jnp.exp(jnp.where(mask, rb - mm, NEG))
        s = (_dot_nt(qa, k) * e).astype(BF16)
        v2 = jnp.concatenate([v, ones_col], axis=1)
        idx = 2 * a + (1 if bw else 0)
        cst = c_ref[idx]
        both = _dot(s, v2) + jnp.exp(m0 - mm) * _dot(qa, cst.astype(BF16))
        num = both[:, :ML_DV]
        den = both[:, ML_DV:ML_DV + 1]
        hout = num / jnp.maximum(jnp.abs(den), jnp.exp(-cum_col - mm))
        dst = hb_ref if bw else hf_ref
        dst[pl.ds(off, L), a * ML_DV:(a + 1) * ML_DV] = hout
        m_loc = tot + jnp.max(r_row, axis=1, keepdims=True)
        x = (v2.astype(F32) * jnp.exp(tot + r_col - m_loc)).astype(BF16)
        u = _dot_tn(ka, x)
        m_new = jnp.maximum(tot + m0, m_loc)
        c_ref[idx] = jnp.exp(tot + m0 - m_new) * cst + jnp.exp(m_loc - m_new) * u
        return m_new

    def body(i, ms):
        cb = jnp.where(i == 0, 0, nchunk - i)
        return (chunk(i, 0, False, ms[0]), chunk(i, 1, False, ms[1]),
                chunk(cb, 0, True, ms[2]), chunk(cb, 1, True, ms[3]))

    z = jnp.zeros((1, 1), F32)
    lax.fori_loop(0, nchunk, body, (z, z, z, z))

    def epilogue(i, carry):
        off = pl.multiple_of(i * L, L)
        for a in range(2):
            sl = slice(a * ML_DV, (a + 1) * ML_DV)
            hs = hf_ref[pl.ds(off, L), sl] + hb_ref[pl.ds(off, L), sl]
            hn = hs * lax.rsqrt(jnp.mean(hs * hs, axis=-1, keepdims=True) + EPS) * mg_ref[:, sl]
            o_ref[0, pl.ds(off, L), sl] = (hn * jax.nn.sigmoid(og_ref[0, pl.ds(off, L), sl].astype(F32))).astype(BF16)
        return carry

    lax.fori_loop(0, nchunk, epilogue, 0)


def _mlstm(q, k, v, og, gr, gc, mh_gain):
    bsz, t, _ = q.shape
    hp = ML_HEADS // 2
    nv = ML_HEADS * ML_DV
    w2 = 2 * ML_DV
    gr = gr.reshape(bsz, 4, hp, 2, t).transpose(0, 2, 1, 3, 4).reshape(bsz, hp, 8, t)
    gc = gc.reshape(bsz, t, 4, hp, 2).transpose(0, 3, 1, 2, 4).reshape(bsz, hp, t, 8)
    return pl.pallas_call(
        _mlstm_kernel,
        grid=(bsz, hp),
        in_specs=[pl.BlockSpec((1, t, LANES), lambda b, h: (b, 0, h)),
                  pl.BlockSpec((1, t, LANES), lambda b, h: (b, 0, h)),
                  pl.BlockSpec((1, t, w2), lambda b, h: (b, 0, h)),
                  pl.BlockSpec((1, t, w2), lambda b, h: (b, 0, h)),
                  pl.BlockSpec((1, 1, 8, t), lambda b, h: (b, h, 0, 0)),
                  pl.BlockSpec((1, 1, t, 8), lambda b, h: (b, h, 0, 0)),
                  pl.BlockSpec((1, w2), lambda b, h: (0, h))],
        out_specs=pl.BlockSpec((1, t, w2), lambda b, h: (b, 0, h)),
        out_shape=jax.ShapeDtypeStruct((bsz, t, nv), BF16),
        scratch_shapes=[pltpu.VMEM((t, w2), F32), pltpu.VMEM((t, w2), F32),
                        pltpu.VMEM((4, LANES, w2), F32)],
        compiler_params=_params("arbitrary", "arbitrary"),
        name="mlstm",
    )(q, k, v, og, gr, gc, mh_gain.reshape(1, nv))


def _moe_pre_kernel(x_ref, g_ref, sh_ref, sc_ref, whi_ref, wlo_ref, br_ref,
                    h_ref, id_ref, wt_ref, rk_ref, cnt_ref):
    step = pl.program_id(0) * pl.num_programs(1) + pl.program_id(1)

    @pl.when(step == 0)
    def _():
        cnt_ref[...] = jnp.zeros_like(cnt_ref)

    h = _norm_mod(x_ref[0], g_ref[...], sh_ref[...], sc_ref[...])
    h_ref[0] = h
    hhi = h.astype(BF16)
    hlo = (h - hhi.astype(F32)).astype(BF16)
    whi = whi_ref[...]
    logits = _dot_nt(whi, hhi) + _dot_nt(whi, hlo) + _dot_nt(wlo_ref[...], hhi) + br_ref[...]
    ne = logits.shape[0]
    eidx = lax.broadcasted_iota(I32, logits.shape, 0)
    sels, vals, ids = [], [], []
    l = logits
    for _ in range(TOP_K):
        mval = jnp.max(l, axis=0, keepdims=True)
        idx = jnp.min(jnp.where(l == mval, eidx, ne), axis=0, keepdims=True)
        sel = eidx == idx
        l = jnp.where(sel, -jnp.inf, l)
        sels.append(sel)
        vals.append(mval)
        ids.append(idx)
    ex = [jnp.exp(v - vals[0]) for v in vals]
    tot = ex[0] + ex[1] + ex[2] + ex[3]
    member = sels[0] | sels[1] | sels[2] | sels[3]
    ti = lax.broadcasted_iota(I32, (TM, TM), 0)
    si = lax.broadcasted_iota(I32, (TM, TM), 1)
    before = (ti < si).astype(BF16)
    rank_e = _dot(member.astype(BF16), before) + cnt_ref[:, 0:1]
    for kk in range(TOP_K):
        id_ref[kk:kk + 1, :] = ids[kk]
        wt_ref[kk:kk + 1, :] = ex[kk] / tot
        rk_ref[kk:kk + 1, :] = jnp.sum(jnp.where(sels[kk], rank_e, 0.0), axis=0, keepdims=True).astype(I32)
    cnt_ref[...] = cnt_ref[...] + jnp.sum(member.astype(F32), axis=1, keepdims=True)


def _moe_pre(s, gain, mod, w_router, b_router):
    bsz, t, d = s.shape
    n = bsz * t
    nt = t // TM
    wt = w_router.T
    whi = wt.astype(BF16)
    wlo = (wt - whi.astype(F32)).astype(BF16)
    tok = lambda dt: jax.ShapeDtypeStruct((TOP_K, n), dt)
    tok_spec = pl.BlockSpec((TOP_K, TM), lambda b, i: (0, b * nt + i))
    return pl.pallas_call(
        _moe_pre_kernel,
        grid=(bsz, nt),
        in_specs=[pl.BlockSpec((1, TM, d), lambda b, i: (b, i, 0)),
                  pl.BlockSpec((1, d), lambda b, i: (0, 0)),
                  _mod_spec_d(bsz, 3, d), _mod_spec_d(bsz, 4, d),
                  pl.BlockSpec(whi.shape, lambda b, i: (0, 0)),
                  pl.BlockSpec(wlo.shape, lambda b, i: (0, 0)),
                  pl.BlockSpec((N_EXPERTS, 1), lambda b, i: (0, 0))],
        out_specs=[pl.BlockSpec((1, TM, d), lambda b, i: (b, i, 0)), tok_spec, tok_spec, tok_spec,
                   pl.BlockSpec((N_EXPERTS, LANES), lambda b, i: (0, 0))],
        out_shape=[jax.ShapeDtypeStruct((bsz, t, d), F32), tok(I32), tok(F32), tok(I32),
                   jax.ShapeDtypeStruct((N_EXPERTS, LANES), F32)],
        compiler_params=_params("arbitrary", "arbitrary"),
        name="moe_pre",
    )(s, gain.reshape(1, d), mod, mod, whi, wlo, b_router.reshape(N_EXPERTS, 1))


def _dispatch_kernel(last_ref, slot_ref, h_ref, xs_ref, zero_ref, sem, zsem):
    step = pl.program_id(0)
    n_exp = last_ref.shape[0] - 1
    n_tiles = xs_ref.shape[0] // TM

    def zero_copy(row):
        return pltpu.make_async_copy(zero_ref, xs_ref.at[pl.ds(pl.multiple_of(row, TM), TM)], zsem)

    @pl.when(step == 0)
    def _():
        zero_ref[...] = jnp.zeros_like(zero_ref)
        for e in range(n_exp):
            @pl.when(last_ref[e] >= 0)
            def _():
                zero_copy(last_ref[e]).start()
        for e in range(n_exp):
            @pl.when(last_ref[e] >= 0)
            def _():
                zero_copy(last_ref[e]).wait()

        def tail(i, carry):
            cp = zero_copy(i * TM)
            cp.start()
            cp.wait()
            return carry
        lax.fori_loop(last_ref[n_exp] // TM, n_tiles, tail, 0)

    def row_copy(j, kk):
        return pltpu.make_async_copy(h_ref.at[pl.ds(j, 1)], xs_ref.at[pl.ds(slot_ref[0, kk, j], 1)], sem)

    def issue(j, carry):
        for kk in range(TOP_K):
            row_copy(j, kk).start()
        return carry

    def drain(j, carry):
        for kk in range(TOP_K):
            row_copy(j, kk).wait()
        return carry

    lax.fori_loop(0, TM, issue, 0)
    lax.fori_loop(0, TM, drain, 0)


def _dispatch(h, slots, last_tile, n_slots):
    n, d = h.shape
    nt = n // TM
    slots3 = slots.reshape(TOP_K, nt, TM).transpose(1, 0, 2)
    return pl.pallas_call(
        _dispatch_kernel,
        grid_spec=pltpu.PrefetchScalarGridSpec(
            num_scalar_prefetch=1,
            grid=(nt,),
            in_specs=[pl.BlockSpec((1, TOP_K, TM), lambda i, last: (i, 0, 0), memory_space=pltpu.SMEM),
                      pl.BlockSpec((TM, d), lambda i, last: (i, 0))],
            out_specs=pl.BlockSpec(memory_space=pl.ANY),
            scratch_shapes=[pltpu.VMEM((TM, d), F32), pltpu.SemaphoreType.DMA, pltpu.SemaphoreType.DMA]),
        out_shape=jax.ShapeDtypeStruct((n_slots, d), F32),
        compiler_params=_params("arbitrary"),
        name="moe_dispatch",
    )(last_tile, slots3, h)


def _ffn_kernel(te_ref, nu_ref, x_ref, w1_ref, b1_ref, w2_ref, b2_ref, y_ref):
    i = pl.program_id(0)

    @pl.when(i < nu_ref[0])
    def _():
        dff = w2_ref.shape[1]
        u = _dot(x_ref[...].astype(BF16), w1_ref[0]) + b1_ref[0]
        glu = jnp.minimum(u[:, :dff], SWIGLU_LIMIT)
        lin = jnp.clip(u[:, dff:], -SWIGLU_LIMIT, SWIGLU_LIMIT)
        act = glu * jax.nn.sigmoid(SWIGLU_ALPHA * glu) * (lin + 1.0)
        y_ref[...] = _dot(act.astype(BF16), w2_ref[0]) + b2_ref[0]

    @pl.when(i >= nu_ref[0])
    def _():
        y_ref[...] = jnp.zeros_like(y_ref)


def _ffn(xs, tile_expert, n_used, w1, b1, w2, b2):
    ns, d = xs.shape
    ne, _, n1 = w1.shape
    row = lambda i, te, nu: (i, 0)
    exp3 = lambda i, te, nu: (te[i], 0, 0)
    return pl.pallas_call(
        _ffn_kernel,
        grid_spec=pltpu.PrefetchScalarGridSpec(
            num_scalar_prefetch=2,
            grid=(ns // TM,),
            in_specs=[pl.BlockSpec((TM, d), row),
                      pl.BlockSpec((1, d, n1), exp3), pl.BlockSpec((1, 1, n1), exp3),
                      pl.BlockSpec((1, w2.shape[1], d), exp3), pl.BlockSpec((1, 1, d), exp3)],
            out_specs=pl.BlockSpec((TM, d), row)),
        out_shape=jax.ShapeDtypeStruct((ns, d), F32),
        compiler_params=_params("arbitrary"),
        name="moe_ffn",
    )(tile_expert, n_used, xs, w1, b1.reshape(ne, 1, n1), w2, b2.reshape(ne, 1, d))


def _combine_kernel(slot_ref, wt_ref, s_ref, g_ref, ys_ref, o_ref, buf_ref, sem):
    def row_copy(j, kk):
        return pltpu.make_async_copy(ys_ref.at[pl.ds(slot_ref[0, kk, j], 1)], buf_ref.at[kk, pl.ds(j, 1)], sem)

    def issue(j, carry):
        for kk in range(TOP_K):
            row_copy(j, kk).start()
        return carry

    def drain(j, carry):
        for kk in range(TOP_K):
            row_copy(j, kk).wait()
        return carry

    lax.fori_loop(0, TM, issue, 0)
    lax.fori_loop(0, TM, drain, 0)
    wt = wt_ref[...]
    f = wt[:, 0:1] * buf_ref[0]
    for kk in range(1, TOP_K):
        f = f + wt[:, kk:kk + 1] * buf_ref[kk]
    o_ref[0] = s_ref[0] + g_ref[...] * f


def _combine(ys, slots, wts, s, mod, which):
    bsz, t, d = s.shape
    nt = t // TM
    n = bsz * t
    slots3 = slots.reshape(TOP_K, n // TM, TM).transpose(1, 0, 2)
    return pl.pallas_call(
        _combine_kernel,
        grid=(bsz, nt),
        in_specs=[pl.BlockSpec((1, TOP_K, TM), lambda b, i: (b * nt + i, 0, 0), memory_space=pltpu.SMEM),
                  pl.BlockSpec((TM, TOP_K), lambda b, i: (b * nt + i, 0)),
                  pl.BlockSpec((1, TM, d), lambda b, i: (b, i, 0)),
                  _mod_spec_d(bsz, which, d),
                  pl.BlockSpec(memory_space=pl.ANY)],
        out_specs=pl.BlockSpec((1, TM, d), lambda b, i: (b, i, 0)),
        out_shape=jax.ShapeDtypeStruct(s.shape, F32),
        scratch_shapes=[pltpu.VMEM((TOP_K, TM, d), F32), pltpu.SemaphoreType.DMA],
        input_output_aliases={2: 0},
        compiler_params=_params("arbitrary", "arbitrary"),
        name="moe_combine",
    )(slots3, wts.T, s, mod, ys)


def _moe(s, gain, mod, w_router, b_router, w1, b1, w2, b2):
    bsz, t, d = s.shape
    n = bsz * t
    h, ids, wts, ranks, counts = _moe_pre(s, gain, mod, w_router, b_router)
    cnt = counts[:, 0].astype(I32)
    padded = ((cnt + TM - 1) // TM) * TM
    ends = jnp.cumsum(padded)
    offs = ends - padded
    n_tiles = (TOP_K * n + N_EXPERTS * (TM - 1) + TM - 1) // TM
    n_used = (ends[-1] // TM).astype(I32)
    tile_start = jnp.arange(n_tiles, dtype=I32) * TM
    te = jnp.sum(tile_start[:, None] >= ends[None, :], axis=1).astype(I32)
    te_last = jnp.max(jnp.where(cnt > 0, jnp.arange(N_EXPERTS, dtype=I32), 0))
    te = jnp.minimum(te, te_last)
    slots = offs[ids] + ranks
    last_tile = jnp.concatenate([jnp.where(cnt > 0, ends - TM, -1), ends[-1:]]).astype(I32)
    xs = _dispatch(h.reshape(n, d), slots, last_tile, n_tiles * TM)
    ys = _ffn(xs, te, n_used.reshape(1), w1.astype(BF16), b1, w2.astype(BF16), b2)
    return _combine(ys, slots, wts, s, mod, 5)


def kernel(x, c, ctx, c_ctx, w_ada, b_ada, g_norm_mix, g_norm_ffn, na_w_qkv, na_q_gain, na_k_gain, na_rpb,
           na_w_o, ml_w_in, ml_b_gate, ml_mh_gain, ml_w_o, moe_w_router, moe_b_router, moe_w1, moe_b1,
           moe_w2, moe_b2):
    bsz, seq, d = x.shape
    n_ctx = ctx.shape[1]
    depth = w_ada.shape[0]
    rows = seq // GRID_W
    n_mod_rows = ((bsz + 1 + 7) // 8) * 8
    cc = jnp.concatenate([c, c_ctx[None, :], jnp.zeros((n_mod_rows - bsz - 1, d), F32)], axis=0)
    mod_all = _ada(cc, w_ada, b_ada).reshape(depth, n_mod_rows, 6, 1, d)
    s = jnp.concatenate([ctx, x], axis=1)
    for i in range(depth):
        j = i // 2
        mod = mod_all[i]
        if i % 2 == 0:
            qkv = _na_qkv(s, g_norm_mix[i], mod, na_w_qkv[j].astype(BF16))
            y = _na_attn(qkv, _na_bias(na_rpb[j], rows), na_q_gain[j], na_k_gain[j], n_ctx)
            s = _oproj(y, na_w_o[j].astype(BF16), s, mod, 2)
        else:
            q, k, v, og, gr, gc = _ml_proj(s, g_norm_mix[i], mod, ml_w_in[j], ml_b_gate[j], n_ctx)
            y = _mlstm(q, k, v, og, gr, gc, ml_mh_gain[j])
            s = _oproj(y, ml_w_o[j].astype(BF16), s, mod, 2)
        s = _moe(s, g_norm_ffn[i], mod, moe_w_router[i], moe_b_router[i],
                 moe_w1[i], moe_b1[i], moe_w2[i], moe_b2[i])
    return s[:, n_ctx:]
```

```python
import functools
import math

import numpy as np
import jax
import jax.numpy as jnp
from jax import lax
from jax.experimental import pallas as pl
from jax.experimental.pallas import tpu as pltpu

F32 = jnp.float32
BF16 = jnp.bfloat16
I32 = jnp.int32

EPS = 1e-6
GRID_W = 64
NA_HEADS = 16
NA_HEAD_DIM = 64
NA_KR = 8
NA_KC = 16
NA_QROWS = 4
NA_KROWS = NA_QROWS + NA_KR - 1
ML_HEADS = 8
ML_DV = 128
ML_DQK = 64
ML_CHUNK = 256
ML_GATE_CAP = 15.0
ROPE_THETA = 10000.0
N_EXPERTS = 32
TOP_K = 4
SWIGLU_LIMIT = 7.0
SWIGLU_ALPHA = 1.702

TM = 256
LANES = 128
NEG = -1e30
VMEM_LIMIT = 48 * 1024 * 1024


def _dot(a, b):
    return jnp.dot(a, b, preferred_element_type=F32)


def _dot_nt(a, b):
    return lax.dot_general(a, b, (((1,), (1,)), ((), ())), preferred_element_type=F32)


def _dot_tn(a, b):
    return lax.dot_general(a, b, (((0,), (0,)), ((), ())), preferred_element_type=F32)


def _split3(x):
    hi = x.astype(BF16)
    r1 = x - hi.astype(F32)
    mid = r1.astype(BF16)
    lo = (r1 - mid.astype(F32)).astype(BF16)
    return hi, mid, lo


def _params(*sem):
    return pltpu.CompilerParams(dimension_semantics=sem, vmem_limit_bytes=VMEM_LIMIT)


def _norm_mod(x, gain, shift, scale):
    r = lax.rsqrt(jnp.mean(x * x, axis=-1, keepdims=True) + EPS)
    return (x * r * gain) * (1.0 + scale) + shift


def _mod_spec_d(n_batch, which, d):
    return pl.BlockSpec((None, None, 1, d), lambda b, t: (jnp.where(t == 0, n_batch, b), which, 0, 0))


def _ada_kernel(c_ref, w_ref, b_ref, o_ref):
    c = c_ref[...]
    s = c * jax.nn.sigmoid(c)
    o_ref[0] = _dot(s.astype(BF16), w_ref[0].astype(BF16)) + b_ref[0]


def _ada(cc, w_ada, b_ada):
    depth, d, n = w_ada.shape
    bn = 1536
    return pl.pallas_call(
        _ada_kernel,
        grid=(depth, n // bn),
        in_specs=[pl.BlockSpec((cc.shape[0], d), lambda l, j: (0, 0)),
                  pl.BlockSpec((1, d, bn), lambda l, j: (l, 0, j)),
                  pl.BlockSpec((1, 1, bn), lambda l, j: (l, 0, j))],
        out_specs=pl.BlockSpec((1, cc.shape[0], bn), lambda l, j: (l, 0, j)),
        out_shape=jax.ShapeDtypeStruct((depth, cc.shape[0], n), F32),
        compiler_params=_params("arbitrary", "arbitrary"),
        name="ada",
    )(cc, w_ada, b_ada.reshape(depth, 1, n))


def _na_qkv_kernel(x_ref, g_ref, sh_ref, sc_ref, w_ref, o_ref):
    h = _norm_mod(x_ref[0], g_ref[...], sh_ref[...], sc_ref[...])
    o_ref[0] = _dot(h.astype(BF16), w_ref[...]).astype(BF16)


def _na_qkv(s, gain, mod, w):
    bsz, t, d = s.shape
    n = w.shape[1]
    return pl.pallas_call(
        _na_qkv_kernel,
        grid=(bsz, t // TM),
        in_specs=[pl.BlockSpec((1, TM, d), lambda b, i: (b, i, 0)),
                  pl.BlockSpec((1, d), lambda b, i: (0, 0)),
                  _mod_spec_d(bsz, 0, d), _mod_spec_d(bsz, 1, d),
                  pl.BlockSpec((d, n), lambda b, i: (0, 0))],
        out_specs=pl.BlockSpec((1, TM, n), lambda b, i: (b, i, 0)),
        out_shape=jax.ShapeDtypeStruct((bsz, t, n), BF16),
        compiler_params=_params("arbitrary", "arbitrary"),
        name="na_qkv",
    )(s, gain.reshape(1, d), mod, mod, w)


def _na_bias(rpb, rows):
    nblk = rows // NA_QROWS
    kmax = rows - NA_KROWS
    nrr, ncc = 2 * NA_KR - 1, 2 * NA_KC - 1
    qc = np.arange(GRID_W)[:, None]
    kc = np.arange(GRID_W)[None, :]
    c0 = np.clip(qc - NA_KC // 2, 0, GRID_W - NA_KC)
    col_ok = (kc >= c0) & (kc < c0 + NA_KC)
    col_pick = np.eye(ncc)[np.clip(kc - qc + NA_KC - 1, 0, ncc - 1)]
    row_pick, valid = [], []
    for blk in (0, 1, nblk - 1):
        rs = blk * NA_QROWS
        ks = min(max(rs - NA_KR // 2, 0), kmax)
        qr = rs + np.arange(NA_QROWS)[:, None]
        kr = ks + np.arange(NA_KROWS)[None, :]
        r0 = np.clip(qr - NA_KR // 2, 0, rows - NA_KR)
        row_ok = (kr >= r0) & (kr < r0 + NA_KR)
        row_pick.append(np.eye(nrr)[np.clip(kr - qr + NA_KR - 1, 0, nrr - 1)])
        valid.append(row_ok[:, None, :, None] & col_ok[None, :, None, :])
    bias = jnp.einsum('vrka,hab,qcb->hvrqkc', jnp.asarray(np.stack(row_pick), F32), rpb,
                      jnp.asarray(col_pick, F32), precision=lax.Precision.HIGHEST)
    bias = jnp.where(np.stack(valid)[None], bias, NEG)
    nq, nk = NA_QROWS * GRID_W, NA_KROWS * GRID_W
    return bias.reshape(NA_HEADS // 2, 2, 3, nq, nk).transpose(0, 2, 1, 3, 4).astype(F32)


def _na_attn_kernel(q_ref, k_ref, v_ref, bias_ref, qg_ref, kg_ref, o_ref, kn_ref, *, n_ctx, rows):
    t = pl.program_id(2)
    first = lax.broadcasted_iota(I32, (1, LANES), 1) < NA_HEAD_DIM
    nq = NA_QROWS * GRID_W
    nk = NA_KROWS * GRID_W

    def head_rms(z, gain):
        z2 = z * z
        sa = jnp.sum(jnp.where(first, z2, 0.0), axis=-1, keepdims=True)
        sb = jnp.sum(jnp.where(first, 0.0, z2), axis=-1, keepdims=True)
        r = jnp.where(first, lax.rsqrt(sa / NA_HEAD_DIM + EPS), lax.rsqrt(sb / NA_HEAD_DIM + EPS))
        return z * r * gain

    @pl.when(t == 0)
    def _():
        def body(i, carry):
            off = pl.multiple_of(i * TM, TM)
            kn_ref[pl.ds(off, TM), :] = head_rms(k_ref[0, pl.ds(off, TM), :].astype(F32), kg_ref[...]).astype(BF16)
            return carry
        lax.fori_loop(0, k_ref.shape[1] // TM, body, 0)

    qn = (head_rms(q_ref[0].astype(F32), qg_ref[...]) * (NA_HEAD_DIM ** -0.5)).astype(BF16)
    zero = jnp.zeros_like(qn)
    q_heads = (jnp.where(first, qn, zero), jnp.where(first, zero, qn))
    kc = kn_ref[0:n_ctx, :]
    vc = v_ref[0, 0:n_ctx, :]

    def finish(outs):
        o_ref[0] = jnp.where(first, outs[0], outs[1]).astype(o_ref.dtype)

    @pl.when(t == 0)
    def _():
        outs = []
        for qh in q_heads:
            s = _dot_nt(qh, kc)
            m = jnp.max(s, axis=-1, keepdims=True)
            p = jnp.exp(s - m)
            outs.append(_dot(p.astype(BF16), vc) / jnp.sum(p, axis=-1, keepdims=True))
        finish(outs)

    @pl.when(t > 0)
    def _():
        blk = t - 1
        nblk = rows // NA_QROWS
        ks = jnp.clip(blk * NA_QROWS - NA_KR // 2, 0, rows - NA_KROWS)
        start = pl.multiple_of(n_ctx + ks * GRID_W, GRID_W)
        kw = kn_ref[pl.ds(start, nk), :]
        vw = v_ref[0, pl.ds(start, nk), :]
        kind = jnp.where(blk == 0, 0, jnp.where(blk == nblk - 1, 2, 1))
        outs = []
        for hh, qh in enumerate(q_heads):
            sw = _dot_nt(qh, kw) + bias_ref[0, kind, hh]
            sc = _dot_nt(qh, kc)
            m = jnp.maximum(jnp.max(sw, axis=-1, keepdims=True), jnp.max(sc, axis=-1, keepdims=True))
            pw = jnp.exp(sw - m)
            pc = jnp.exp(sc - m)
            l = jnp.sum(pw, axis=-1, keepdims=True) + jnp.sum(pc, axis=-1, keepdims=True)
            outs.append((_dot(pw.astype(BF16), vw) + _dot(pc.astype(BF16), vc)) / l)
        finish(outs)


def _na_attn(qkv, bias, q_gain, k_gain, n_ctx):
    bsz, t, _ = qkv.shape
    d = NA_HEADS * NA_HEAD_DIM
    rows = (t - n_ctx) // GRID_W
    assert n_ctx == TM and NA_QROWS * GRID_W == TM and rows % NA_QROWS == 0 and rows >= NA_KROWS
    hp = NA_HEADS // 2
    nq, nk = bias.shape[-2:]
    gq = jnp.tile(q_gain, 2).reshape(1, LANES)
    gk = jnp.tile(k_gain, 2).reshape(1, LANES)
    return pl.pallas_call(
        functools.partial(_na_attn_kernel, n_ctx=n_ctx, rows=rows),
        grid=(hp, bsz, t // TM),
        in_specs=[pl.BlockSpec((1, TM, LANES), lambda h, b, i: (b, i, h)),
                  pl.BlockSpec((1, t, LANES), lambda h, b, i: (b, 0, hp + h)),
                  pl.BlockSpec((1, t, LANES), lambda h, b, i: (b, 0, 2 * hp + h)),
                  pl.BlockSpec((1, 3, 2, nq, nk), lambda h, b, i: (h, 0, 0, 0, 0)),
                  pl.BlockSpec((1, LANES), lambda h, b, i: (0, 0)),
                  pl.BlockSpec((1, LANES), lambda h, b, i: (0, 0))],
        out_specs=pl.BlockSpec((1, TM, LANES), lambda h, b, i: (b, i, h)),
        out_shape=jax.ShapeDtypeStruct((bsz, t, d), BF16),
        scratch_shapes=[pltpu.VMEM((t, LANES), BF16)],
        compiler_params=_params("arbitrary", "arbitrary", "arbitrary"),
        name="na_attn",
    )(qkv, qkv, qkv, bias, gq, gk)


def _oproj_kernel(y_ref, w_ref, s_ref, g_ref, o_ref):
    o_ref[0] = s_ref[0] + g_ref[...] * _dot(y_ref[0], w_ref[...])


def _oproj(y, w, s, mod, which):
    bsz, t, d = s.shape
    return pl.pallas_call(
        _oproj_kernel,
        grid=(bsz, t // TM),
        in_specs=[pl.BlockSpec((1, TM, y.shape[-1]), lambda b, i: (b, i, 0)),
                  pl.BlockSpec(w.shape, lambda b, i: (0, 0)),
                  pl.BlockSpec((1, TM, d), lambda b, i: (b, i, 0)),
                  _mod_spec_d(bsz, which, d)],
        out_specs=pl.BlockSpec((1, TM, d), lambda b, i: (b, i, 0)),
        out_shape=jax.ShapeDtypeStruct(s.shape, F32),
        input_output_aliases={2: 0},
        compiler_params=_params("arbitrary", "arbitrary"),
        name="oproj",
    )(y, w, s, mod)


def _rope_tables(n_ctx, seq):
    lane = np.arange(LANES)
    dd = lane % ML_DQK
    grp = dd // (ML_DQK // 2)
    j = dd % (ML_DQK // 2)
    n = ML_DQK // 4
    inv = ROPE_THETA ** (-(j % n).astype(np.float64) / n)
    pos = np.arange(seq)
    p = np.where(grp[None, :] == 0, (pos // GRID_W)[:, None], (pos % GRID_W)[:, None]).astype(np.float64)
    ang = p * inv[None, :]
    cos = np.cos(ang)
    sin = np.where(j[None, :] < n, -np.sin(ang), np.sin(ang))
    cos = np.concatenate([np.ones((n_ctx, LANES)), cos], axis=0)
    sin = np.concatenate([np.zeros((n_ctx, LANES)), sin], axis=0)
    return jnp.asarray(cos, F32), jnp.asarray(sin, F32)


def _ml_proj_kernel(x_ref, g_ref, sh_ref, sc_ref, wm_ref, wg_ref, wgt_ref, bgr_ref, bgc_ref, cos_ref, sin_ref,
                    q_ref, k_ref, v_ref, og_ref, gr_ref, gc_ref):
    h = _norm_mod(x_ref[0], g_ref[...], sh_ref[...], sc_ref[...]).astype(BF16)
    main = _dot(h, wm_ref[...])
    nq = q_ref.shape[-1]
    nv = v_ref.shape[-1]
    cos = cos_ref[...]
    sin = sin_ref[...]
    n = ML_DQK // 4
    lane = lax.broadcasted_iota(I32, (1, LANES), 1)
    low = (lane % (2 * n)) < n

    def rope(z):
        zr = jnp.where(low, pltpu.roll(z, LANES - n, 1), pltpu.roll(z, n, 1))
        return z * cos + zr * sin

    for j in range(nq // LANES):
        sl = slice(j * LANES, (j + 1) * LANES)
        q_ref[0, :, sl] = (rope(main[:, sl]) * (ML_DQK ** -0.5)).astype(BF16)
        k_ref[0, :, sl] = rope(main[:, nq + j * LANES: nq + (j + 1) * LANES]).astype(BF16)
    v_ref[0] = main[:, 2 * nq: 2 * nq + nv].astype(BF16)
    og_ref[0] = main[:, 2 * nq + nv:].astype(BF16)

    nh = ML_HEADS
    ti = lax.broadcasted_iota(I32, (TM, TM), 0)
    si = lax.broadcasted_iota(I32, (TM, TM), 1)
    le = (si <= ti).astype(BF16)
    ge = (si >= ti).astype(BF16)

    def capped(pre):
        return ML_GATE_CAP * jnp.tanh(pre / ML_GATE_CAP)

    def exact_cum(tri, x, nt):
        parts = _split3(x)
        if nt:
            return sum(_dot_nt(p, tri) for p in parts)
        return sum(_dot(tri, p) for p in parts)

    gc = capped(_dot(h, wg_ref[...]) + bgr_ref[...])
    lf = jax.nn.log_sigmoid(gc)
    pre = exact_cum(le, lf, False)
    suf = exact_cum(ge, lf, False)
    cum = jnp.where(lane < 2 * nh, pre, suf)
    r = gc - pltpu.roll(cum, LANES - nh, 1)
    is_i = (lane // nh) % 2 == 0
    gc_ref[0] = jnp.where(is_i, r, cum)[:, :4 * nh]

    gr = capped(_dot_nt(wgt_ref[...], h) + bgc_ref[...])
    lfr = jax.nn.log_sigmoid(gr)
    pre_r = exact_cum(le, lfr, True)
    suf_r = exact_cum(ge, lfr, True)
    gr_ref[0, 0 * nh:1 * nh, :] = gr[0 * nh:1 * nh] - pre_r[1 * nh:2 * nh]
    gr_ref[0, 1 * nh:2 * nh, :] = pre_r[1 * nh:2 * nh]
    gr_ref[0, 2 * nh:3 * nh, :] = gr[2 * nh:3 * nh] - suf_r[3 * nh:4 * nh]
    gr_ref[0, 3 * nh:4 * nh, :] = suf_r[3 * nh:4 * nh]


def _ml_proj(s, gain, mod, w_in, b_gate, n_ctx):
    bsz, t, d = s.shape
    nq = ML_HEADS * ML_DQK
    nv = ML_HEADS * ML_DV
    ng = 4 * ML_HEADS
    assert TM == ML_CHUNK
    wm = w_in[:, :2 * nq + nv + d].astype(BF16)
    wg = w_in[:, 2 * nq + nv + d:]
    wg_pad = jnp.pad(wg, ((0, 0), (0, LANES - ng))).astype(BF16)
    wgt = wg.T.astype(BF16)
    bgr = jnp.pad(b_gate, (0, LANES - ng)).reshape(1, LANES)
    bgc = b_gate.reshape(ng, 1)
    cos, sin = _rope_tables(n_ctx, t - n_ctx)
    tile = lambda n: pl.BlockSpec((1, TM, n), lambda b, i: (b, i, 0))
    const = lambda a: pl.BlockSpec(a.shape, lambda b, i: (0, 0))
    return pl.pallas_call(
        _ml_proj_kernel,
        grid=(bsz, t // TM),
        in_specs=[tile(d), pl.BlockSpec((1, d), lambda b, i: (0, 0)),
                  _mod_spec_d(bsz, 0, d), _mod_spec_d(bsz, 1, d),
                  const(wm), const(wg_pad), const(wgt), const(bgr), const(bgc),
                  pl.BlockSpec((TM, LANES), lambda b, i: (i, 0)),
                  pl.BlockSpec((TM, LANES), lambda b, i: (i, 0))],
        out_specs=[tile(nq), tile(nq), tile(nv), tile(d),
                   pl.BlockSpec((1, ng, TM), lambda b, i: (b, 0, i)),
                   tile(ng)],
        out_shape=[jax.ShapeDtypeStruct((bsz, t, nq), BF16), jax.ShapeDtypeStruct((bsz, t, nq), BF16),
                   jax.ShapeDtypeStruct((bsz, t, nv), BF16), jax.ShapeDtypeStruct((bsz, t, d), BF16),
                   jax.ShapeDtypeStruct((bsz, ng, t), F32), jax.ShapeDtypeStruct((bsz, t, ng), F32)],
        compiler_params=_params("arbitrary", "arbitrary"),
        name="ml_proj",
    )(s, gain.reshape(1, d), mod, mod, wm, wg_pad, wgt, bgr, bgc, cos, sin)


def _mlstm_kernel(q_ref, k_ref, v_ref, og_ref, gr_ref, gc_ref, mg_ref, o_ref, hf_ref, hb_ref, c_ref):
    L = ML_CHUNK
    nchunk = q_ref.shape[1] // L
    first = lax.broadcasted_iota(I32, (1, LANES), 1) < ML_DQK
    ti = lax.broadcasted_iota(I32, (L, L), 0)
    si = lax.broadcasted_iota(I32, (L, L), 1)
    masks = (si <= ti, si >= ti)
    ones_col = (lax.broadcasted_iota(I32, (L, ML_DV), 1) == 0).astype(BF16)

    c_ref[...] = jnp.zeros_like(c_ref)

    def chunk(c, a, bw, m0):
        off = pl.multiple_of(c * L, L)
        sel = first if a == 0 else jnp.logical_not(first)
        q = q_ref[0, pl.ds(off, L), :]
        k = k_ref[0, pl.ds(off, L), :]
        qa = jnp.where(sel, q, jnp.zeros_like(q))
        ka = jnp.where(sel, k, jnp.zeros_like(k))
        v = v_ref[0, pl.ds(off, L), a * ML_DV:(a + 1) * ML_DV]
        w = 2 if bw else 0
        r_row = gr_ref[0, 0, 2 * w + a: 2 * w + a + 1, pl.ds(off, L)]
        cum_row = gr_ref[0, 0, 2 * (w + 1) + a: 2 * (w + 1) + a + 1, pl.ds(off, L)]
        r_col = gc_ref[0, 0, pl.ds(off, L), 2 * w + a: 2 * w + a + 1]
        cum_col = gc_ref[0, 0, pl.ds(off, L), 2 * (w + 1) + a: 2 * (w + 1) + a + 1]
        tot = cum_row[:, 0:1] if bw else cum_row[:, L - 1:L]
        mask = masks[1 if bw else 0]
        rb = jnp.broadcast_to(r_row, (L, L))
        mx = jnp.max(jnp.where(mask, rb, NEG), axis=1, keepdims=True)
        mm = jnp.maximum(m0, mx)
        e = jnp.exp(jnp.where(mask, rb - mm, NEG))
        s = (_dot_nt(qa, k) * e).astype(BF16)
        v2 = jnp.concatenate([v, ones_col], axis=1)
        idx = 2 * a + (1 if bw else 0)
        cst = c_ref[idx]
        both = _dot(s, v2) + jnp.exp(m0 - mm) * _dot(qa, cst.astype(BF16))
        num = both[:, :ML_DV]
        den = both[:, ML_DV:ML_DV + 1]
        hout = num / jnp.maximum(jnp.abs(den), jnp.exp(-cum_col - mm))
        dst = hb_ref if bw else hf_ref
        dst[pl.ds(off, L), a * ML_DV:(a + 1) * ML_DV] = hout
        m_loc = tot + jnp.max(r_row, axis=1, keepdims=True)
        x = (v2.astype(F32) * jnp.exp(tot + r_col - m_loc)).astype(BF16)
        u = _dot_tn(ka, x)
        m_new = jnp.maximum(tot + m0, m_loc)
        c_ref[idx] = jnp.exp(tot + m0 - m_new) * cst + jnp.exp(m_loc - m_new) * u
        return m_new

    def body(i, ms):
        cb = jnp.where(i == 0, 0, nchunk - i)
        return (chunk(i, 0, False, ms[0]), chunk(i, 1, False, ms[1]),
                chunk(cb, 0, True, ms[2]), chunk(cb, 1, True, ms[3]))

    z = jnp.zeros((1, 1), F32)
    lax.fori_loop(0, nchunk, body, (z, z, z, z))

    def epilogue(i, carry):
        off = pl.multiple_of(i * L, L)
        for a in range(2):
            sl = slice(a * ML_DV, (a + 1) * ML_DV)
            hs = hf_ref[pl.ds(off, L), sl] + hb_ref[pl.ds(off, L), sl]
            hn = hs * lax.rsqrt(jnp.mean(hs * hs, axis=-1, keepdims=True) + EPS) * mg_ref[:, sl]
            o_ref[0, pl.ds(off, L), sl] = (hn * jax.nn.sigmoid(og_ref[0, pl.ds(off, L), sl].astype(F32))).astype(BF16)
        return carry

    lax.fori_loop(0, nchunk, epilogue, 0)


def _mlstm(q, k, v, og, gr, gc, mh_gain):
    bsz, t, _ = q.shape
    hp = ML_HEADS // 2
    nv = ML_HEADS * ML_DV
    w2 = 2 * ML_DV
    gr = gr.reshape(bsz, 4, hp, 2, t).transpose(0, 2, 1, 3, 4).reshape(bsz, hp, 8, t)
    gc = gc.reshape(bsz, t, 4, hp, 2).transpose(0, 3, 1, 2, 4).reshape(bsz, hp, t, 8)
    return pl.pallas_call(
        _mlstm_kernel,
        grid=(bsz, hp),
        in_specs=[pl.BlockSpec((1, t, LANES), lambda b, h: (b, 0, h)),
                  pl.BlockSpec((1, t, LANES), lambda b, h: (b, 0, h)),
                  pl.BlockSpec((1, t, w2), lambda b, h: (b, 0, h)),
                  pl.BlockSpec((1, t, w2), lambda b, h: (b, 0, h)),
                  pl.BlockSpec((1, 1, 8, t), lambda b, h: (b, h, 0, 0)),
                  pl.BlockSpec((1, 1, t, 8), lambda b, h: (b, h, 0, 0)),
                  pl.BlockSpec((1, w2), lambda b, h: (0, h))],
        out_specs=pl.BlockSpec((1, t, w2), lambda b, h: (b, 0, h)),
        out_shape=jax.ShapeDtypeStruct((bsz, t, nv), BF16),
        scratch_shapes=[pltpu.VMEM((t, w2), F32), pltpu.VMEM((t, w2), F32),
                        pltpu.VMEM((4, LANES, w2), F32)],
        compiler_params=_params("arbitrary", "arbitrary"),
        name="mlstm",
    )(q, k, v, og, gr, gc, mh_gain.reshape(1, nv))


def _moe_pre_kernel(x_ref, g_ref, sh_ref, sc_ref, whi_ref, wlo_ref, br_ref,
                    h_ref, id_ref, wt_ref, rk_ref, cnt_ref):
    step = pl.program_id(0) * pl.num_programs(1) + pl.program_id(1)

    @pl.when(step == 0)
    def _():
        cnt_ref[...] = jnp.zeros_like(cnt_ref)

    h = _norm_mod(x_ref[0], g_ref[...], sh_ref[...], sc_ref[...])
    h_ref[0] = h
    hhi = h.astype(BF16)
    hlo = (h - hhi.astype(F32)).astype(BF16)
    whi = whi_ref[...]
    logits = _dot_nt(whi, hhi) + _dot_nt(whi, hlo) + _dot_nt(wlo_ref[...], hhi) + br_ref[...]
    ne = logits.shape[0]
    eidx = lax.broadcasted_iota(I32, logits.shape, 0)
    sels, vals, ids = [], [], []
    l = logits
    for _ in range(TOP_K):
        mval = jnp.max(l, axis=0, keepdims=True)
        idx = jnp.min(jnp.where(l == mval, eidx, ne), axis=0, keepdims=True)
        sel = eidx == idx
        l = jnp.where(sel, -jnp.inf, l)
        sels.append(sel)
        vals.append(mval)
        ids.append(idx)
    ex = [jnp.exp(v - vals[0]) for v in vals]
    tot = ex[0] + ex[1] + ex[2] + ex[3]
    member = sels[0] | sels[1] | sels[2] | sels[3]
    ti = lax.broadcasted_iota(I32, (TM, TM), 0)
    si = lax.broadcasted_iota(I32, (TM, TM), 1)
    before = (ti < si).astype(BF16)
    rank_e = _dot(member.astype(BF16), before) + cnt_ref[:, 0:1]
    for kk in range(TOP_K):
        id_ref[kk:kk + 1, :] = ids[kk]
        wt_ref[kk:kk + 1, :] = ex[kk] / tot
        rk_ref[kk:kk + 1, :] = jnp.sum(jnp.where(sels[kk], rank_e, 0.0), axis=0, keepdims=True).astype(I32)
    cnt_ref[...] = cnt_ref[...] + jnp.sum(member.astype(F32), axis=1, keepdims=True)


def _moe_pre(s, gain, mod, w_router, b_router):
    bsz, t, d = s.shape
    n = bsz * t
    nt = t // TM
    wt = w_router.T
    whi = wt.astype(BF16)
    wlo = (wt - whi.astype(F32)).astype(BF16)
    tok = lambda dt: jax.ShapeDtypeStruct((TOP_K, n), dt)
    tok_spec = pl.BlockSpec((TOP_K, TM), lambda b, i: (0, b * nt + i))
    return pl.pallas_call(
        _moe_pre_kernel,
        grid=(bsz, nt),
        in_specs=[pl.BlockSpec((1, TM, d), lambda b, i: (b, i, 0)),
                  pl.BlockSpec((1, d), lambda b, i: (0, 0)),
                  _mod_spec_d(bsz, 3, d), _mod_spec_d(bsz, 4, d),
                  pl.BlockSpec(whi.shape, lambda b, i: (0, 0)),
                  pl.BlockSpec(wlo.shape, lambda b, i: (0, 0)),
                  pl.BlockSpec((N_EXPERTS, 1), lambda b, i: (0, 0))],
        out_specs=[pl.BlockSpec((1, TM, d), lambda b, i: (b, i, 0)), tok_spec, tok_spec, tok_spec,
                   pl.BlockSpec((N_EXPERTS, LANES), lambda b, i: (0, 0))],
        out_shape=[jax.ShapeDtypeStruct((bsz, t, d), F32), tok(I32), tok(F32), tok(I32),
                   jax.ShapeDtypeStruct((N_EXPERTS, LANES), F32)],
        compiler_params=_params("arbitrary", "arbitrary"),
        name="moe_pre",
    )(s, gain.reshape(1, d), mod, mod, whi, wlo, b_router.reshape(N_EXPERTS, 1))


def _dispatch_kernel(last_ref, slot_ref, h_ref, xs_ref, zero_ref, sem, zsem):
    step = pl.program_id(0)
    n_exp = last_ref.shape[0] - 1
    n_tiles = xs_ref.shape[0] // TM

    def zero_copy(row):
        return pltpu.make_async_copy(zero_ref, xs_ref.at[pl.ds(pl.multiple_of(row, TM), TM)], zsem)

    @pl.when(step == 0)
    def _():
        zero_ref[...] = jnp.zeros_like(zero_ref)
        for e in range(n_exp):
            @pl.when(last_ref[e] >= 0)
            def _():
                zero_copy(last_ref[e]).start()
        for e in range(n_exp):
            @pl.when(last_ref[e] >= 0)
            def _():
                zero_copy(last_ref[e]).wait()

        def tail(i, carry):
            cp = zero_copy(i * TM)
            cp.start()
            cp.wait()
            return carry
        lax.fori_loop(last_ref[n_exp] // TM, n_tiles, tail, 0)

    def row_copy(j, kk):
        return pltpu.make_async_copy(h_ref.at[pl.ds(j, 1)], xs_ref.at[pl.ds(slot_ref[0, kk, j], 1)], sem)

    def issue(j, carry):
        for kk in range(TOP_K):
            row_copy(j, kk).start()
        return carry

    def drain(j, carry):
        for kk in range(TOP_K):
            row_copy(j, kk).wait()
        return carry

    lax.fori_loop(0, TM, issue, 0)
    lax.fori_loop(0, TM, drain, 0)


def _dispatch(h, slots, last_tile, n_slots):
    n, d = h.shape
    nt = n // TM
    slots3 = slots.reshape(TOP_K, nt, TM).transpose(1, 0, 2)
    return pl.pallas_call(
        _dispatch_kernel,
        grid_spec=pltpu.PrefetchScalarGridSpec(
            num_scalar_prefetch=1,
            grid=(nt,),
            in_specs=[pl.BlockSpec((1, TOP_K, TM), lambda i, last: (i, 0, 0), memory_space=pltpu.SMEM),
                      pl.BlockSpec((TM, d), lambda i, last: (i, 0))],
            out_specs=pl.BlockSpec(memory_space=pl.ANY),
            scratch_shapes=[pltpu.VMEM((TM, d), F32), pltpu.SemaphoreType.DMA, pltpu.SemaphoreType.DMA]),
        out_shape=jax.ShapeDtypeStruct((n_slots, d), F32),
        compiler_params=_params("arbitrary"),
        name="moe_dispatch",
    )(last_tile, slots3, h)


def _ffn_kernel(te_ref, nu_ref, x_ref, w1_ref, b1_ref, w2_ref, b2_ref, y_ref, w1b_ref, w2b_ref):
    i = pl.program_id(0)

    @pl.when(jnp.logical_or(i == 0, te_ref[i] != te_ref[jnp.maximum(i - 1, 0)]))
    def _():
        w1b_ref[...] = w1_ref[0].astype(BF16)
        w2b_ref[...] = w2_ref[0].astype(BF16)

    @pl.when(i < nu_ref[0])
    def _():
        dff = w2_ref.shape[1]
        u = _dot(x_ref[...].astype(BF16), w1b_ref[...]) + b1_ref[0]
        glu = jnp.minimum(u[:, :dff], SWIGLU_LIMIT)
        lin = jnp.clip(u[:, dff:], -SWIGLU_LIMIT, SWIGLU_LIMIT)
        act = glu * jax.nn.sigmoid(SWIGLU_ALPHA * glu) * (lin + 1.0)
        y_ref[...] = _dot(act.astype(BF16), w2b_ref[...]) + b2_ref[0]

    @pl.when(i >= nu_ref[0])
    def _():
        y_ref[...] = jnp.zeros_like(y_ref)


def _ffn(xs, tile_expert, n_used, w1, b1, w2, b2):
    ns, d = xs.shape
    ne, _, n1 = w1.shape
    row = lambda i, te, nu: (i, 0)
    exp3 = lambda i, te, nu: (te[i], 0, 0)
    return pl.pallas_call(
        _ffn_kernel,
        grid_spec=pltpu.PrefetchScalarGridSpec(
            num_scalar_prefetch=2,
            grid=(ns // TM,),
            in_specs=[pl.BlockSpec((TM, d), row),
                      pl.BlockSpec((1, d, n1), exp3), pl.BlockSpec((1, 1, n1), exp3),
                      pl.BlockSpec((1, w2.shape[1], d), exp3), pl.BlockSpec((1, 1, d), exp3)],
            out_specs=pl.BlockSpec((TM, d), row),
            scratch_shapes=[pltpu.VMEM((d, n1), BF16), pltpu.VMEM((w2.shape[1], d), BF16)]),
        out_shape=jax.ShapeDtypeStruct((ns, d), F32),
        compiler_params=_params("arbitrary"),
        name="moe_ffn",
    )(tile_expert, n_used, xs, w1, b1.reshape(ne, 1, n1), w2, b2.reshape(ne, 1, d))


def _combine_kernel(slot_ref, wt_ref, s_ref, g_ref, ys_ref, o_ref, buf_ref, sem):
    def row_copy(j, kk):
        return pltpu.make_async_copy(ys_ref.at[pl.ds(slot_ref[0, kk, j], 1)], buf_ref.at[kk, pl.ds(j, 1)], sem)

    def issue(j, carry):
        for kk in range(TOP_K):
            row_copy(j, kk).start()
        return carry

    def drain(j, carry):
        for kk in range(TOP_K):
            row_copy(j, kk).wait()
        return carry

    lax.fori_loop(0, TM, issue, 0)
    lax.fori_loop(0, TM, drain, 0)
    wt = wt_ref[...]
    f = wt[:, 0:1] * buf_ref[0]
    for kk in range(1, TOP_K):
        f = f + wt[:, kk:kk + 1] * buf_ref[kk]
    o_ref[0] = s_ref[0] + g_ref[...] * f


def _combine(ys, slots, wts, s, mod, which):
    bsz, t, d = s.shape
    nt = t // TM
    n = bsz * t
    slots3 = slots.reshape(TOP_K, n // TM, TM).transpose(1, 0, 2)
    return pl.pallas_call(
        _combine_kernel,
        grid=(bsz, nt),
        in_specs=[pl.BlockSpec((1, TOP_K, TM), lambda b, i: (b * nt + i, 0, 0), memory_space=pltpu.SMEM),
                  pl.BlockSpec((TM, TOP_K), lambda b, i: (b * nt + i, 0)),
                  pl.BlockSpec((1, TM, d), lambda b, i: (b, i, 0)),
                  _mod_spec_d(bsz, which, d),
                  pl.BlockSpec(memory_space=pl.ANY)],
        out_specs=pl.BlockSpec((1, TM, d), lambda b, i: (b, i, 0)),
        out_shape=jax.ShapeDtypeStruct(s.shape, F32),
        scratch_shapes=[pltpu.VMEM((TOP_K, TM, d), F32), pltpu.SemaphoreType.DMA],
        input_output_aliases={2: 0},
        compiler_params=_params("arbitrary", "arbitrary"),
        name="moe_combine",
    )(slots3, wts.T, s, mod, ys)


def _moe(s, gain, mod, w_router, b_router, w1, b1, w2, b2):
    bsz, t, d = s.shape
    n = bsz * t
    h, ids, wts, ranks, counts = _moe_pre(s, gain, mod, w_router, b_router)
    cnt = counts[:, 0].astype(I32)
    padded = ((cnt + TM - 1) // TM) * TM
    ends = jnp.cumsum(padded)
    offs = ends - padded
    n_tiles = (TOP_K * n + N_EXPERTS * (TM - 1) + TM - 1) // TM
    n_used = (ends[-1] // TM).astype(I32)
    tile_start = jnp.arange(n_tiles, dtype=I32) * TM
    te = jnp.sum(tile_start[:, None] >= ends[None, :], axis=1).astype(I32)
    te_last = jnp.max(jnp.where(cnt > 0, jnp.arange(N_EXPERTS, dtype=I32), 0))
    te = jnp.minimum(te, te_last)
    slots = ranks + jnp.sum(jnp.where(ids[..., None] == jnp.arange(N_EXPERTS, dtype=I32), offs, 0), axis=-1)
    last_tile = jnp.concatenate([jnp.where(cnt > 0, ends - TM, -1), ends[-1:]]).astype(I32)
    xs = _dispatch(h.reshape(n, d), slots, last_tile, n_tiles * TM)
    ys = _ffn(xs, te, n_used.reshape(1), w1, b1, w2, b2)
    return _combine(ys, slots, wts, s, mod, 5)


def kernel(x, c, ctx, c_ctx, w_ada, b_ada, g_norm_mix, g_norm_ffn, na_w_qkv, na_q_gain, na_k_gain, na_rpb,
           na_w_o, ml_w_in, ml_b_gate, ml_mh_gain, ml_w_o, moe_w_router, moe_b_router, moe_w1, moe_b1,
           moe_w2, moe_b2):
    bsz, seq, d = x.shape
    n_ctx = ctx.shape[1]
    depth = w_ada.shape[0]
    rows = seq // GRID_W
    n_mod_rows = ((bsz + 1 + 7) // 8) * 8
    cc = jnp.concatenate([c, c_ctx[None, :], jnp.zeros((n_mod_rows - bsz - 1, d), F32)], axis=0)
    mod_all = _ada(cc, w_ada, b_ada).reshape(depth, n_mod_rows, 6, 1, d)
    s = jnp.concatenate([ctx, x], axis=1)
    for i in range(depth):
        j = i // 2
        mod = mod_all[i]
        if i % 2 == 0:
            qkv = _na_qkv(s, g_norm_mix[i], mod, na_w_qkv[j].astype(BF16))
            y = _na_attn(qkv, _na_bias(na_rpb[j], rows), na_q_gain[j], na_k_gain[j], n_ctx)
            s = _oproj(y, na_w_o[j].astype(BF16), s, mod, 2)
        else:
            q, k, v, og, gr, gc = _ml_proj(s, g_norm_mix[i], mod, ml_w_in[j], ml_b_gate[j], n_ctx)
            y = _mlstm(q, k, v, og, gr, gc, ml_mh_gain[j])
            s = _oproj(y, ml_w_o[j].astype(BF16), s, mod, 2)
        s = _moe(s, g_norm_ffn[i], mod, moe_w_router[i], moe_b_router[i],
                 moe_w1[i], moe_b1[i], moe_w2[i], moe_b2[i])
    return s[:, n_ctx:]
```

```python
import functools
import math

import numpy as np
import jax
import jax.numpy as jnp
from jax import lax
from jax.experimental import pallas as pl
from jax.experimental.pallas import tpu as pltpu

F32 = jnp.float32
BF16 = jnp.bfloat16
I32 = jnp.int32

EPS = 1e-6
GRID_W = 64
NA_HEADS = 16
NA_HEAD_DIM = 64
NA_KR = 8
NA_KC = 16
NA_QROWS = 4
NA_KROWS = NA_QROWS + NA_KR - 1
ML_HEADS = 8
ML_DV = 128
ML_DQK = 64
ML_CHUNK = 256
ML_GATE_CAP = 15.0
ROPE_THETA = 10000.0
N_EXPERTS = 32
TOP_K = 4
SWIGLU_LIMIT = 7.0
SWIGLU_ALPHA = 1.702

TM = 256
TMX = 512
LANES = 128
NEG = -1e30
VMEM_LIMIT = 48 * 1024 * 1024
FFN_VMEM_LIMIT = 56 * 1024 * 1024


def _dot(a, b):
    return jnp.dot(a, b, preferred_element_type=F32)


def _dot_nt(a, b):
    return lax.dot_general(a, b, (((1,), (1,)), ((), ())), preferred_element_type=F32)


def _split3(x):
    hi = x.astype(BF16)
    r1 = x - hi.astype(F32)
    mid = r1.astype(BF16)
    lo = (r1 - mid.astype(F32)).astype(BF16)
    return hi, mid, lo


def _params(*sem):
    return pltpu.CompilerParams(dimension_semantics=sem, vmem_limit_bytes=VMEM_LIMIT)


def _norm_mod(x, gain, shift, scale):
    r = lax.rsqrt(jnp.mean(x * x, axis=-1, keepdims=True) + EPS)
    return (x * r * gain) * (1.0 + scale) + shift


def _mod_spec_d(n_batch, which, d):
    return pl.BlockSpec((None, None, 1, d), lambda b, t: (jnp.where(t == 0, n_batch, b), which, 0, 0))


def _ada_kernel(c_ref, w_ref, b_ref, o_ref):
    c = c_ref[...]
    s = c * jax.nn.sigmoid(c)
    o_ref[0] = _dot(s.astype(BF16), w_ref[0].astype(BF16)) + b_ref[0]


def _ada(cc, w_ada, b_ada):
    depth, d, n = w_ada.shape
    bn = 1536
    return pl.pallas_call(
        _ada_kernel,
        grid=(depth, n // bn),
        in_specs=[pl.BlockSpec((cc.shape[0], d), lambda l, j: (0, 0)),
                  pl.BlockSpec((1, d, bn), lambda l, j: (l, 0, j)),
                  pl.BlockSpec((1, 1, bn), lambda l, j: (l, 0, j))],
        out_specs=pl.BlockSpec((1, cc.shape[0], bn), lambda l, j: (l, 0, j)),
        out_shape=jax.ShapeDtypeStruct((depth, cc.shape[0], n), F32),
        compiler_params=_params("arbitrary", "arbitrary"),
        name="ada",
    )(cc, w_ada, b_ada.reshape(depth, 1, n))


def _na_qkv_kernel(x_ref, g_ref, sh_ref, sc_ref, w_ref, o_ref):
    h = _norm_mod(x_ref[0], g_ref[...], sh_ref[...], sc_ref[...])
    o_ref[0] = _dot(h.astype(BF16), w_ref[...]).astype(BF16)


def _na_qkv(s, gain, mod, w):
    bsz, t, d = s.shape
    n = w.shape[1]
    return pl.pallas_call(
        _na_qkv_kernel,
        grid=(bsz, t // TM),
        in_specs=[pl.BlockSpec((1, TM, d), lambda b, i: (b, i, 0)),
                  pl.BlockSpec((1, d), lambda b, i: (0, 0)),
                  _mod_spec_d(bsz, 0, d), _mod_spec_d(bsz, 1, d),
                  pl.BlockSpec((d, n), lambda b, i: (0, 0))],
        out_specs=pl.BlockSpec((1, TM, n), lambda b, i: (b, i, 0)),
        out_shape=jax.ShapeDtypeStruct((bsz, t, n), BF16),
        compiler_params=_params("arbitrary", "arbitrary"),
        name="na_qkv",
    )(s, gain.reshape(1, d), mod, mod, w)


def _na_bias(rpb, rows):
    nblk = rows // NA_QROWS
    kmax = rows - NA_KROWS
    nrr, ncc = 2 * NA_KR - 1, 2 * NA_KC - 1
    qc = np.arange(GRID_W)[:, None]
    kc = np.arange(GRID_W)[None, :]
    c0 = np.clip(qc - NA_KC // 2, 0, GRID_W - NA_KC)
    col_ok = (kc >= c0) & (kc < c0 + NA_KC)
    col_pick = np.eye(ncc)[np.clip(kc - qc + NA_KC - 1, 0, ncc - 1)]
    row_pick, valid = [], []
    for blk in (0, 1, nblk - 1):
        rs = blk * NA_QROWS
        ks = min(max(rs - NA_KR // 2, 0), kmax)
        qr = rs + np.arange(NA_QROWS)[:, None]
        kr = ks + np.arange(NA_KROWS)[None, :]
        r0 = np.clip(qr - NA_KR // 2, 0, rows - NA_KR)
        row_ok = (kr >= r0) & (kr < r0 + NA_KR)
        row_pick.append(np.eye(nrr)[np.clip(kr - qr + NA_KR - 1, 0, nrr - 1)])
        valid.append(row_ok[:, None, :, None] & col_ok[None, :, None, :])
    bias = jnp.einsum('vrka,hab,qcb->hvrqkc', jnp.asarray(np.stack(row_pick), F32), rpb,
                      jnp.asarray(col_pick, F32), precision=lax.Precision.HIGHEST)
    bias = jnp.where(np.stack(valid)[None], bias, NEG)
    nq, nk = NA_QROWS * GRID_W, NA_KROWS * GRID_W
    return bias.reshape(NA_HEADS // 2, 2, 3, nq, nk).transpose(0, 2, 1, 3, 4).astype(F32)


def _na_attn_kernel(q_ref, k_ref, v_ref, bias_ref, qg_ref, kg_ref, o_ref, kn_ref, *, n_ctx, rows):
    t = pl.program_id(2)
    first = lax.broadcasted_iota(I32, (1, LANES), 1) < NA_HEAD_DIM
    nq = NA_QROWS * GRID_W
    nk = NA_KROWS * GRID_W

    def head_rms(z, gain):
        z2 = z * z
        sa = jnp.sum(jnp.where(first, z2, 0.0), axis=-1, keepdims=True)
        sb = jnp.sum(jnp.where(first, 0.0, z2), axis=-1, keepdims=True)
        r = jnp.where(first, lax.rsqrt(sa / NA_HEAD_DIM + EPS), lax.rsqrt(sb / NA_HEAD_DIM + EPS))
        return z * r * gain

    @pl.when(t == 0)
    def _():
        def body(i, carry):
            off = pl.multiple_of(i * TM, TM)
            kn_ref[pl.ds(off, TM), :] = head_rms(k_ref[0, pl.ds(off, TM), :].astype(F32), kg_ref[...]).astype(BF16)
            return carry
        lax.fori_loop(0, k_ref.shape[1] // TM, body, 0)

    qn = (head_rms(q_ref[0].astype(F32), qg_ref[...]) * (NA_HEAD_DIM ** -0.5)).astype(BF16)
    zero = jnp.zeros_like(qn)
    q_heads = (jnp.where(first, qn, zero), jnp.where(first, zero, qn))
    kc = kn_ref[0:n_ctx, :]
    vc = v_ref[0, 0:n_ctx, :]

    def finish(outs):
        o_ref[0] = jnp.where(first, outs[0], outs[1]).astype(o_ref.dtype)

    @pl.when(t == 0)
    def _():
        outs = []
        for qh in q_heads:
            s = _dot_nt(qh, kc)
            m = jnp.max(s, axis=-1, keepdims=True)
            p = jnp.exp(s - m)
            outs.append(_dot(p.astype(BF16), vc) / jnp.sum(p, axis=-1, keepdims=True))
        finish(outs)

    @pl.when(t > 0)
    def _():
        blk = t - 1
        nblk = rows // NA_QROWS
        ks = jnp.clip(blk * NA_QROWS - NA_KR // 2, 0, rows - NA_KROWS)
        start = pl.multiple_of(n_ctx + ks * GRID_W, GRID_W)
        kw = kn_ref[pl.ds(start, nk), :]
        vw = v_ref[0, pl.ds(start, nk), :]
        kind = jnp.where(blk == 0, 0, jnp.where(blk == nblk - 1, 2, 1))
        outs = []
        for hh, qh in enumerate(q_heads):
            sw = _dot_nt(qh, kw) + bias_ref[0, kind, hh]
            sc = _dot_nt(qh, kc)
            m = jnp.maximum(jnp.max(sw, axis=-1, keepdims=True), jnp.max(sc, axis=-1, keepdims=True))
            pw = jnp.exp(sw - m)
            pc = jnp.exp(sc - m)
            l = jnp.sum(pw, axis=-1, keepdims=True) + jnp.sum(pc, axis=-1, keepdims=True)
            outs.append((_dot(pw.astype(BF16), vw) + _dot(pc.astype(BF16), vc)) / l)
        finish(outs)


def _na_attn(qkv, bias, q_gain, k_gain, n_ctx):
    bsz, t, _ = qkv.shape
    d = NA_HEADS * NA_HEAD_DIM
    rows = (t - n_ctx) // GRID_W
    assert n_ctx == TM and NA_QROWS * GRID_W == TM and rows % NA_QROWS == 0 and rows >= NA_KROWS
    hp = NA_HEADS // 2
    nq, nk = bias.shape[-2:]
    gq = jnp.tile(q_gain, 2).reshape(1, LANES)
    gk = jnp.tile(k_gain, 2).reshape(1, LANES)
    return pl.pallas_call(
        functools.partial(_na_attn_kernel, n_ctx=n_ctx, rows=rows),
        grid=(hp, bsz, t // TM),
        in_specs=[pl.BlockSpec((1, TM, LANES), lambda h, b, i: (b, i, h)),
                  pl.BlockSpec((1, t, LANES), lambda h, b, i: (b, 0, hp + h)),
                  pl.BlockSpec((1, t, LANES), lambda h, b, i: (b, 0, 2 * hp + h)),
                  pl.BlockSpec((1, 3, 2, nq, nk), lambda h, b, i: (h, 0, 0, 0, 0)),
                  pl.BlockSpec((1, LANES), lambda h, b, i: (0, 0)),
                  pl.BlockSpec((1, LANES), lambda h, b, i: (0, 0))],
        out_specs=pl.BlockSpec((1, TM, LANES), lambda h, b, i: (b, i, h)),
        out_shape=jax.ShapeDtypeStruct((bsz, t, d), BF16),
        scratch_shapes=[pltpu.VMEM((t, LANES), BF16)],
        compiler_params=_params("arbitrary", "arbitrary", "arbitrary"),
        name="na_attn",
    )(qkv, qkv, qkv, bias, gq, gk)


def _oproj_kernel(y_ref, w_ref, s_ref, g_ref, o_ref):
    o_ref[0] = s_ref[0] + g_ref[...] * _dot(y_ref[0], w_ref[...])


def _oproj(y, w, s, mod, which):
    bsz, t, d = s.shape
    return pl.pallas_call(
        _oproj_kernel,
        grid=(bsz, t // TM),
        in_specs=[pl.BlockSpec((1, TM, y.shape[-1]), lambda b, i: (b, i, 0)),
                  pl.BlockSpec(w.shape, lambda b, i: (0, 0)),
                  pl.BlockSpec((1, TM, d), lambda b, i: (b, i, 0)),
                  _mod_spec_d(bsz, which, d)],
        out_specs=pl.BlockSpec((1, TM, d), lambda b, i: (b, i, 0)),
        out_shape=jax.ShapeDtypeStruct(s.shape, F32),
        input_output_aliases={2: 0},
        compiler_params=_params("arbitrary", "arbitrary"),
        name="oproj",
    )(y, w, s, mod)


def _rope_tables(n_ctx, seq):
    lane = np.arange(LANES)
    dd = lane % ML_DQK
    grp = dd // (ML_DQK // 2)
    j = dd % (ML_DQK // 2)
    n = ML_DQK // 4
    inv = ROPE_THETA ** (-(j % n).astype(np.float64) / n)
    pos = np.arange(seq)
    p = np.where(grp[None, :] == 0, (pos // GRID_W)[:, None], (pos % GRID_W)[:, None]).astype(np.float64)
    ang = p * inv[None, :]
    cos = np.cos(ang)
    sin = np.where(j[None, :] < n, -np.sin(ang), np.sin(ang))
    cos = np.concatenate([np.ones((n_ctx, LANES)), cos], axis=0)
    sin = np.concatenate([np.zeros((n_ctx, LANES)), sin], axis=0)
    return jnp.asarray(cos, F32), jnp.asarray(sin, F32)


def _ml_proj_kernel(x_ref, g_ref, sh_ref, sc_ref, wm_ref, wg_ref, wgt_ref, bgr_ref, bgc_ref, cos_ref, sin_ref,
                    q_ref, kt_ref, v_ref, og_ref, gr_ref, gc_ref, gm_ref):
    h = _norm_mod(x_ref[0], g_ref[...], sh_ref[...], sc_ref[...]).astype(BF16)
    main = _dot(h, wm_ref[...])
    nq = q_ref.shape[-1]
    nv = v_ref.shape[-1]
    cos = cos_ref[...]
    sin = sin_ref[...]
    n = ML_DQK // 4
    lane = lax.broadcasted_iota(I32, (1, LANES), 1)
    low = (lane % (2 * n)) < n

    def rope(z):
        zr = jnp.where(low, pltpu.roll(z, LANES - n, 1), pltpu.roll(z, n, 1))
        return z * cos + zr * sin

    for j in range(nq // LANES):
        sl = slice(j * LANES, (j + 1) * LANES)
        q_ref[0, :, sl] = (rope(main[:, sl]) * (ML_DQK ** -0.5)).astype(BF16)
        kt_ref[0, sl, :] = rope(main[:, nq + j * LANES: nq + (j + 1) * LANES]).T.astype(BF16)
    v_ref[0] = main[:, 2 * nq: 2 * nq + nv].astype(BF16)
    og_ref[0] = main[:, 2 * nq + nv:].astype(BF16)

    nh = ML_HEADS
    ti = lax.broadcasted_iota(I32, (TM, TM), 0)
    si = lax.broadcasted_iota(I32, (TM, TM), 1)
    le = (si <= ti).astype(BF16)
    ge = (si >= ti).astype(BF16)

    def capped(pre):
        return ML_GATE_CAP * jnp.tanh(pre / ML_GATE_CAP)

    def exact_cum(tri, x, nt):
        parts = _split3(x)
        if nt:
            return sum(_dot_nt(p, tri) for p in parts)
        return sum(_dot(tri, p) for p in parts)

    gc = capped(_dot(h, wg_ref[...]) + bgr_ref[...])
    lf = jax.nn.log_sigmoid(gc)
    pre = exact_cum(le, lf, False)
    suf = exact_cum(ge, lf, False)
    cum = jnp.where(lane < 2 * nh, pre, suf)
    r = gc - pltpu.roll(cum, LANES - nh, 1)
    is_i = (lane // nh) % 2 == 0
    gc_ref[0] = jnp.where(is_i, r, cum)[:, :4 * nh]
    row = lax.broadcasted_iota(I32, (TM, 1), 0)
    pmax = r
    smax = r
    step = 1
    while step < TM:
        pmax = jnp.maximum(pmax, jnp.where(row >= step, pltpu.roll(pmax, step, 0), NEG))
        smax = jnp.maximum(smax, jnp.where(row < TM - step, pltpu.roll(smax, TM - step, 0), NEG))
        step *= 2
    gm_ref[0] = jnp.where(lane < 2 * nh, pmax, smax)[:, :4 * nh]

    gr = capped(_dot_nt(wgt_ref[...], h) + bgc_ref[...])
    lfr = jax.nn.log_sigmoid(gr)
    pre_r = exact_cum(le, lfr, True)
    suf_r = exact_cum(ge, lfr, True)
    gr_ref[0, 0 * nh:1 * nh, :] = gr[0 * nh:1 * nh] - pre_r[1 * nh:2 * nh]
    gr_ref[0, 1 * nh:2 * nh, :] = pre_r[1 * nh:2 * nh]
    gr_ref[0, 2 * nh:3 * nh, :] = gr[2 * nh:3 * nh] - suf_r[3 * nh:4 * nh]
    gr_ref[0, 3 * nh:4 * nh, :] = suf_r[3 * nh:4 * nh]


def _ml_proj(s, gain, mod, w_in, b_gate, n_ctx):
    bsz, t, d = s.shape
    nq = ML_HEADS * ML_DQK
    nv = ML_HEADS * ML_DV
    ng = 4 * ML_HEADS
    assert TM == ML_CHUNK
    wm = w_in[:, :2 * nq + nv + d].astype(BF16)
    wg = w_in[:, 2 * nq + nv + d:]
    wg_pad = jnp.pad(wg, ((0, 0), (0, LANES - ng))).astype(BF16)
    wgt = wg.T.astype(BF16)
    bgr = jnp.pad(b_gate, (0, LANES - ng)).reshape(1, LANES)
    bgc = b_gate.reshape(ng, 1)
    cos, sin = _rope_tables(n_ctx, t - n_ctx)
    tile = lambda n: pl.BlockSpec((1, TM, n), lambda b, i: (b, i, 0))
    const = lambda a: pl.BlockSpec(a.shape, lambda b, i: (0, 0))
    return pl.pallas_call(
        _ml_proj_kernel,
        grid=(bsz, t // TM),
        in_specs=[tile(d), pl.BlockSpec((1, d), lambda b, i: (0, 0)),
                  _mod_spec_d(bsz, 0, d), _mod_spec_d(bsz, 1, d),
                  const(wm), const(wg_pad), const(wgt), const(bgr), const(bgc),
                  pl.BlockSpec((TM, LANES), lambda b, i: (i, 0)),
                  pl.BlockSpec((TM, LANES), lambda b, i: (i, 0))],
        out_specs=[tile(nq), pl.BlockSpec((1, nq, TM), lambda b, i: (b, 0, i)), tile(nv), tile(d),
                   pl.BlockSpec((1, ng, TM), lambda b, i: (b, 0, i)),
                   tile(ng), tile(ng)],
        out_shape=[jax.ShapeDtypeStruct((bsz, t, nq), BF16), jax.ShapeDtypeStruct((bsz, nq, t), BF16),
                   jax.ShapeDtypeStruct((bsz, t, nv), BF16), jax.ShapeDtypeStruct((bsz, t, d), BF16),
                   jax.ShapeDtypeStruct((bsz, ng, t), F32), jax.ShapeDtypeStruct((bsz, t, ng), F32),
                   jax.ShapeDtypeStruct((bsz, t, ng), F32)],
        compiler_params=_params("arbitrary", "arbitrary"),
        name="ml_proj",
    )(s, gain.reshape(1, d), mod, mod, wm, wg_pad, wgt, bgr, bgc, cos, sin)


def _mlstm_kernel(q_ref, kt_ref, v_ref, og_ref, gr_ref, gc_ref, gm_ref, mg_ref, o_ref, hf_ref, hb_ref, c_ref):
    L = ML_CHUNK
    nchunk = q_ref.shape[1] // L
    first = lax.broadcasted_iota(I32, (1, LANES), 1) < ML_DQK
    first_row = lax.broadcasted_iota(I32, (LANES, 1), 0) < ML_DQK
    ti = lax.broadcasted_iota(I32, (L, L), 0)
    si = lax.broadcasted_iota(I32, (L, L), 1)
    masks = (si <= ti, si >= ti)
    ones_col = (lax.broadcasted_iota(I32, (L, ML_DV), 1) == 0).astype(BF16)

    c_ref[...] = jnp.zeros_like(c_ref)

    def chunk(c, a, bw, m0):
        off = pl.multiple_of(c * L, L)
        q = q_ref[0, pl.ds(off, L), :]
        kt = kt_ref[0, :, pl.ds(off, L)]
        qa = jnp.where(first if a == 0 else jnp.logical_not(first), q, jnp.zeros_like(q))
        kta = jnp.where(first_row if a == 0 else jnp.logical_not(first_row), kt, jnp.zeros_like(kt))
        v = v_ref[0, pl.ds(off, L), a * ML_DV:(a + 1) * ML_DV]
        w = 2 if bw else 0
        r_row = gr_ref[0, 0, 2 * w + a: 2 * w + a + 1, pl.ds(off, L)]
        cum_row = gr_ref[0, 0, 2 * (w + 1) + a: 2 * (w + 1) + a + 1, pl.ds(off, L)]
        cum_col = gc_ref[0, 0, pl.ds(off, L), 2 * (w + 1) + a: 2 * (w + 1) + a + 1]
        tot = cum_row[:, 0:1] if bw else cum_row[:, L - 1:L]
        mask = masks[1 if bw else 0]
        mx = gm_ref[0, 0, pl.ds(off, L), w + a: w + a + 1]
        mm = jnp.maximum(m0, mx)
        e = jnp.exp(jnp.where(mask, r_row - mm, NEG))
        s = (_dot(qa, kt) * e).astype(BF16)
        v2 = jnp.concatenate([v, ones_col], axis=1)
        idx = 2 * a + (1 if bw else 0)
        cst = c_ref[idx]
        both = _dot(s, v2) + jnp.exp(m0 - mm) * _dot(qa, cst.astype(BF16))
        num = both[:, :ML_DV]
        den = both[:, ML_DV:ML_DV + 1]
        hout = num / jnp.maximum(jnp.abs(den), jnp.exp(-cum_col - mm))
        dst = hb_ref if bw else hf_ref
        dst[pl.ds(off, L), a * ML_DV:(a + 1) * ML_DV] = hout
        m_loc = tot + jnp.max(r_row, axis=1, keepdims=True)
        kte = (kta.astype(F32) * jnp.exp(tot + r_row - m_loc)).astype(BF16)
        u = _dot(kte, v2)
        m_new = jnp.maximum(tot + m0, m_loc)
        c_ref[idx] = jnp.exp(tot + m0 - m_new) * cst + jnp.exp(m_loc - m_new) * u
        return m_new

    def body(i, ms):
        cb = jnp.where(i == 0, 0, nchunk - i)
        return (chunk(i, 0, False, ms[0]), chunk(i, 1, False, ms[1]),
                chunk(cb, 0, True, ms[2]), chunk(cb, 1, True, ms[3]))

    z = jnp.zeros((1, 1), F32)
    lax.fori_loop(0, nchunk, body, (z, z, z, z))

    def epilogue(i, carry):
        off = pl.multiple_of(i * L, L)
        for a in range(2):
            sl = slice(a * ML_DV, (a + 1) * ML_DV)
            hs = hf_ref[pl.ds(off, L), sl] + hb_ref[pl.ds(off, L), sl]
            hn = hs * lax.rsqrt(jnp.mean(hs * hs, axis=-1, keepdims=True) + EPS) * mg_ref[:, sl]
            o_ref[0, pl.ds(off, L), sl] = (hn * jax.nn.sigmoid(og_ref[0, pl.ds(off, L), sl].astype(F32))).astype(BF16)
        return carry

    lax.fori_loop(0, nchunk, epilogue, 0)


def _mlstm(q, kt, v, og, gr, gc, gm, mh_gain):
    bsz, t, _ = q.shape
    hp = ML_HEADS // 2
    nv = ML_HEADS * ML_DV
    w2 = 2 * ML_DV
    gr = gr.reshape(bsz, 4, hp, 2, t).transpose(0, 2, 1, 3, 4).reshape(bsz, hp, 8, t)
    gc = gc.reshape(bsz, t, 4, hp, 2).transpose(0, 3, 1, 2, 4).reshape(bsz, hp, t, 8)
    gm = gm.reshape(bsz, t, 4, hp, 2)[:, :, 0::2].transpose(0, 3, 1, 2, 4).reshape(bsz, hp, t, 4)
    return pl.pallas_call(
        _mlstm_kernel,
        grid=(bsz, hp),
        in_specs=[pl.BlockSpec((1, t, LANES), lambda b, h: (b, 0, h)),
                  pl.BlockSpec((1, LANES, t), lambda b, h: (b, h, 0)),
                  pl.BlockSpec((1, t, w2), lambda b, h: (b, 0, h)),
                  pl.BlockSpec((1, t, w2), lambda b, h: (b, 0, h)),
                  pl.BlockSpec((1, 1, 8, t), lambda b, h: (b, h, 0, 0)),
                  pl.BlockSpec((1, 1, t, 8), lambda b, h: (b, h, 0, 0)),
                  pl.BlockSpec((1, 1, t, 4), lambda b, h: (b, h, 0, 0)),
                  pl.BlockSpec((1, w2), lambda b, h: (0, h))],
        out_specs=pl.BlockSpec((1, t, w2), lambda b, h: (b, 0, h)),
        out_shape=jax.ShapeDtypeStruct((bsz, t, nv), BF16),
        scratch_shapes=[pltpu.VMEM((t, w2), F32), pltpu.VMEM((t, w2), F32),
                        pltpu.VMEM((4, LANES, w2), F32)],
        compiler_params=_params("arbitrary", "arbitrary"),
        name="mlstm",
    )(q, kt, v, og, gr, gc, gm, mh_gain.reshape(1, nv))


def _moe_pre_kernel(x_ref, g_ref, sh_ref, sc_ref, whi_ref, wlo_ref, br_ref,
                    h_ref, id_ref, wt_ref, rk_ref, cnt_ref):
    step = pl.program_id(0) * pl.num_programs(1) + pl.program_id(1)

    @pl.when(step == 0)
    def _():
        cnt_ref[...] = jnp.zeros_like(cnt_ref)

    h = _norm_mod(x_ref[0], g_ref[...], sh_ref[...], sc_ref[...])
    h_ref[0] = h
    hhi = h.astype(BF16)
    hlo = (h - hhi.astype(F32)).astype(BF16)
    whi = whi_ref[...]
    logits = _dot_nt(whi, hhi) + _dot_nt(whi, hlo) + _dot_nt(wlo_ref[...], hhi) + br_ref[...]
    ne = logits.shape[0]
    eidx = lax.broadcasted_iota(I32, logits.shape, 0)
    sels, vals, ids = [], [], []
    l = logits
    for _ in range(TOP_K):
        mval = jnp.max(l, axis=0, keepdims=True)
        idx = jnp.min(jnp.where(l == mval, eidx, ne), axis=0, keepdims=True)
        sel = eidx == idx
        l = jnp.where(sel, -jnp.inf, l)
        sels.append(sel)
        vals.append(mval)
        ids.append(idx)
    ex = [jnp.exp(v - vals[0]) for v in vals]
    tot = ex[0] + ex[1] + ex[2] + ex[3]
    member = sels[0] | sels[1] | sels[2] | sels[3]
    ti = lax.broadcasted_iota(I32, (TM, TM), 0)
    si = lax.broadcasted_iota(I32, (TM, TM), 1)
    before = (ti < si).astype(BF16)
    rank_e = _dot(member.astype(BF16), before) + cnt_ref[:, 0:1]
    for kk in range(TOP_K):
        id_ref[kk:kk + 1, :] = ids[kk]
        wt_ref[kk:kk + 1, :] = ex[kk] / tot
        rk_ref[kk:kk + 1, :] = jnp.sum(jnp.where(sels[kk], rank_e, 0.0), axis=0, keepdims=True).astype(I32)
    cnt_ref[...] = cnt_ref[...] + jnp.sum(member.astype(F32), axis=1, keepdims=True)


def _moe_pre(s, gain, mod, w_router, b_router):
    bsz, t, d = s.shape
    n = bsz * t
    nt = t // TM
    wt = w_router.T
    whi = wt.astype(BF16)
    wlo = (wt - whi.astype(F32)).astype(BF16)
    tok = lambda dt: jax.ShapeDtypeStruct((TOP_K, n), dt)
    tok_spec = pl.BlockSpec((TOP_K, TM), lambda b, i: (0, b * nt + i))
    return pl.pallas_call(
        _moe_pre_kernel,
        grid=(bsz, nt),
        in_specs=[pl.BlockSpec((1, TM, d), lambda b, i: (b, i, 0)),
                  pl.BlockSpec((1, d), lambda b, i: (0, 0)),
                  _mod_spec_d(bsz, 3, d), _mod_spec_d(bsz, 4, d),
                  pl.BlockSpec(whi.shape, lambda b, i: (0, 0)),
                  pl.BlockSpec(wlo.shape, lambda b, i: (0, 0)),
                  pl.BlockSpec((N_EXPERTS, 1), lambda b, i: (0, 0))],
        out_specs=[pl.BlockSpec((1, TM, d), lambda b, i: (b, i, 0)), tok_spec, tok_spec, tok_spec,
                   pl.BlockSpec((N_EXPERTS, LANES), lambda b, i: (0, 0))],
        out_shape=[jax.ShapeDtypeStruct((bsz, t, d), F32), tok(I32), tok(F32), tok(I32),
                   jax.ShapeDtypeStruct((N_EXPERTS, LANES), F32)],
        compiler_params=_params("arbitrary", "arbitrary"),
        name="moe_pre",
    )(s, gain.reshape(1, d), mod, mod, whi, wlo, b_router.reshape(N_EXPERTS, 1))


def _dispatch_kernel(last_ref, slot_ref, h_ref, xs_ref, zero_ref, sem, zsem):
    step = pl.program_id(0)
    n_exp = last_ref.shape[0] - 1
    n_tiles = xs_ref.shape[0] // TM

    def zero_copy(row):
        return pltpu.make_async_copy(zero_ref, xs_ref.at[pl.ds(pl.multiple_of(row, TM), TM)], zsem)

    @pl.when(step == 0)
    def _():
        zero_ref[...] = jnp.zeros_like(zero_ref)
        for e in range(n_exp):
            @pl.when(last_ref[e] >= 0)
            def _():
                zero_copy(last_ref[e]).start()
        for e in range(n_exp):
            @pl.when(last_ref[e] >= 0)
            def _():
                zero_copy(last_ref[e]).wait()

        def tail(i, carry):
            cp = zero_copy(i * TM)
            cp.start()
            cp.wait()
            return carry
        lax.fori_loop(last_ref[n_exp] // TM, n_tiles, tail, 0)

    def row_copy(j, kk):
        return pltpu.make_async_copy(h_ref.at[pl.ds(j, 1)], xs_ref.at[pl.ds(slot_ref[0, kk, j], 1)], sem)

    def issue(j, carry):
        for kk in range(TOP_K):
            row_copy(j, kk).start()
        return carry

    def drain(j, carry):
        for kk in range(TOP_K):
            row_copy(j, kk).wait()
        return carry

    lax.fori_loop(0, TM, issue, 0)
    lax.fori_loop(0, TM, drain, 0)


def _dispatch(h, slots, last_tile, n_slots):
    n, d = h.shape
    nt = n // TM
    slots3 = slots.reshape(TOP_K, nt, TM).transpose(1, 0, 2)
    return pl.pallas_call(
        _dispatch_kernel,
        grid_spec=pltpu.PrefetchScalarGridSpec(
            num_scalar_prefetch=1,
            grid=(nt,),
            in_specs=[pl.BlockSpec((1, TOP_K, TM), lambda i, last: (i, 0, 0), memory_space=pltpu.SMEM),
                      pl.BlockSpec((TM, d), lambda i, last: (i, 0))],
            out_specs=pl.BlockSpec(memory_space=pl.ANY),
            scratch_shapes=[pltpu.VMEM((TM, d), F32), pltpu.SemaphoreType.DMA, pltpu.SemaphoreType.DMA]),
        out_shape=jax.ShapeDtypeStruct((n_slots, d), F32),
        compiler_params=_params("arbitrary"),
        name="moe_dispatch",
    )(last_tile, slots3, h)


def _ffn_kernel(te_ref, nu_ref, x_ref, w1_ref, b1_ref, w2_ref, b2_ref, y_ref, w1b_ref, w2b_ref):
    i = pl.program_id(0)

    @pl.when(jnp.logical_or(i == 0, te_ref[i] != te_ref[jnp.maximum(i - 1, 0)]))
    def _():
        w1b_ref[...] = w1_ref[0].astype(BF16)
        w2b_ref[...] = w2_ref[0].astype(BF16)

    @pl.when(i < nu_ref[0])
    def _():
        dff = w2_ref.shape[1]
        u = _dot(x_ref[...].astype(BF16), w1b_ref[...]) + b1_ref[0]
        glu = jnp.minimum(u[:, :dff], SWIGLU_LIMIT)
        lin = jnp.clip(u[:, dff:], -SWIGLU_LIMIT, SWIGLU_LIMIT)
        act = glu * jax.nn.sigmoid(SWIGLU_ALPHA * glu) * (lin + 1.0)
        y_ref[...] = _dot(act.astype(BF16), w2b_ref[...]) + b2_ref[0]

    @pl.when(i >= nu_ref[0])
    def _():
        y_ref[...] = jnp.zeros_like(y_ref)


def _ffn(xs, tile_expert, n_used, layer, w1, b1, w2, b2):
    ns, d = xs.shape
    depth, ne, _, n1 = w1.shape
    dff = w2.shape[2]
    row = lambda i, te, nu: (i, 0)
    exp4 = lambda i, te, nu: (layer, te[i], 0, 0)
    return pl.pallas_call(
        _ffn_kernel,
        grid_spec=pltpu.PrefetchScalarGridSpec(
            num_scalar_prefetch=2,
            grid=(ns // TMX,),
            in_specs=[pl.BlockSpec((TMX, d), row),
                      pl.BlockSpec((None, 1, d, n1), exp4), pl.BlockSpec((None, 1, 1, n1), exp4),
                      pl.BlockSpec((None, 1, dff, d), exp4), pl.BlockSpec((None, 1, 1, d), exp4)],
            out_specs=pl.BlockSpec((TMX, d), row),
            scratch_shapes=[pltpu.VMEM((d, n1), BF16), pltpu.VMEM((dff, d), BF16)]),
        out_shape=jax.ShapeDtypeStruct((ns, d), F32),
        compiler_params=pltpu.CompilerParams(dimension_semantics=("arbitrary",), vmem_limit_bytes=FFN_VMEM_LIMIT),
        name="moe_ffn",
    )(tile_expert, n_used, xs, w1, b1.reshape(depth, ne, 1, n1), w2, b2.reshape(depth, ne, 1, d))


def _combine_kernel(slot_ref, wt_ref, s_ref, g_ref, ys_ref, o_ref, buf_ref, sem):
    def row_copy(j, kk):
        return pltpu.make_async_copy(ys_ref.at[pl.ds(slot_ref[0, kk, j], 1)], buf_ref.at[kk, pl.ds(j, 1)], sem)

    def issue(j, carry):
        for kk in range(TOP_K):
            row_copy(j, kk).start()
        return carry

    def drain(j, carry):
        for kk in range(TOP_K):
            row_copy(j, kk).wait()
        return carry

    lax.fori_loop(0, TM, issue, 0)
    lax.fori_loop(0, TM, drain, 0)
    wt = wt_ref[...]
    f = wt[:, 0:1] * buf_ref[0]
    for kk in range(1, TOP_K):
        f = f + wt[:, kk:kk + 1] * buf_ref[kk]
    o_ref[0] = s_ref[0] + g_ref[...] * f


def _combine(ys, slots, wts, s, mod, which):
    bsz, t, d = s.shape
    nt = t // TM
    n = bsz * t
    slots3 = slots.reshape(TOP_K, n // TM, TM).transpose(1, 0, 2)
    return pl.pallas_call(
        _combine_kernel,
        grid=(bsz, nt),
        in_specs=[pl.BlockSpec((1, TOP_K, TM), lambda b, i: (b * nt + i, 0, 0), memory_space=pltpu.SMEM),
                  pl.BlockSpec((TM, TOP_K), lambda b, i: (b * nt + i, 0)),
                  pl.BlockSpec((1, TM, d), lambda b, i: (b, i, 0)),
                  _mod_spec_d(bsz, which, d),
                  pl.BlockSpec(memory_space=pl.ANY)],
        out_specs=pl.BlockSpec((1, TM, d), lambda b, i: (b, i, 0)),
        out_shape=jax.ShapeDtypeStruct(s.shape, F32),
        scratch_shapes=[pltpu.VMEM((TOP_K, TM, d), F32), pltpu.SemaphoreType.DMA],
        input_output_aliases={2: 0},
        compiler_params=_params("arbitrary", "arbitrary"),
        name="moe_combine",
    )(slots3, wts.T, s, mod, ys)


def _moe(s, gain, mod, layer, w_router, b_router, w1, b1, w2, b2):
    bsz, t, d = s.shape
    n = bsz * t
    h, ids, wts, ranks, counts = _moe_pre(s, gain, mod, w_router, b_router)
    cnt = counts[:, 0].astype(I32)
    padded = ((cnt + TMX - 1) // TMX) * TMX
    ends = jnp.cumsum(padded)
    offs = ends - padded
    n_tiles = (TOP_K * n + N_EXPERTS * (TMX - 1) + TMX - 1) // TMX
    n_used = (ends[-1] // TMX).astype(I32)
    tile_start = jnp.arange(n_tiles, dtype=I32) * TMX
    te = jnp.sum(tile_start[:, None] >= ends[None, :], axis=1).astype(I32)
    te_last = jnp.max(jnp.where(cnt > 0, jnp.arange(N_EXPERTS, dtype=I32), 0))
    te = jnp.minimum(te, te_last)
    slots = ranks + jnp.sum(jnp.where(ids[..., None] == jnp.arange(N_EXPERTS, dtype=I32), offs, 0), axis=-1)
    pad_rows = [jnp.where(cnt > 0, ends - (r + 1) * TM, -1) for r in range(TMX // TM)]
    clear = jnp.concatenate(pad_rows + [ends[-1:]]).astype(I32)
    xs = _dispatch(h.reshape(n, d), slots, clear, n_tiles * TMX)
    ys = _ffn(xs, te, n_used.reshape(1), layer, w1, b1, w2, b2)
    return _combine(ys, slots, wts, s, mod, 5)


def kernel(x, c, ctx, c_ctx, w_ada, b_ada, g_norm_mix, g_norm_ffn, na_w_qkv, na_q_gain, na_k_gain, na_rpb,
           na_w_o, ml_w_in, ml_b_gate, ml_mh_gain, ml_w_o, moe_w_router, moe_b_router, moe_w1, moe_b1,
           moe_w2, moe_b2):
    bsz, seq, d = x.shape
    n_ctx = ctx.shape[1]
    depth = w_ada.shape[0]
    rows = seq // GRID_W
    n_mod_rows = ((bsz + 1 + 7) // 8) * 8
    cc = jnp.concatenate([c, c_ctx[None, :], jnp.zeros((n_mod_rows - bsz - 1, d), F32)], axis=0)
    mod_all = _ada(cc, w_ada, b_ada).reshape(depth, n_mod_rows, 6, 1, d)
    s = jnp.concatenate([ctx, x], axis=1)
    for i in range(depth):
        j = i // 2
        mod = mod_all[i]
        if i % 2 == 0:
            qkv = _na_qkv(s, g_norm_mix[i], mod, na_w_qkv[j].astype(BF16))
            y = _na_attn(qkv, _na_bias(na_rpb[j], rows), na_q_gain[j], na_k_gain[j], n_ctx)
            s = _oproj(y, na_w_o[j].astype(BF16), s, mod, 2)
        else:
            q, kt, v, og, gr, gc, gm = _ml_proj(s, g_norm_mix[i], mod, ml_w_in[j], ml_b_gate[j], n_ctx)
            y = _mlstm(q, kt, v, og, gr, gc, gm, ml_mh_gain[j])
            s = _oproj(y, ml_w_o[j].astype(BF16), s, mod, 2)
        s = _moe(s, g_norm_ffn[i], mod, i, moe_w_router[i], moe_b_router[i], moe_w1, moe_b1, moe_w2, moe_b2)
    return s[:, n_ctx:]
```

```python
import functools
import math

import numpy as np
import jax
import jax.numpy as jnp
from jax import lax
from jax.experimental import pallas as pl
from jax.experimental.pallas import tpu as pltpu
from jax.experimental.pallas import tpu_sc as plsc

F32 = jnp.float32
BF16 = jnp.bfloat16
I32 = jnp.int32

EPS = 1e-6
GRID_W = 64
NA_HEADS = 16
NA_HEAD_DIM = 64
NA_KR = 8
NA_KC = 16
NA_QROWS = 4
NA_KROWS = NA_QROWS + NA_KR - 1
ML_HEADS = 8
ML_DV = 128
ML_DQK = 64
ML_CHUNK = 256
ML_GATE_CAP = 15.0
ROPE_THETA = 10000.0
N_EXPERTS = 32
TOP_K = 4
SWIGLU_LIMIT = 7.0
SWIGLU_ALPHA = 1.702

TM = 256
TMX = 512
LANES = 128
NEG = -1e30
VMEM_LIMIT = 48 * 1024 * 1024
SC_CORES = 2
SC_SUBCORES = 16
SC_CHUNK = 64
FFN_VMEM_LIMIT =56 * 1024 * 1024


def _dot(a, b):
    return jnp.dot(a, b, preferred_element_type=F32)


def _dot_nt(a, b):
    return lax.dot_general(a, b, (((1,), (1,)), ((), ())), preferred_element_type=F32)


def _split3(x):
    hi = x.astype(BF16)
    r1 = x - hi.astype(F32)
    mid = r1.astype(BF16)
    lo = (r1 - mid.astype(F32)).astype(BF16)
    return hi, mid, lo


def _params(*sem):
    return pltpu.CompilerParams(dimension_semantics=sem, vmem_limit_bytes=VMEM_LIMIT)


def _norm_mod(x, gain, shift, scale):
    r = lax.rsqrt(jnp.mean(x * x, axis=-1, keepdims=True) + EPS)
    return (x * r * gain) * (1.0 + scale) + shift


def _mod_spec_d(n_batch, which, d):
    return pl.BlockSpec((None, None, 1, d), lambda b, t: (jnp.where(t == 0, n_batch, b), which, 0, 0))


def _ada_kernel(c_ref, w_ref, b_ref, o_ref):
    c = c_ref[...]
    s = c * jax.nn.sigmoid(c)
    o_ref[0] = _dot(s.astype(BF16), w_ref[0].astype(BF16)) + b_ref[0]


def _ada(cc, w_ada, b_ada):
    depth, d, n = w_ada.shape
    bn = 1536
    return pl.pallas_call(
        _ada_kernel,
        grid=(depth, n // bn),
        in_specs=[pl.BlockSpec((cc.shape[0], d), lambda l, j: (0, 0)),
                  pl.BlockSpec((1, d, bn), lambda l, j: (l, 0, j)),
                  pl.BlockSpec((1, 1, bn), lambda l, j: (l, 0, j))],
        out_specs=pl.BlockSpec((1, cc.shape[0], bn), lambda l, j: (l, 0, j)),
        out_shape=jax.ShapeDtypeStruct((depth, cc.shape[0], n), F32),
        compiler_params=_params("arbitrary", "arbitrary"),
        name="ada",
    )(cc, w_ada, b_ada.reshape(depth, 1, n))


def _na_qkv_kernel(x_ref, g_ref, sh_ref, sc_ref, w_ref, o_ref):
    h = _norm_mod(x_ref[0], g_ref[...], sh_ref[...], sc_ref[...])
    o_ref[0] = _dot(h.astype(BF16), w_ref[...]).astype(BF16)


def _na_qkv(s, gain, mod, w):
    bsz, t, d = s.shape
    n = w.shape[1]
    return pl.pallas_call(
        _na_qkv_kernel,
        grid=(bsz, t // TM),
        in_specs=[pl.BlockSpec((1, TM, d), lambda b, i: (b, i, 0)),
                  pl.BlockSpec((1, d), lambda b, i: (0, 0)),
                  _mod_spec_d(bsz, 0, d), _mod_spec_d(bsz, 1, d),
                  pl.BlockSpec((d, n), lambda b, i: (0, 0))],
        out_specs=pl.BlockSpec((1, TM, n), lambda b, i: (b, i, 0)),
        out_shape=jax.ShapeDtypeStruct((bsz, t, n), BF16),
        compiler_params=_params("arbitrary", "arbitrary"),
        name="na_qkv",
    )(s, gain.reshape(1, d), mod, mod, w)


def _na_bias(rpb, rows):
    nblk = rows // NA_QROWS
    kmax = rows - NA_KROWS
    nrr, ncc = 2 * NA_KR - 1, 2 * NA_KC - 1
    qc = np.arange(GRID_W)[:, None]
    kc = np.arange(GRID_W)[None, :]
    c0 = np.clip(qc - NA_KC // 2, 0, GRID_W - NA_KC)
    col_ok = (kc >= c0) & (kc < c0 + NA_KC)
    col_pick = np.eye(ncc)[np.clip(kc - qc + NA_KC - 1, 0, ncc - 1)]
    row_pick, valid = [], []
    for blk in (0, 1, nblk - 1):
        rs = blk * NA_QROWS
        ks = min(max(rs - NA_KR // 2, 0), kmax)
        qr = rs + np.arange(NA_QROWS)[:, None]
        kr = ks + np.arange(NA_KROWS)[None, :]
        r0 = np.clip(qr - NA_KR // 2, 0, rows - NA_KR)
        row_ok = (kr >= r0) & (kr < r0 + NA_KR)
        row_pick.append(np.eye(nrr)[np.clip(kr - qr + NA_KR - 1, 0, nrr - 1)])
        valid.append(row_ok[:, None, :, None] & col_ok[None, :, None, :])
    bias = jnp.einsum('vrka,hab,qcb->hvrqkc', jnp.asarray(np.stack(row_pick), F32), rpb,
                      jnp.asarray(col_pick, F32), precision=lax.Precision.HIGHEST)
    bias = jnp.where(np.stack(valid)[None], bias, NEG)
    nq, nk = NA_QROWS * GRID_W, NA_KROWS * GRID_W
    return bias.reshape(NA_HEADS // 2, 2, 3, nq, nk).transpose(0, 2, 1, 3, 4).astype(F32)


def _na_attn_kernel(q_ref, k_ref, v_ref, bias_ref, qg_ref, kg_ref, o_ref, kn_ref, *, n_ctx, rows):
    t = pl.program_id(2)
    first = lax.broadcasted_iota(I32, (1, LANES), 1) < NA_HEAD_DIM
    nq = NA_QROWS * GRID_W
    nk = NA_KROWS * GRID_W

    def head_rms(z, gain):
        z2 = z * z
        sa = jnp.sum(jnp.where(first, z2, 0.0), axis=-1, keepdims=True)
        sb = jnp.sum(jnp.where(first, 0.0, z2), axis=-1, keepdims=True)
        r = jnp.where(first, lax.rsqrt(sa / NA_HEAD_DIM + EPS), lax.rsqrt(sb / NA_HEAD_DIM + EPS))
        return z * r * gain

    @pl.when(t == 0)
    def _():
        def body(i, carry):
            off = pl.multiple_of(i * TM, TM)
            kn_ref[pl.ds(off, TM), :] = head_rms(k_ref[0, pl.ds(off, TM), :].astype(F32), kg_ref[...]).astype(BF16)
            return carry
        lax.fori_loop(0, k_ref.shape[1] // TM, body, 0)

    qn = (head_rms(q_ref[0].astype(F32), qg_ref[...]) * (NA_HEAD_DIM ** -0.5)).astype(BF16)
    zero = jnp.zeros_like(qn)
    q_heads = (jnp.where(first, qn, zero), jnp.where(first, zero, qn))
    kc = kn_ref[0:n_ctx, :]
    vc = v_ref[0, 0:n_ctx, :]

    def finish(outs):
        o_ref[0] = jnp.where(first, outs[0], outs[1]).astype(o_ref.dtype)

    @pl.when(t == 0)
    def _():
        outs = []
        for qh in q_heads:
            s = _dot_nt(qh, kc)
            m = jnp.max(s, axis=-1, keepdims=True)
            p = jnp.exp(s - m)
            outs.append(_dot(p.astype(BF16), vc) / jnp.sum(p, axis=-1, keepdims=True))
        finish(outs)

    @pl.when(t > 0)
    def _():
        blk = t - 1
        nblk = rows // NA_QROWS
        ks = jnp.clip(blk * NA_QROWS - NA_KR // 2, 0, rows - NA_KROWS)
        start = pl.multiple_of(n_ctx + ks * GRID_W, GRID_W)
        kw = kn_ref[pl.ds(start, nk), :]
        vw = v_ref[0, pl.ds(start, nk), :]
        kind = jnp.where(blk == 0, 0, jnp.where(blk == nblk - 1, 2, 1))
        outs = []
        for hh, qh in enumerate(q_heads):
            sw = _dot_nt(qh, kw) + bias_ref[0, kind, hh]
            sc = _dot_nt(qh, kc)
            m = jnp.maximum(jnp.max(sw, axis=-1, keepdims=True), jnp.max(sc, axis=-1, keepdims=True))
            pw = jnp.exp(sw - m)
            pc = jnp.exp(sc - m)
            l = jnp.sum(pw, axis=-1, keepdims=True) + jnp.sum(pc, axis=-1, keepdims=True)
            outs.append((_dot(pw.astype(BF16), vw) + _dot(pc.astype(BF16), vc)) / l)
        finish(outs)


def _na_attn(qkv, bias, q_gain, k_gain, n_ctx):
    bsz, t, _ = qkv.shape
    d = NA_HEADS * NA_HEAD_DIM
    rows = (t - n_ctx) // GRID_W
    assert n_ctx == TM and NA_QROWS * GRID_W == TM and rows % NA_QROWS == 0 and rows >= NA_KROWS
    hp = NA_HEADS // 2
    nq, nk = bias.shape[-2:]
    gq = jnp.tile(q_gain, 2).reshape(1, LANES)
    gk = jnp.tile(k_gain, 2).reshape(1, LANES)
    return pl.pallas_call(
        functools.partial(_na_attn_kernel, n_ctx=n_ctx, rows=rows),
        grid=(hp, bsz, t // TM),
        in_specs=[pl.BlockSpec((1, TM, LANES), lambda h, b, i: (b, i, h)),
                  pl.BlockSpec((1, t, LANES), lambda h, b, i: (b, 0, hp + h)),
                  pl.BlockSpec((1, t, LANES), lambda h, b, i: (b, 0, 2 * hp + h)),
                  pl.BlockSpec((1, 3, 2, nq, nk), lambda h, b, i: (h, 0, 0, 0, 0)),
                  pl.BlockSpec((1, LANES), lambda h, b, i: (0, 0)),
                  pl.BlockSpec((1, LANES), lambda h, b, i: (0, 0))],
        out_specs=pl.BlockSpec((1, TM, LANES), lambda h, b, i: (b, i, h)),
        out_shape=jax.ShapeDtypeStruct((bsz, t, d), BF16),
        scratch_shapes=[pltpu.VMEM((t, LANES), BF16)],
        compiler_params=_params("arbitrary", "arbitrary", "arbitrary"),
        name="na_attn",
    )(qkv, qkv, qkv, bias, gq, gk)


def _oproj_kernel(y_ref, w_ref, s_ref, g_ref, o_ref):
    o_ref[0] = s_ref[0] + g_ref[...] * _dot(y_ref[0], w_ref[...])


def _oproj(y, w, s, mod, which):
    bsz, t, d = s.shape
    return pl.pallas_call(
        _oproj_kernel,
        grid=(bsz, t // TM),
        in_specs=[pl.BlockSpec((1, TM, y.shape[-1]), lambda b, i: (b, i, 0)),
                  pl.BlockSpec(w.shape, lambda b, i: (0, 0)),
                  pl.BlockSpec((1, TM, d), lambda b, i: (b, i, 0)),
                  _mod_spec_d(bsz, which, d)],
        out_specs=pl.BlockSpec((1, TM, d), lambda b, i: (b, i, 0)),
        out_shape=jax.ShapeDtypeStruct(s.shape, F32),
        input_output_aliases={2: 0},
        compiler_params=_params("arbitrary", "arbitrary"),
        name="oproj",
    )(y, w, s, mod)


def _rope_tables(n_ctx, seq):
    lane = np.arange(LANES)
    dd = lane % ML_DQK
    grp = dd // (ML_DQK // 2)
    j = dd % (ML_DQK // 2)
    n = ML_DQK // 4
    inv = ROPE_THETA ** (-(j % n).astype(np.float64) / n)
    pos = np.arange(seq)
    p = np.where(grp[None, :] == 0, (pos // GRID_W)[:, None], (pos % GRID_W)[:, None]).astype(np.float64)
    ang = p * inv[None, :]
    cos = np.cos(ang)
    sin = np.where(j[None, :] < n, -np.sin(ang), np.sin(ang))
    cos = np.concatenate([np.ones((n_ctx, LANES)), cos], axis=0)
    sin = np.concatenate([np.zeros((n_ctx, LANES)), sin], axis=0)
    return jnp.asarray(cos, F32), jnp.asarray(sin, F32)


def _ml_proj_kernel(x_ref, g_ref, sh_ref, sc_ref, wm_ref, wg_ref, wgt_ref, bgr_ref, bgc_ref, cos_ref, sin_ref,
                    q_ref, kt_ref, v_ref, og_ref, gr_ref, gc_ref, gm_ref):
    h = _norm_mod(x_ref[0], g_ref[...], sh_ref[...], sc_ref[...]).astype(BF16)
    main = _dot(h, wm_ref[...])
    nq = q_ref.shape[-1]
    nv = v_ref.shape[-1]
    cos = cos_ref[...]
    sin = sin_ref[...]
    n = ML_DQK // 4
    lane = lax.broadcasted_iota(I32, (1, LANES), 1)
    low = (lane % (2 * n)) < n

    def rope(z):
        zr = jnp.where(low, pltpu.roll(z, LANES - n, 1), pltpu.roll(z, n, 1))
        return z * cos + zr * sin

    for j in range(nq // LANES):
        sl = slice(j * LANES, (j + 1) * LANES)
        q_ref[0, :, sl] = (rope(main[:, sl]) * (ML_DQK ** -0.5)).astype(BF16)
        kt_ref[0, sl, :] = rope(main[:, nq + j * LANES: nq + (j + 1) * LANES]).T.astype(BF16)
    v_ref[0] = main[:, 2 * nq: 2 * nq + nv].astype(BF16)
    og_ref[0] = main[:, 2 * nq + nv:].astype(BF16)

    nh = ML_HEADS
    ti = lax.broadcasted_iota(I32, (TM, TM), 0)
    si = lax.broadcasted_iota(I32, (TM, TM), 1)
    le = (si <= ti).astype(BF16)
    ge = (si >= ti).astype(BF16)

    def capped(pre):
        return ML_GATE_CAP * jnp.tanh(pre / ML_GATE_CAP)

    def exact_cum(tri, x, nt):
        parts = _split3(x)
        if nt:
            return sum(_dot_nt(p, tri) for p in parts)
        return sum(_dot(tri, p) for p in parts)

    gc = capped(_dot(h, wg_ref[...]) + bgr_ref[...])
    lf = jax.nn.log_sigmoid(gc)
    pre = exact_cum(le, lf, False)
    suf = exact_cum(ge, lf, False)
    cum = jnp.where(lane < 2 * nh, pre, suf)
    r = gc - pltpu.roll(cum, LANES - nh, 1)
    is_i = (lane // nh) % 2 == 0
    gc_ref[0] = jnp.where(is_i, r, cum)[:, :4 * nh]
    row = lax.broadcasted_iota(I32, (TM, 1), 0)
    pmax = r
    smax = r
    step = 1
    while step < TM:
        pmax = jnp.maximum(pmax, jnp.where(row >= step, pltpu.roll(pmax, step, 0), NEG))
        smax = jnp.maximum(smax, jnp.where(row < TM - step, pltpu.roll(smax, TM - step, 0), NEG))
        step *= 2
    gm_ref[0] = jnp.where(lane < 2 * nh, pmax, smax)[:, :4 * nh]

    gr = capped(_dot_nt(wgt_ref[...], h) + bgc_ref[...])
    lfr = jax.nn.log_sigmoid(gr)
    pre_r = exact_cum(le, lfr, True)
    suf_r = exact_cum(ge, lfr, True)
    gr_ref[0, 0 * nh:1 * nh, :] = gr[0 * nh:1 * nh] - pre_r[1 * nh:2 * nh]
    gr_ref[0, 1 * nh:2 * nh, :] = pre_r[1 * nh:2 * nh]
    gr_ref[0, 2 * nh:3 * nh, :] = gr[2 * nh:3 * nh] - suf_r[3 * nh:4 * nh]
    gr_ref[0, 3 * nh:4 * nh, :] = suf_r[3 * nh:4 * nh]


def _ml_proj(s, gain, mod, w_in, b_gate, n_ctx):
    bsz, t, d = s.shape
    nq = ML_HEADS * ML_DQK
    nv = ML_HEADS * ML_DV
    ng = 4 * ML_HEADS
    assert TM == ML_CHUNK
    wm = w_in[:, :2 * nq + nv + d].astype(BF16)
    wg = w_in[:, 2 * nq + nv + d:]
    wg_pad = jnp.pad(wg, ((0, 0), (0, LANES - ng))).astype(BF16)
    wgt = wg.T.astype(BF16)
    bgr = jnp.pad(b_gate, (0, LANES - ng)).reshape(1, LANES)
    bgc = b_gate.reshape(ng, 1)
    cos, sin = _rope_tables(n_ctx, t - n_ctx)
    tile = lambda n: pl.BlockSpec((1, TM, n), lambda b, i: (b, i, 0))
    const = lambda a: pl.BlockSpec(a.shape, lambda b, i: (0, 0))
    return pl.pallas_call(
        _ml_proj_kernel,
        grid=(bsz, t // TM),
        in_specs=[tile(d), pl.BlockSpec((1, d), lambda b, i: (0, 0)),
                  _mod_spec_d(bsz, 0, d), _mod_spec_d(bsz, 1, d),
                  const(wm), const(wg_pad), const(wgt), const(bgr), const(bgc),
                  pl.BlockSpec((TM, LANES), lambda b, i: (i, 0)),
                  pl.BlockSpec((TM, LANES), lambda b, i: (i, 0))],
        out_specs=[tile(nq), pl.BlockSpec((1, nq, TM), lambda b, i: (b, 0, i)), tile(nv), tile(d),
                   pl.BlockSpec((1, ng, TM), lambda b, i: (b, 0, i)),
                   tile(ng), tile(ng)],
        out_shape=[jax.ShapeDtypeStruct((bsz, t, nq), BF16), jax.ShapeDtypeStruct((bsz, nq, t), BF16),
                   jax.ShapeDtypeStruct((bsz, t, nv), BF16), jax.ShapeDtypeStruct((bsz, t, d), BF16),
                   jax.ShapeDtypeStruct((bsz, ng, t), F32), jax.ShapeDtypeStruct((bsz, t, ng), F32),
                   jax.ShapeDtypeStruct((bsz, t, ng), F32)],
        compiler_params=_params("arbitrary", "arbitrary"),
        name="ml_proj",
    )(s, gain.reshape(1, d), mod, mod, wm, wg_pad, wgt, bgr, bgc, cos, sin)


def _mlstm_kernel(q_ref, kt_ref, v_ref, og_ref, gr_ref, gc_ref, gm_ref, mg_ref, o_ref, hf_ref, hb_ref, c_ref):
    L = ML_CHUNK
    nchunk = q_ref.shape[1] // L
    first = lax.broadcasted_iota(I32, (1, LANES), 1) < ML_DQK
    first_row = lax.broadcasted_iota(I32, (LANES, 1), 0) < ML_DQK
    ti = lax.broadcasted_iota(I32, (L, L), 0)
    si = lax.broadcasted_iota(I32, (L, L), 1)
    masks = (si <= ti, si >= ti)
    ones_col = (lax.broadcasted_iota(I32, (L, ML_DV), 1) == 0).astype(BF16)

    c_ref[...] = jnp.zeros_like(c_ref)

    def chunk(c, a, bw, m0):
        off = pl.multiple_of(c * L, L)
        q = q_ref[0, pl.ds(off, L), :]
        kt = kt_ref[0, :, pl.ds(off, L)]
        qa = jnp.where(first if a == 0 else jnp.logical_not(first), q, jnp.zeros_like(q))
        kta = jnp.where(first_row if a == 0 else jnp.logical_not(first_row), kt, jnp.zeros_like(kt))
        v = v_ref[0, pl.ds(off, L), a * ML_DV:(a + 1) * ML_DV]
        w = 2 if bw else 0
        r_row = gr_ref[0, 0, 2 * w + a: 2 * w + a + 1, pl.ds(off, L)]
        cum_row = gr_ref[0, 0, 2 * (w + 1) + a: 2 * (w + 1) + a + 1, pl.ds(off, L)]
        cum_col = gc_ref[0, 0, pl.ds(off, L), 2 * (w + 1) + a: 2 * (w + 1) + a + 1]
        tot = cum_row[:, 0:1] if bw else cum_row[:, L - 1:L]
        mask = masks[1 if bw else 0]
        mx = gm_ref[0, 0, pl.ds(off, L), w + a: w + a + 1]
        mm = jnp.maximum(m0, mx)
        e = jnp.exp(jnp.where(mask, r_row - mm, NEG))
        s = (_dot(qa, kt) * e).astype(BF16)
        v2 = jnp.concatenate([v, ones_col], axis=1)
        idx = 2 * a + (1 if bw else 0)
        cst = c_ref[idx]
        both = _dot(s, v2) + jnp.exp(m0 - mm) * _dot(qa, cst.astype(BF16))
        num = both[:, :ML_DV]
        den = both[:, ML_DV:ML_DV + 1]
        hout = num / jnp.maximum(jnp.abs(den), jnp.exp(-cum_col - mm))
        dst = hb_ref if bw else hf_ref
        dst[pl.ds(off, L), a * ML_DV:(a + 1) * ML_DV] = hout
        m_loc = tot + jnp.max(r_row, axis=1, keepdims=True)
        kte = (kta.astype(F32) * jnp.exp(tot + r_row - m_loc)).astype(BF16)
        u = _dot(kte, v2)
        m_new = jnp.maximum(tot + m0, m_loc)
        c_ref[idx] = jnp.exp(tot + m0 - m_new) * cst + jnp.exp(m_loc - m_new) * u
        return m_new

    def body(i, ms):
        cb = jnp.where(i == 0, 0, nchunk - i)
        return (chunk(i, 0, False, ms[0]), chunk(i, 1, False, ms[1]),
                chunk(cb, 0, True, ms[2]), chunk(cb, 1, True, ms[3]))

    z = jnp.zeros((1, 1), F32)
    lax.fori_loop(0, nchunk, body, (z, z, z, z))

    def epilogue(i, carry):
        off = pl.multiple_of(i * L, L)
        for a in range(2):
            sl = slice(a * ML_DV, (a + 1) * ML_DV)
            hs = hf_ref[pl.ds(off, L), sl] + hb_ref[pl.ds(off, L), sl]
            hn = hs * lax.rsqrt(jnp.mean(hs * hs, axis=-1, keepdims=True) + EPS) * mg_ref[:, sl]
            o_ref[0, pl.ds(off, L), sl] = (hn * jax.nn.sigmoid(og_ref[0, pl.ds(off, L), sl].astype(F32))).astype(BF16)
        return carry

    lax.fori_loop(0, nchunk, epilogue, 0)


def _mlstm(q, kt, v, og, gr, gc, gm, mh_gain):
    bsz, t, _ = q.shape
    hp = ML_HEADS // 2
    nv = ML_HEADS * ML_DV
    w2 = 2 * ML_DV
    gr = gr.reshape(bsz, 4, hp, 2, t).transpose(0, 2, 1, 3, 4).reshape(bsz, hp, 8, t)
    gc = gc.reshape(bsz, t, 4, hp, 2).transpose(0, 3, 1, 2, 4).reshape(bsz, hp, t, 8)
    gm = gm.reshape(bsz, t, 4, hp, 2)[:, :, 0::2].transpose(0, 3, 1, 2, 4).reshape(bsz, hp, t, 4)
    return pl.pallas_call(
        _mlstm_kernel,
        grid=(bsz, hp),
        in_specs=[pl.BlockSpec((1, t, LANES), lambda b, h: (b, 0, h)),
                  pl.BlockSpec((1, LANES, t), lambda b, h: (b, h, 0)),
                  pl.BlockSpec((1, t, w2), lambda b, h: (b, 0, h)),
                  pl.BlockSpec((1, t, w2), lambda b, h: (b, 0, h)),
                  pl.BlockSpec((1, 1, 8, t), lambda b, h: (b, h, 0, 0)),
                  pl.BlockSpec((1, 1, t, 8), lambda b, h: (b, h, 0, 0)),
                  pl.BlockSpec((1, 1, t, 4), lambda b, h: (b, h, 0, 0)),
                  pl.BlockSpec((1, w2), lambda b, h: (0, h))],
        out_specs=pl.BlockSpec((1, t, w2), lambda b, h: (b, 0, h)),
        out_shape=jax.ShapeDtypeStruct((bsz, t, nv), BF16),
        scratch_shapes=[pltpu.VMEM((t, w2), F32), pltpu.VMEM((t, w2), F32),
                        pltpu.VMEM((4, LANES, w2), F32)],
        compiler_params=_params("arbitrary", "arbitrary"),
        name="mlstm",
    )(q, kt, v, og, gr, gc, gm, mh_gain.reshape(1, nv))


def _moe_pre_kernel(x_ref, g_ref, sh_ref, sc_ref, whi_ref, wlo_ref, br_ref,
                    h_ref, id_ref, wt_ref, rk_ref, cnt_ref):
    step = pl.program_id(0) * pl.num_programs(1) + pl.program_id(1)

    @pl.when(step == 0)
    def _():
        cnt_ref[...] = jnp.zeros_like(cnt_ref)

    h = _norm_mod(x_ref[0], g_ref[...], sh_ref[...], sc_ref[...])
    h_ref[0] = _pack_pairs(h)
    hhi = h.astype(BF16)
    hlo = (h - hhi.astype(F32)).astype(BF16)
    whi = whi_ref[...]
    logits = _dot_nt(whi, hhi) + _dot_nt(whi, hlo) + _dot_nt(wlo_ref[...], hhi) + br_ref[...]
    ne = logits.shape[0]
    eidx = lax.broadcasted_iota(I32, logits.shape, 0)
    sels, vals, ids = [], [], []
    l = logits
    for _ in range(TOP_K):
        mval = jnp.max(l, axis=0, keepdims=True)
        idx = jnp.min(jnp.where(l == mval, eidx, ne), axis=0, keepdims=True)
        sel = eidx == idx
        l = jnp.where(sel, -jnp.inf, l)
        sels.append(sel)
        vals.append(mval)
        ids.append(idx)
    ex = [jnp.exp(v - vals[0]) for v in vals]
    tot = ex[0] + ex[1] + ex[2] + ex[3]
    member = sels[0] | sels[1] | sels[2] | sels[3]
    ti = lax.broadcasted_iota(I32, (TM, TM), 0)
    si = lax.broadcasted_iota(I32, (TM, TM), 1)
    before = (ti < si).astype(BF16)
    rank_e = _dot(member.astype(BF16), before) + cnt_ref[:, 0:1]
    for kk in range(TOP_K):
        id_ref[kk:kk + 1, :] = ids[kk]
        wt_ref[kk:kk + 1, :] = ex[kk] / tot
        rk_ref[kk:kk + 1, :] = jnp.sum(jnp.where(sels[kk], rank_e, 0.0), axis=0, keepdims=True).astype(I32)
    cnt_ref[...] = cnt_ref[...] + jnp.sum(member.astype(F32), axis=1, keepdims=True)


def _moe_pre(s, gain, mod, w_router, b_router):
    bsz, t, d = s.shape
    n = bsz * t
    nt = t // TM
    wt = w_router.T
    whi = wt.astype(BF16)
    wlo = (wt - whi.astype(F32)).astype(BF16)
    tok = lambda dt: jax.ShapeDtypeStruct((TOP_K, n), dt)
    tok_spec = pl.BlockSpec((TOP_K, TM), lambda b, i: (0, b * nt + i))
    return pl.pallas_call(
        _moe_pre_kernel,
        grid=(bsz, nt),
        in_specs=[pl.BlockSpec((1, TM, d), lambda b, i: (b, i, 0)),
                  pl.BlockSpec((1, d), lambda b, i: (0, 0)),
                  _mod_spec_d(bsz, 3, d), _mod_spec_d(bsz, 4, d),
                  pl.BlockSpec(whi.shape, lambda b, i: (0, 0)),
                  pl.BlockSpec(wlo.shape, lambda b, i: (0, 0)),
                  pl.BlockSpec((N_EXPERTS, 1), lambda b, i: (0, 0))],
        out_specs=[pl.BlockSpec((1, TM, d // 2), lambda b, i: (b, i, 0)), tok_spec, tok_spec, tok_spec,
                   pl.BlockSpec((N_EXPERTS, LANES), lambda b, i: (0, 0))],
        out_shape=[jax.ShapeDtypeStruct((bsz, t, d // 2), I32), tok(I32), tok(F32), tok(I32),
                   jax.ShapeDtypeStruct((N_EXPERTS, LANES), F32)],
        compiler_params=_params("arbitrary", "arbitrary"),
        name="moe_pre",
    )(s, gain.reshape(1, d), mod, mod, whi, wlo, b_router.reshape(N_EXPERTS, 1))


def _sc_mesh():
    return plsc.VectorSubcoreMesh(core_axis_name="c", subcore_axis_name="s",
                                  num_cores=SC_CORES, num_subcores=SC_SUBCORES)


def _sc_worker_chunks(nchunks, body):
    n_workers = SC_CORES * SC_SUBCORES
    per_worker = -(-nchunks // n_workers)
    wid = lax.axis_index("s") * SC_CORES + lax.axis_index("c")

    @pl.loop(0, per_worker)
    def _(i):
        c = wid * per_worker + i

        @pl.when(c < nchunks)
        def _():
            body(c)


def _dispatch(hp, idx3, n_slots):
    n, w = hp.shape
    nchunks, kk, ch = idx3.shape

    @functools.partial(pl.kernel, mesh=_sc_mesh(), out_type=jax.ShapeDtypeStruct((n_slots, w), hp.dtype),
                       scratch_types=[pltpu.VMEM((kk, ch), I32), pltpu.VMEM((ch, w), hp.dtype)],
                       name="moe_dispatch")
    def scatter(src_hbm, idx_hbm, out_hbm, idx_v, rows_v):
        def body(c):
            pltpu.sync_copy(idx_hbm.at[c], idx_v)
            pltpu.sync_copy(src_hbm.at[pl.ds(c * ch, ch)], rows_v)
            for j in range(kk):
                pltpu.sync_copy(rows_v, out_hbm.at[idx_v.at[j]])
        _sc_worker_chunks(nchunks, body)

    return scatter(hp, idx3)


def _gather(ys, idx3):
    nchunks, kk, ch = idx3.shape
    w = ys.shape[1]

    @functools.partial(pl.kernel, mesh=_sc_mesh(), out_type=jax.ShapeDtypeStruct((kk, nchunks * ch, w), ys.dtype),
                       scratch_types=[pltpu.VMEM((kk, ch), I32), pltpu.VMEM((ch, w), ys.dtype)],
                       name="moe_gather")
    def gather(tab_hbm, idx_hbm, out_hbm, idx_v, rows_v):
        def body(c):
            pltpu.sync_copy(idx_hbm.at[c], idx_v)
            for j in range(kk):
                pltpu.sync_copy(tab_hbm.at[idx_v.at[j]], rows_v)
                pltpu.sync_copy(rows_v, out_hbm.at[j, pl.ds(c * ch, ch)])
        _sc_worker_chunks(nchunks, body)

    return gather(ys, idx3)


def _pack_pairs(x):
    w = x.shape[1] // 2
    bits = lax.bitcast_convert_type(x.astype(BF16).astype(F32), jnp.uint32)
    return lax.bitcast_convert_type((bits[:, :w] >> 16) | (bits[:, w:] & jnp.uint32(0xFFFF0000)), I32)


def _unpack_pairs(p):
    u = lax.bitcast_convert_type(p, jnp.uint32)
    lo = lax.bitcast_convert_type(u << 16, F32)
    hi = lax.bitcast_convert_type(u & jnp.uint32(0xFFFF0000), F32)
    return jnp.concatenate([lo, hi], axis=1)


def _ffn_kernel(te_ref, nv_ref, x_ref, w1_ref, b1_ref, w2_ref, b2_ref, y_ref, w1b_ref, w2b_ref):
    i = pl.program_id(0)

    @pl.when(jnp.logical_or(i == 0, te_ref[i] != te_ref[jnp.maximum(i - 1, 0)]))
    def _():
        w1b_ref[...] = w1_ref[0].astype(BF16)
        w2b_ref[...] = w2_ref[0].astype(BF16)

    @pl.when(nv_ref[i] > 0)
    def _():
        dff = w2_ref.shape[1]
        row = lax.broadcasted_iota(I32, (x_ref.shape[0], 1), 0)
        x = _unpack_pairs(jnp.where(row < nv_ref[i], x_ref[...], 0)).astype(BF16)
        u = _dot(x, w1b_ref[...]) + b1_ref[0]
        glu = jnp.minimum(u[:, :dff], SWIGLU_LIMIT)
        lin = jnp.clip(u[:, dff:], -SWIGLU_LIMIT, SWIGLU_LIMIT)
        act = glu * jax.nn.sigmoid(SWIGLU_ALPHA * glu) * (lin + 1.0)
        y_ref[...] = _pack_pairs(_dot(act.astype(BF16), w2b_ref[...]) + b2_ref[0])

    @pl.when(nv_ref[i] <= 0)
    def _():
        y_ref[...] = jnp.zeros_like(y_ref)


def _ffn(xs, tile_expert, n_valid, layer, w1, b1, w2, b2):
    ns, wd = xs.shape
    depth, ne, d, n1 = w1.shape
    dff = w2.shape[2]
    row = lambda i, te, nv: (i, 0)
    exp4 = lambda i, te, nv: (layer, te[i], 0, 0)
    return pl.pallas_call(
        _ffn_kernel,
        grid_spec=pltpu.PrefetchScalarGridSpec(
            num_scalar_prefetch=2,
            grid=(ns // TMX,),
            in_specs=[pl.BlockSpec((TMX, wd), row),
                      pl.BlockSpec((None, 1, d, n1), exp4), pl.BlockSpec((None, 1, 1, n1), exp4),
                      pl.BlockSpec((None, 1, dff, d), exp4), pl.BlockSpec((None, 1, 1, d), exp4)],
            out_specs=pl.BlockSpec((TMX, wd), row),
            scratch_shapes=[pltpu.VMEM((d, n1), BF16), pltpu.VMEM((dff, d), BF16)]),
        out_shape=jax.ShapeDtypeStruct((ns, wd), I32),
        compiler_params=pltpu.CompilerParams(dimension_semantics=("arbitrary",), vmem_limit_bytes=FFN_VMEM_LIMIT),
        name="moe_ffn",
    )(tile_expert, n_valid, xs, w1, b1.reshape(depth, ne, 1, n1), w2, b2.reshape(depth, ne, 1, d))


def _combine_kernel(wt_ref, s_ref, g_ref, yk_ref, o_ref):
    wt = wt_ref[...]
    f = wt[:, 0:1] * _unpack_pairs(yk_ref[0])
    for kk in range(1, TOP_K):
        f = f + wt[:, kk:kk + 1] * _unpack_pairs(yk_ref[kk])
    o_ref[0] = s_ref[0] + g_ref[...] * f


def _combine(yk, wts, s, mod, which):
    bsz, t, d = s.shape
    nt = t // TM
    return pl.pallas_call(
        _combine_kernel,
        grid=(bsz, nt),
        in_specs=[pl.BlockSpec((TM, TOP_K), lambda b, i: (b * nt + i, 0)),
                  pl.BlockSpec((1, TM, d), lambda b, i: (b, i, 0)),
                  _mod_spec_d(bsz, which, d),
                  pl.BlockSpec((TOP_K, TM, yk.shape[-1]), lambda b, i: (0, b * nt + i, 0))],
        out_specs=pl.BlockSpec((1, TM, d), lambda b, i: (b, i, 0)),
        out_shape=jax.ShapeDtypeStruct(s.shape, F32),
        input_output_aliases={1: 0},
        compiler_params=_params("arbitrary", "arbitrary"),
        name="moe_combine",
    )(wts.T, s, mod, yk)


def _moe(s, gain, mod, layer, w_router, b_router, w1, b1, w2, b2):
    bsz, t, d = s.shape
    n = bsz * t
    assert n % SC_CHUNK == 0
    hp, ids, wts, ranks, counts = _moe_pre(s, gain, mod, w_router, b_router)
    cnt = counts[:, 0].astype(I32)
    padded = ((cnt + TMX - 1) // TMX) * TMX
    ends = jnp.cumsum(padded)
    offs = ends - padded
    n_tiles = (TOP_K * n + N_EXPERTS * (TMX - 1) + TMX - 1) // TMX
    tile_start = jnp.arange(n_tiles, dtype=I32) * TMX
    te = jnp.sum(tile_start[:, None] >= ends[None, :], axis=1).astype(I32)
    te_last = jnp.max(jnp.where(cnt > 0, jnp.arange(N_EXPERTS, dtype=I32), 0))
    te = jnp.minimum(te, te_last)
    group_end = jnp.sum(jnp.where(te[:, None] == jnp.arange(N_EXPERTS, dtype=I32), offs + cnt, 0), axis=-1)
    n_valid = jnp.where(tile_start < ends[-1], jnp.clip(group_end - tile_start, 0, TMX), 0).astype(I32)
    slots = ranks + jnp.sum(jnp.where(ids[..., None] == jnp.arange(N_EXPERTS, dtype=I32), offs, 0), axis=-1)
    idx3 = slots.reshape(TOP_K, n // SC_CHUNK, SC_CHUNK).transpose(1, 0, 2)
    xs = _dispatch(hp.reshape(n, d // 2), idx3, n_tiles * TMX)
    ys = _ffn(xs, te, n_valid, layer, w1, b1, w2, b2)
    return _combine(_gather(ys, idx3), wts, s, mod, 5)


def kernel(x, c, ctx, c_ctx, w_ada, b_ada, g_norm_mix, g_norm_ffn, na_w_qkv, na_q_gain, na_k_gain, na_rpb,
           na_w_o, ml_w_in, ml_b_gate, ml_mh_gain, ml_w_o, moe_w_router, moe_b_router, moe_w1, moe_b1,
           moe_w2, moe_b2):
    bsz, seq, d = x.shape
    n_ctx = ctx.shape[1]
    depth = w_ada.shape[0]
    rows = seq // GRID_W
    n_mod_rows = ((bsz + 1 + 7) // 8) * 8
    cc = jnp.concatenate([c, c_ctx[None, :], jnp.zeros((n_mod_rows - bsz - 1, d), F32)], axis=0)
    mod_all = _ada(cc, w_ada, b_ada).reshape(depth, n_mod_rows, 6, 1, d)
    s = jnp.concatenate([ctx, x], axis=1)
    for i in range(depth):
        j = i // 2
        mod = mod_all[i]
        if i % 2 == 0:
            qkv = _na_qkv(s, g_norm_mix[i], mod, na_w_qkv[j].astype(BF16))
            y = _na_attn(qkv, _na_bias(na_rpb[j], rows), na_q_gain[j], na_k_gain[j], n_ctx)
            s = _oproj(y, na_w_o[j].astype(BF16), s, mod, 2)
        else:
            q, kt, v, og, gr, gc, gm = _ml_proj(s, g_norm_mix[i], mod, ml_w_in[j], ml_b_gate[j], n_ctx)
            y = _mlstm(q, kt, v, og, gr, gc, gm, ml_mh_gain[j])
            s = _oproj(y, ml_w_o[j].astype(BF16), s, mod, 2)
        s = _moe(s, g_norm_ffn[i], mod, i, moe_w_router[i], moe_b_router[i], moe_w1, moe_b1, moe_w2, moe_b2)
    return s[:, n_ctx:]
```

```python
import functools
import math

import numpy as np
import jax
import jax.numpy as jnp
from jax import lax
from jax.experimental import pallas as pl
from jax.experimental.pallas import tpu as pltpu
from jax.experimental.pallas import tpu_sc as plsc

F32 = jnp.float32
BF16 = jnp.bfloat16
I32 = jnp.int32

EPS = 1e-6
GRID_W = 64
NA_HEADS = 16
NA_HEAD_DIM = 64
NA_KR = 8
NA_KC = 16
NA_QROWS = 4
NA_KROWS = NA_QROWS + NA_KR - 1
ML_HEADS = 8
ML_DV = 128
ML_DQK = 64
ML_CHUNK = 256
ML_GATE_CAP = 15.0
ROPE_THETA = 10000.0
N_EXPERTS = 32
TOP_K = 4
SWIGLU_LIMIT = 7.0
SWIGLU_ALPHA = 1.702

TM = 256
TMX = 512
LANES = 128
NEG = -1e30
VMEM_LIMIT = 48 * 1024 * 1024
SC_CORES = 2
SC_SUBCORES = 16
SC_CHUNK = 64
FFN_VMEM_LIMIT =56 * 1024 * 1024


def _dot(a, b):
    return jnp.dot(a, b, preferred_element_type=F32)


def _dot_nt(a, b):
    return lax.dot_general(a, b, (((1,), (1,)), ((), ())), preferred_element_type=F32)


def _split3(x):
    hi = x.astype(BF16)
    r1 = x - hi.astype(F32)
    mid = r1.astype(BF16)
    lo = (r1 - mid.astype(F32)).astype(BF16)
    return hi, mid, lo


def _params(*sem):
    return pltpu.CompilerParams(dimension_semantics=sem, vmem_limit_bytes=VMEM_LIMIT)


def _norm_mod(x, gain, shift, scale):
    r = lax.rsqrt(jnp.mean(x * x, axis=-1, keepdims=True) + EPS)
    return (x * r * gain) * (1.0 + scale) + shift


def _mod_spec_d(n_batch, which, d, skip=0):
    return pl.BlockSpec((None, None, 1, d), lambda b, t: (jnp.where(t + skip == 0, n_batch, b), which, 0, 0))


def _ada_kernel(c_ref, w_ref, b_ref, o_ref):
    c = c_ref[...]
    s = c * jax.nn.sigmoid(c)
    o_ref[0] = _dot(s.astype(BF16), w_ref[0].astype(BF16)) + b_ref[0]


def _ada(cc, w_ada, b_ada):
    depth, d, n = w_ada.shape
    bn = 1536
    return pl.pallas_call(
        _ada_kernel,
        grid=(depth, n // bn),
        in_specs=[pl.BlockSpec((cc.shape[0], d), lambda l, j: (0, 0)),
                  pl.BlockSpec((1, d, bn), lambda l, j: (l, 0, j)),
                  pl.BlockSpec((1, 1, bn), lambda l, j: (l, 0, j))],
        out_specs=pl.BlockSpec((1, cc.shape[0], bn), lambda l, j: (l, 0, j)),
        out_shape=jax.ShapeDtypeStruct((depth, cc.shape[0], n), F32),
        compiler_params=_params("arbitrary", "arbitrary"),
        name="ada",
    )(cc, w_ada, b_ada.reshape(depth, 1, n))


def _na_qkv_kernel(x_ref, g_ref, sh_ref, sc_ref, w_ref, o_ref):
    h = _norm_mod(x_ref[0], g_ref[...], sh_ref[...], sc_ref[...])
    o_ref[0] = _dot(h.astype(BF16), w_ref[...]).astype(BF16)


def _na_qkv(s, gain, mod, w):
    bsz, t, d = s.shape
    n = w.shape[1]
    return pl.pallas_call(
        _na_qkv_kernel,
        grid=(bsz, t // TM),
        in_specs=[pl.BlockSpec((1, TM, d), lambda b, i: (b, i, 0)),
                  pl.BlockSpec((1, d), lambda b, i: (0, 0)),
                  _mod_spec_d(bsz, 0, d), _mod_spec_d(bsz, 1, d),
                  pl.BlockSpec((d, n), lambda b, i: (0, 0))],
        out_specs=pl.BlockSpec((1, TM, n), lambda b, i: (b, i, 0)),
        out_shape=jax.ShapeDtypeStruct((bsz, t, n), BF16),
        compiler_params=_params("arbitrary", "arbitrary"),
        name="na_qkv",
    )(s, gain.reshape(1, d), mod, mod, w)


def _na_bias(rpb, rows):
    nblk = rows // NA_QROWS
    kmax = rows - NA_KROWS
    nrr, ncc = 2 * NA_KR - 1, 2 * NA_KC - 1
    qc = np.arange(GRID_W)[:, None]
    kc = np.arange(GRID_W)[None, :]
    c0 = np.clip(qc - NA_KC // 2, 0, GRID_W - NA_KC)
    col_ok = (kc >= c0) & (kc < c0 + NA_KC)
    col_pick = np.eye(ncc)[np.clip(kc - qc + NA_KC - 1, 0, ncc - 1)]
    row_pick, valid = [], []
    for blk in (0, 1, nblk - 1):
        rs = blk * NA_QROWS
        ks = min(max(rs - NA_KR // 2, 0), kmax)
        qr = rs + np.arange(NA_QROWS)[:, None]
        kr = ks + np.arange(NA_KROWS)[None, :]
        r0 = np.clip(qr - NA_KR // 2, 0, rows - NA_KR)
        row_ok = (kr >= r0) & (kr < r0 + NA_KR)
        row_pick.append(np.eye(nrr)[np.clip(kr - qr + NA_KR - 1, 0, nrr - 1)])
        valid.append(row_ok[:, None, :, None] & col_ok[None, :, None, :])
    bias = jnp.einsum('vrka,hab,qcb->hvrqkc', jnp.asarray(np.stack(row_pick), F32), rpb,
                      jnp.asarray(col_pick, F32), precision=lax.Precision.HIGHEST)
    bias = jnp.where(np.stack(valid)[None], bias, NEG)
    nq, nk = NA_QROWS * GRID_W, NA_KROWS * GRID_W
    return bias.reshape(NA_HEADS // 2, 2, 3, nq, nk).transpose(0, 2, 1, 3, 4).astype(F32)


def _na_attn_kernel(q_ref, k_ref, v_ref, bias_ref, qg_ref, kg_ref, o_ref, kn_ref, va_ref, vb_ref, *, n_ctx, rows):
    t = pl.program_id(2)
    first = lax.broadcasted_iota(I32, (1, LANES), 1) < NA_HEAD_DIM
    nk = NA_KROWS * GRID_W

    def head_rms(z, gain):
        z2 = z * z
        sa = jnp.sum(jnp.where(first, z2, 0.0), axis=-1, keepdims=True)
        sb = jnp.sum(jnp.where(first, 0.0, z2), axis=-1, keepdims=True)
        r = jnp.where(first, lax.rsqrt(sa / NA_HEAD_DIM + EPS), lax.rsqrt(sb / NA_HEAD_DIM + EPS))
        return z * r * gain

    @pl.when(t == 0)
    def _():
        def body(i, carry):
            off = pl.multiple_of(i * TM, TM)
            kn_ref[pl.ds(off, TM), :] = head_rms(k_ref[0, pl.ds(off, TM), :].astype(F32), kg_ref[...]).astype(BF16)
            v = v_ref[0, pl.ds(off, TM), :]
            va_ref[pl.ds(off, TM), :] = jnp.where(first, v, jnp.ones_like(v))
            vb_ref[pl.ds(off, TM), :] = jnp.where(first, jnp.ones_like(v), v)
            return carry
        lax.fori_loop(0, k_ref.shape[1] // TM, body, 0)

    qn = (head_rms(q_ref[0].astype(F32), qg_ref[...]) * (NA_HEAD_DIM ** -0.5)).astype(BF16)
    zero = jnp.zeros_like(qn)
    q_heads = (jnp.where(first, qn, zero), jnp.where(first, zero, qn))
    v_heads = (va_ref, vb_ref)
    kc = kn_ref[0:n_ctx, :]

    def finish(outs):
        oa, ob = outs
        o_ref[0] = jnp.where(first, oa / pltpu.roll(oa, NA_HEAD_DIM, 1),
                             ob / pltpu.roll(ob, NA_HEAD_DIM, 1)).astype(o_ref.dtype)

    @pl.when(t == 0)
    def _():
        outs = []
        for qh, vh in zip(q_heads, v_heads):
            s = _dot_nt(qh, kc)
            m = jnp.max(s, axis=-1, keepdims=True)
            outs.append(_dot(jnp.exp((s - m).astype(BF16)), vh[0:n_ctx, :]))
        finish(outs)

    @pl.when(t > 0)
    def _():
        blk = t - 1
        nblk = rows // NA_QROWS
        ks = jnp.clip(blk * NA_QROWS - NA_KR // 2, 0, rows - NA_KROWS)
        start = pl.multiple_of(n_ctx + ks * GRID_W, GRID_W)
        kw = kn_ref[pl.ds(start, nk), :]
        kind = jnp.where(blk == 0, 0, jnp.where(blk == nblk - 1, 2, 1))
        outs = []
        for hh, (qh, vh) in enumerate(zip(q_heads, v_heads)):
            sw = _dot_nt(qh, kw) + bias_ref[0, kind, hh]
            sc = _dot_nt(qh, kc)
            m = jnp.maximum(jnp.max(sw, axis=-1, keepdims=True), jnp.max(sc, axis=-1, keepdims=True))
            pw = jnp.exp((sw - m).astype(BF16))
            pc = jnp.exp((sc - m).astype(BF16))
            outs.append(_dot(pw, vh[pl.ds(start, nk), :]) + _dot(pc, vh[0:n_ctx, :]))
        finish(outs)


def _na_attn(qkv, bias, q_gain, k_gain, n_ctx):
    bsz, t, _ = qkv.shape
    d = NA_HEADS * NA_HEAD_DIM
    rows = (t - n_ctx) // GRID_W
    assert n_ctx == TM and NA_QROWS * GRID_W == TM and rows % NA_QROWS == 0 and rows >= NA_KROWS
    hp = NA_HEADS // 2
    nq, nk = bias.shape[-2:]
    gq = jnp.tile(q_gain, 2).reshape(1, LANES)
    gk = jnp.tile(k_gain, 2).reshape(1, LANES)
    return pl.pallas_call(
        functools.partial(_na_attn_kernel, n_ctx=n_ctx, rows=rows),
        grid=(hp, bsz, t // TM),
        in_specs=[pl.BlockSpec((1, TM, LANES), lambda h, b, i: (b, i, h)),
                  pl.BlockSpec((1, t, LANES), lambda h, b, i: (b, 0, hp + h)),
                  pl.BlockSpec((1, t, LANES), lambda h, b, i: (b, 0, 2 * hp + h)),
                  pl.BlockSpec((1, 3, 2, nq, nk), lambda h, b, i: (h, 0, 0, 0, 0)),
                  pl.BlockSpec((1, LANES), lambda h, b, i: (0, 0)),
                  pl.BlockSpec((1, LANES), lambda h, b, i: (0, 0))],
        out_specs=pl.BlockSpec((1, TM, LANES), lambda h, b, i: (b, i, h)),
        out_shape=jax.ShapeDtypeStruct((bsz, t, d), BF16),
        scratch_shapes=[pltpu.VMEM((t, LANES), BF16)] * 3,
        compiler_params=_params("arbitrary", "arbitrary", "arbitrary"),
        name="na_attn",
    )(qkv, qkv, qkv, bias, gq, gk)


def _oproj_kernel(y_ref, w_ref, s_ref, g_ref, o_ref):
    o_ref[0] = s_ref[0] + g_ref[...] * _dot(y_ref[0], w_ref[...])


def _oproj(y, w, s, mod, which):
    bsz, t, d = s.shape
    return pl.pallas_call(
        _oproj_kernel,
        grid=(bsz, t // TM),
        in_specs=[pl.BlockSpec((1, TM, y.shape[-1]), lambda b, i: (b, i, 0)),
                  pl.BlockSpec(w.shape, lambda b, i: (0, 0)),
                  pl.BlockSpec((1, TM, d), lambda b, i: (b, i, 0)),
                  _mod_spec_d(bsz, which, d)],
        out_specs=pl.BlockSpec((1, TM, d), lambda b, i: (b, i, 0)),
        out_shape=jax.ShapeDtypeStruct(s.shape, F32),
        input_output_aliases={2: 0},
        compiler_params=_params("arbitrary", "arbitrary"),
        name="oproj",
    )(y, w, s, mod)


def _rope_tables(n_ctx, seq):
    lane = np.arange(LANES)
    dd = lane % ML_DQK
    grp = dd // (ML_DQK // 2)
    j = dd % (ML_DQK // 2)
    n = ML_DQK // 4
    inv = ROPE_THETA ** (-(j % n).astype(np.float64) / n)
    pos = np.arange(seq)
    p = np.where(grp[None, :] == 0, (pos // GRID_W)[:, None], (pos % GRID_W)[:, None]).astype(np.float64)
    ang = p * inv[None, :]
    cos = np.cos(ang)
    sin = np.where(j[None, :] < n, -np.sin(ang), np.sin(ang))
    cos = np.concatenate([np.ones((n_ctx, LANES)), cos], axis=0)
    sin = np.concatenate([np.zeros((n_ctx, LANES)), sin], axis=0)
    return jnp.asarray(cos, F32), jnp.asarray(sin, F32)


def _ml_proj_kernel(x_ref, g_ref, sh_ref, sc_ref, wm_ref, wg_ref, wgt_ref, bgr_ref, bgc_ref, cos_ref, sin_ref,
                    q_ref, kt_ref, v_ref, og_ref, gr_ref, gc_ref, gm_ref):
    h = _norm_mod(x_ref[0], g_ref[...], sh_ref[...], sc_ref[...]).astype(BF16)
    main = _dot(h, wm_ref[...])
    nq = q_ref.shape[-1]
    nv = v_ref.shape[-1]
    cos = cos_ref[...]
    sin = sin_ref[...]
    n = ML_DQK // 4
    lane = lax.broadcasted_iota(I32, (1, LANES), 1)
    low = (lane % (2 * n)) < n

    def rope(z):
        zr = jnp.where(low, pltpu.roll(z, LANES - n, 1), pltpu.roll(z, n, 1))
        return z * cos + zr * sin

    for j in range(nq // LANES):
        sl = slice(j * LANES, (j + 1) * LANES)
        q_ref[0, :, sl] = (rope(main[:, sl]) * (ML_DQK ** -0.5)).astype(BF16)
        kt_ref[0, sl, :] = rope(main[:, nq + j * LANES: nq + (j + 1) * LANES]).T.astype(BF16)
    v_ref[0] = main[:, 2 * nq: 2 * nq + nv].astype(BF16)
    og_ref[0] = main[:, 2 * nq + nv:].astype(BF16)

    nh = ML_HEADS
    ti = lax.broadcasted_iota(I32, (TM, TM), 0)
    si = lax.broadcasted_iota(I32, (TM, TM), 1)
    le = (si <= ti).astype(BF16)
    ge = (si >= ti).astype(BF16)

    def capped(pre):
        return ML_GATE_CAP * jnp.tanh(pre / ML_GATE_CAP)

    def exact_cum(tri, x, nt):
        parts = _split3(x)
        if nt:
            return sum(_dot_nt(p, tri) for p in parts)
        return sum(_dot(tri, p) for p in parts)

    gc = capped(_dot(h, wg_ref[...]) + bgr_ref[...])
    lf = jax.nn.log_sigmoid(gc)
    pre = exact_cum(le, lf, False)
    suf = exact_cum(ge, lf, False)
    cum = jnp.where(lane < 2 * nh, pre, suf)
    r = gc - pltpu.roll(cum, LANES - nh, 1)
    is_i = (lane // nh) % 2 == 0
    gc_ref[0] = jnp.where(is_i, r, cum)[:, :4 * nh]
    row = lax.broadcasted_iota(I32, (TM, 1), 0)
    pmax = r
    smax = r
    step = 1
    while step < TM:
        pmax = jnp.maximum(pmax, jnp.where(row >= step, pltpu.roll(pmax, step, 0), NEG))
        smax = jnp.maximum(smax, jnp.where(row < TM - step, pltpu.roll(smax, TM - step, 0), NEG))
        step *= 2
    gm_ref[0] = jnp.where(lane < 2 * nh, pmax, smax)[:, :4 * nh]

    gr = capped(_dot_nt(wgt_ref[...], h) + bgc_ref[...])
    lfr = jax.nn.log_sigmoid(gr)
    pre_r = exact_cum(le, lfr, True)
    suf_r = exact_cum(ge, lfr, True)
    gr_ref[0, 0 * nh:1 * nh, :] = gr[0 * nh:1 * nh] - pre_r[1 * nh:2 * nh]
    gr_ref[0, 1 * nh:2 * nh, :] = pre_r[1 * nh:2 * nh]
    gr_ref[0, 2 * nh:3 * nh, :] = gr[2 * nh:3 * nh] - suf_r[3 * nh:4 * nh]
    gr_ref[0, 3 * nh:4 * nh, :] = suf_r[3 * nh:4 * nh]


def _ml_proj(s, gain, mod, w_in, b_gate, n_ctx):
    bsz, t, d = s.shape
    nq = ML_HEADS * ML_DQK
    nv = ML_HEADS * ML_DV
    ng = 4 * ML_HEADS
    assert TM == ML_CHUNK
    wm = w_in[:, :2 * nq + nv + d].astype(BF16)
    wg = w_in[:, 2 * nq + nv + d:]
    wg_pad = jnp.pad(wg, ((0, 0), (0, LANES - ng))).astype(BF16)
    wgt = wg.T.astype(BF16)
    bgr = jnp.pad(b_gate, (0, LANES - ng)).reshape(1, LANES)
    bgc = b_gate.reshape(ng, 1)
    cos, sin = _rope_tables(n_ctx, t - n_ctx)
    tile = lambda n: pl.BlockSpec((1, TM, n), lambda b, i: (b, i, 0))
    const = lambda a: pl.BlockSpec(a.shape, lambda b, i: (0, 0))
    return pl.pallas_call(
        _ml_proj_kernel,
        grid=(bsz, t // TM),
        in_specs=[tile(d), pl.BlockSpec((1, d), lambda b, i: (0, 0)),
                  _mod_spec_d(bsz, 0, d), _mod_spec_d(bsz, 1, d),
                  const(wm), const(wg_pad), const(wgt), const(bgr), const(bgc),
                  pl.BlockSpec((TM, LANES), lambda b, i: (i, 0)),
                  pl.BlockSpec((TM, LANES), lambda b, i: (i, 0))],
        out_specs=[tile(nq), pl.BlockSpec((1, nq, TM), lambda b, i: (b, 0, i)), tile(nv), tile(d),
                   pl.BlockSpec((1, ng, TM), lambda b, i: (b, 0, i)),
                   tile(ng), tile(ng)],
        out_shape=[jax.ShapeDtypeStruct((bsz, t, nq), BF16), jax.ShapeDtypeStruct((bsz, nq, t), BF16),
                   jax.ShapeDtypeStruct((bsz, t, nv), BF16), jax.ShapeDtypeStruct((bsz, t, d), BF16),
                   jax.ShapeDtypeStruct((bsz, ng, t), F32), jax.ShapeDtypeStruct((bsz, t, ng), F32),
                   jax.ShapeDtypeStruct((bsz, t, ng), F32)],
        compiler_params=_params("arbitrary", "arbitrary"),
        name="ml_proj",
    )(s, gain.reshape(1, d), mod, mod, wm, wg_pad, wgt, bgr, bgc, cos, sin)


def _mlstm_kernel(q_ref, kt_ref, v_ref, og_ref, gr_ref, gc_ref, gm_ref, mg_ref, o_ref, hf_ref, hb_ref, c_ref):
    L = ML_CHUNK
    nchunk = q_ref.shape[1] // L
    first = lax.broadcasted_iota(I32, (1, LANES), 1) < ML_DQK
    first_row = lax.broadcasted_iota(I32, (LANES, 1), 0) < ML_DQK
    ti = lax.broadcasted_iota(I32, (L, L), 0)
    si = lax.broadcasted_iota(I32, (L, L), 1)
    masks = (si <= ti, si >= ti)
    ones_blk = jnp.ones((L, ML_DV), BF16)

    c_ref[...] = jnp.zeros_like(c_ref)

    def chunk(c, a, bw, m0):
        off = pl.multiple_of(c * L, L)
        q = q_ref[0, pl.ds(off, L), :]
        kt = kt_ref[0, :, pl.ds(off, L)]
        qa = jnp.where(first if a == 0 else jnp.logical_not(first), q, jnp.zeros_like(q))
        kta = jnp.where(first_row if a == 0 else jnp.logical_not(first_row), kt, jnp.zeros_like(kt))
        v = v_ref[0, pl.ds(off, L), a * ML_DV:(a + 1) * ML_DV]
        w = 2 if bw else 0
        r_row = gr_ref[0, 0, 2 * w + a: 2 * w + a + 1, pl.ds(off, L)]
        cum_row = gr_ref[0, 0, 2 * (w + 1) + a: 2 * (w + 1) + a + 1, pl.ds(off, L)]
        cum_col = gc_ref[0, 0, pl.ds(off, L), 2 * (w + 1) + a: 2 * (w + 1) + a + 1]
        tot = cum_row[:, 0:1] if bw else cum_row[:, L - 1:L]
        mask = masks[1 if bw else 0]
        mx = gm_ref[0, 0, pl.ds(off, L), w + a: w + a + 1]
        mm = jnp.broadcast_to(jnp.maximum(m0, mx), (L, LANES))
        cum_b = jnp.broadcast_to(cum_col, (L, LANES))
        e = jnp.exp(jnp.where(mask, r_row - jnp.concatenate([mm] * (L // LANES), axis=1), NEG))
        s = (_dot(qa, kt) * e).astype(BF16)
        v2 = jnp.concatenate([v, ones_blk], axis=1)
        idx = 2 * a + (1 if bw else 0)
        cst = c_ref[idx]
        w_inter = jnp.exp(m0 - mm)
        both = _dot(s, v2) + jnp.concatenate([w_inter, w_inter], axis=1) * _dot(qa, cst.astype(BF16))
        num = both[:, :ML_DV]
        den = both[:, ML_DV:]
        hout = num / jnp.maximum(jnp.abs(den), jnp.exp(-cum_b - mm))
        dst = hb_ref if bw else hf_ref
        dst[pl.ds(off, L), a * ML_DV:(a + 1) * ML_DV] = hout
        m_loc = tot + jnp.max(r_row, axis=1, keepdims=True)
        kte = (kta.astype(F32) * jnp.exp(tot + r_row - m_loc)).astype(BF16)
        u = _dot(kte, v2)
        m_new = jnp.maximum(tot + m0, m_loc)
        c_ref[idx] = jnp.exp(tot + m0 - m_new) * cst + jnp.exp(m_loc - m_new) * u
        return m_new

    def body(i, ms):
        cb = jnp.where(i == 0, 0, nchunk - i)
        return (chunk(i, 0, False, ms[0]), chunk(i, 1, False, ms[1]),
                chunk(cb, 0, True, ms[2]), chunk(cb, 1, True, ms[3]))

    z = jnp.zeros((1, 1), F32)
    lax.fori_loop(0, nchunk, body, (z, z, z, z))

    def epilogue(i, carry):
        off = pl.multiple_of(i * L, L)
        for a in range(2):
            sl = slice(a * ML_DV, (a + 1) * ML_DV)
            hs = hf_ref[pl.ds(off, L), sl] + hb_ref[pl.ds(off, L), sl]
            hn = hs * lax.rsqrt(jnp.mean(hs * hs, axis=-1, keepdims=True) + EPS) * mg_ref[:, sl]
            o_ref[0, pl.ds(off, L), sl] = (hn * jax.nn.sigmoid(og_ref[0, pl.ds(off, L), sl].astype(F32))).astype(BF16)
        return carry

    lax.fori_loop(0, nchunk, epilogue, 0)


def _mlstm(q, kt, v, og, gr, gc, gm, mh_gain):
    bsz, t, _ = q.shape
    hp = ML_HEADS // 2
    nv = ML_HEADS * ML_DV
    w2 = 2 * ML_DV
    gr = gr.reshape(bsz, 4, hp, 2, t).transpose(0, 2, 1, 3, 4).reshape(bsz, hp, 8, t)
    gc = gc.reshape(bsz, t, 4, hp, 2).transpose(0, 3, 1, 2, 4).reshape(bsz, hp, t, 8)
    gm = gm.reshape(bsz, t, 4, hp, 2)[:, :, 0::2].transpose(0, 3, 1, 2, 4).reshape(bsz, hp, t, 4)
    return pl.pallas_call(
        _mlstm_kernel,
        grid=(bsz, hp),
        in_specs=[pl.BlockSpec((1, t, LANES), lambda b, h: (b, 0, h)),
                  pl.BlockSpec((1, LANES, t), lambda b, h: (b, h, 0)),
                  pl.BlockSpec((1, t, w2), lambda b, h: (b, 0, h)),
                  pl.BlockSpec((1, t, w2), lambda b, h: (b, 0, h)),
                  pl.BlockSpec((1, 1, 8, t), lambda b, h: (b, h, 0, 0)),
                  pl.BlockSpec((1, 1, t, 8), lambda b, h: (b, h, 0, 0)),
                  pl.BlockSpec((1, 1, t, 4), lambda b, h: (b, h, 0, 0)),
                  pl.BlockSpec((1, w2), lambda b, h: (0, h))],
        out_specs=pl.BlockSpec((1, t, w2), lambda b, h: (b, 0, h)),
        out_shape=jax.ShapeDtypeStruct((bsz, t, nv), BF16),
        scratch_shapes=[pltpu.VMEM((t, w2), F32), pltpu.VMEM((t, w2), F32),
                        pltpu.VMEM((4, LANES, w2), F32)],
        compiler_params=_params("arbitrary", "arbitrary"),
        name="mlstm",
    )(q, kt, v, og, gr, gc, gm, mh_gain.reshape(1, nv))


def _moe_pre_kernel(x_ref, g_ref, sh_ref, sc_ref, whi_ref, wlo_ref, br_ref,
                    h_ref, id_ref, wt_ref, rk_ref, cnt_ref):
    step = pl.program_id(0) * pl.num_programs(1) + pl.program_id(1)

    @pl.when(step == 0)
    def _():
        cnt_ref[...] = jnp.zeros_like(cnt_ref)

    h = _norm_mod(x_ref[0], g_ref[...], sh_ref[...], sc_ref[...])
    h_ref[0] = _pack_pairs(h)
    hhi = h.astype(BF16)
    hlo = (h - hhi.astype(F32)).astype(BF16)
    whi = whi_ref[...]
    logits = _dot_nt(whi, hhi) + _dot_nt(whi, hlo) + _dot_nt(wlo_ref[...], hhi) + br_ref[...]
    ne = logits.shape[0]
    eidx = lax.broadcasted_iota(I32, logits.shape, 0)
    sels, vals, ids = [], [], []
    l = logits
    for _ in range(TOP_K):
        mval = jnp.max(l, axis=0, keepdims=True)
        idx = jnp.min(jnp.where(l == mval, eidx, ne), axis=0, keepdims=True)
        sel = eidx == idx
        l = jnp.where(sel, -jnp.inf, l)
        sels.append(sel)
        vals.append(mval)
        ids.append(idx)
    ex = [jnp.exp(v - vals[0]) for v in vals]
    tot = ex[0] + ex[1] + ex[2] + ex[3]
    member = sels[0] | sels[1] | sels[2] | sels[3]
    ti = lax.broadcasted_iota(I32, (TM, TM), 0)
    si = lax.broadcasted_iota(I32, (TM, TM), 1)
    before = (ti < si).astype(BF16)
    rank_e = _dot(member.astype(BF16), before) + cnt_ref[:, 0:1]
    for kk in range(TOP_K):
        id_ref[kk:kk + 1, :] = ids[kk]
        wt_ref[kk:kk + 1, :] = ex[kk] / tot
        rk_ref[kk:kk + 1, :] = jnp.sum(jnp.where(sels[kk], rank_e, 0.0), axis=0, keepdims=True).astype(I32)
    cnt_ref[...] = cnt_ref[...] + jnp.sum(member.astype(F32), axis=1, keepdims=True)


def _moe_pre(s, gain, mod, w_router, b_router, skip):
    bsz, t, d = s.shape
    nt = t // TM - skip
    n = bsz * nt * TM
    wt = w_router.T
    whi = wt.astype(BF16)
    wlo = (wt - whi.astype(F32)).astype(BF16)
    tok = lambda dt: jax.ShapeDtypeStruct((TOP_K, n), dt)
    tok_spec = pl.BlockSpec((TOP_K, TM), lambda b, i: (0, b * nt + i))
    return pl.pallas_call(
        _moe_pre_kernel,
        grid=(bsz, nt),
        in_specs=[pl.BlockSpec((1, TM, d), lambda b, i: (b, i + skip, 0)),
                  pl.BlockSpec((1, d), lambda b, i: (0, 0)),
                  _mod_spec_d(bsz, 3, d, skip), _mod_spec_d(bsz, 4, d, skip),
                  pl.BlockSpec(whi.shape, lambda b, i: (0, 0)),
                  pl.BlockSpec(wlo.shape, lambda b, i: (0, 0)),
                  pl.BlockSpec((N_EXPERTS, 1), lambda b, i: (0, 0))],
        out_specs=[pl.BlockSpec((1, TM, d // 2), lambda b, i: (b, i, 0)), tok_spec, tok_spec, tok_spec,
                   pl.BlockSpec((N_EXPERTS, LANES), lambda b, i: (0, 0))],
        out_shape=[jax.ShapeDtypeStruct((bsz, nt * TM, d // 2), I32), tok(I32), tok(F32), tok(I32),
                   jax.ShapeDtypeStruct((N_EXPERTS, LANES), F32)],
        compiler_params=_params("arbitrary", "arbitrary"),
        name="moe_pre",
    )(s, gain.reshape(1, d), mod, mod, whi, wlo, b_router.reshape(N_EXPERTS, 1))


def _sc_mesh():
    return plsc.VectorSubcoreMesh(core_axis_name="c", subcore_axis_name="s",
                                  num_cores=SC_CORES, num_subcores=SC_SUBCORES)


def _sc_worker_chunks(nchunks, body):
    n_workers = SC_CORES * SC_SUBCORES
    per_worker = -(-nchunks // n_workers)
    wid = lax.axis_index("s") * SC_CORES + lax.axis_index("c")

    @pl.loop(0, per_worker)
    def _(i):
        c = wid * per_worker + i

        @pl.when(c < nchunks)
        def _():
            body(c)


def _dispatch(hp, idx3, n_slots):
    n, w = hp.shape
    nchunks, kk, ch = idx3.shape

    @functools.partial(pl.kernel, mesh=_sc_mesh(), out_type=jax.ShapeDtypeStruct((n_slots, w), hp.dtype),
                       scratch_types=[pltpu.VMEM((kk, ch), I32), pltpu.VMEM((ch, w), hp.dtype)],
                       name="moe_dispatch")
    def scatter(src_hbm, idx_hbm, out_hbm, idx_v, rows_v):
        def body(c):
            pltpu.sync_copy(idx_hbm.at[c], idx_v)
            pltpu.sync_copy(src_hbm.at[pl.ds(c * ch, ch)], rows_v)
            for j in range(kk):
                pltpu.sync_copy(rows_v, out_hbm.at[idx_v.at[j]])
        _sc_worker_chunks(nchunks, body)

    return scatter(hp, idx3)


def _gather(ys, idx3):
    nchunks, kk, ch = idx3.shape
    w = ys.shape[1]

    @functools.partial(pl.kernel, mesh=_sc_mesh(), out_type=jax.ShapeDtypeStruct((kk, nchunks * ch, w), ys.dtype),
                       scratch_types=[pltpu.VMEM((kk, ch), I32), pltpu.VMEM((ch, w), ys.dtype)],
                       name="moe_gather")
    def gather(tab_hbm, idx_hbm, out_hbm, idx_v, rows_v):
        def body(c):
            pltpu.sync_copy(idx_hbm.at[c], idx_v)
            for j in range(kk):
                pltpu.sync_copy(tab_hbm.at[idx_v.at[j]], rows_v)
                pltpu.sync_copy(rows_v, out_hbm.at[j, pl.ds(c * ch, ch)])
        _sc_worker_chunks(nchunks, body)

    return gather(ys, idx3)


def _pack_pairs(x):
    w = x.shape[1] // 2
    bits = lax.bitcast_convert_type(x.astype(BF16).astype(F32), jnp.uint32)
    return lax.bitcast_convert_type((bits[:, :w] >> 16) | (bits[:, w:] & jnp.uint32(0xFFFF0000)), I32)


def _unpack_pairs(p):
    u = lax.bitcast_convert_type(p, jnp.uint32)
    lo = lax.bitcast_convert_type(u << 16, F32)
    hi = lax.bitcast_convert_type(u & jnp.uint32(0xFFFF0000), F32)
    return jnp.concatenate([lo, hi], axis=1)


def _ffn_kernel(te_ref, nv_ref, x_ref, w1_ref, b1_ref, w2_ref, b2_ref, y_ref, w1b_ref, w2b_ref):
    i = pl.program_id(0)

    @pl.when(jnp.logical_or(i == 0, te_ref[i] != te_ref[jnp.maximum(i - 1, 0)]))
    def _():
        w1b_ref[...] = w1_ref[0].astype(BF16)
        w2b_ref[...] = w2_ref[0].astype(BF16)

    @pl.when(nv_ref[i] > 0)
    def _():
        dff = w2_ref.shape[1]
        row = lax.broadcasted_iota(I32, (x_ref.shape[0], 1), 0)
        x = _unpack_pairs(jnp.where(row < nv_ref[i], x_ref[...], 0)).astype(BF16)
        u = _dot(x, w1b_ref[...]) + b1_ref[0]
        glu = jnp.minimum(u[:, :dff], SWIGLU_LIMIT)
        lin = jnp.clip(u[:, dff:], -SWIGLU_LIMIT, SWIGLU_LIMIT)
        act = glu * jax.nn.sigmoid(SWIGLU_ALPHA * glu) * (lin + 1.0)
        y_ref[...] = _pack_pairs(_dot(act.astype(BF16), w2b_ref[...]) + b2_ref[0])

    @pl.when(nv_ref[i] <= 0)
    def _():
        y_ref[...] = jnp.zeros_like(y_ref)


def _ffn(xs, tile_expert, n_valid, layer, w1, b1, w2, b2):
    ns, wd = xs.shape
    depth, ne, d, n1 = w1.shape
    dff = w2.shape[2]
    row = lambda i, te, nv: (i, 0)
    exp4 = lambda i, te, nv: (layer, te[i], 0, 0)
    return pl.pallas_call(
        _ffn_kernel,
        grid_spec=pltpu.PrefetchScalarGridSpec(
            num_scalar_prefetch=2,
            grid=(ns // TMX,),
            in_specs=[pl.BlockSpec((TMX, wd), row),
                      pl.BlockSpec((None, 1, d, n1), exp4), pl.BlockSpec((None, 1, 1, n1), exp4),
                      pl.BlockSpec((None, 1, dff, d), exp4), pl.BlockSpec((None, 1, 1, d), exp4)],
            out_specs=pl.BlockSpec((TMX, wd), row),
            scratch_shapes=[pltpu.VMEM((d, n1), BF16), pltpu.VMEM((dff, d), BF16)]),
        out_shape=jax.ShapeDtypeStruct((ns, wd), I32),
        compiler_params=pltpu.CompilerParams(dimension_semantics=("arbitrary",), vmem_limit_bytes=FFN_VMEM_LIMIT),
        name="moe_ffn",
    )(tile_expert, n_valid, xs, w1, b1.reshape(depth, ne, 1, n1), w2, b2.reshape(depth, ne, 1, d))


def _combine_kernel(wt_ref, s_ref, g_ref, yk_ref, o_ref):
    wt = wt_ref[...]
    f = wt[:, 0:1] * _unpack_pairs(yk_ref[0])
    for kk in range(1, TOP_K):
        f = f + wt[:, kk:kk + 1] * _unpack_pairs(yk_ref[kk])
    o_ref[0] = s_ref[0] + g_ref[...] * f


def _combine(yk, wts, s, mod, which, skip):
    bsz, t, d = s.shape
    nt = t // TM - skip
    return pl.pallas_call(
        _combine_kernel,
        grid=(bsz, nt),
        in_specs=[pl.BlockSpec((TM, TOP_K), lambda b, i: (b * nt + i, 0)),
                  pl.BlockSpec((1, TM, d), lambda b, i: (b, i + skip, 0)),
                  _mod_spec_d(bsz, which, d, skip),
                  pl.BlockSpec((TOP_K, TM, yk.shape[-1]), lambda b, i: (0, b * nt + i, 0))],
        out_specs=pl.BlockSpec((1, TM, d), lambda b, i: (b, i, 0)),
        out_shape=jax.ShapeDtypeStruct((bsz, nt * TM, d), F32),
        input_output_aliases={} if skip else {1: 0},
        compiler_params=_params("arbitrary", "arbitrary"),
        name="moe_combine",
    )(wts.T, s, mod, yk)


def _moe(s, gain, mod, layer, w_router, b_router, w1, b1, w2, b2, skip=0):
    bsz, t, d = s.shape
    n = bsz * (t - skip * TM)
    assert n % SC_CHUNK == 0
    hp, ids, wts, ranks, counts = _moe_pre(s, gain, mod, w_router, b_router, skip)
    cnt = counts[:, 0].astype(I32)
    padded = ((cnt + TMX - 1) // TMX) * TMX
    ends = jnp.cumsum(padded)
    offs = ends - padded
    n_tiles = (TOP_K * n + N_EXPERTS * (TMX - 1) + TMX - 1) // TMX
    tile_start = jnp.arange(n_tiles, dtype=I32) * TMX
    te = jnp.sum(tile_start[:, None] >= ends[None, :], axis=1).astype(I32)
    te_last = jnp.max(jnp.where(cnt > 0, jnp.arange(N_EXPERTS, dtype=I32), 0))
    te = jnp.minimum(te, te_last)
    group_end = jnp.sum(jnp.where(te[:, None] == jnp.arange(N_EXPERTS, dtype=I32), offs + cnt, 0), axis=-1)
    n_valid = jnp.where(tile_start < ends[-1], jnp.clip(group_end - tile_start, 0, TMX), 0).astype(I32)
    slots = ranks + jnp.sum(jnp.where(ids[..., None] == jnp.arange(N_EXPERTS, dtype=I32), offs, 0), axis=-1)
    idx3 = slots.reshape(TOP_K, n // SC_CHUNK, SC_CHUNK).transpose(1, 0, 2)
    xs = _dispatch(hp.reshape(n, d // 2), idx3, n_tiles * TMX)
    ys = _ffn(xs, te, n_valid, layer, w1, b1, w2, b2)
    return _combine(_gather(ys, idx3), wts, s, mod, 5, skip)


def kernel(x, c, ctx, c_ctx, w_ada, b_ada, g_norm_mix, g_norm_ffn, na_w_qkv, na_q_gain, na_k_gain, na_rpb,
           na_w_o, ml_w_in, ml_b_gate, ml_mh_gain, ml_w_o, moe_w_router, moe_b_router, moe_w1, moe_b1,
           moe_w2, moe_b2):
    bsz, seq, d = x.shape
    n_ctx = ctx.shape[1]
    depth = w_ada.shape[0]
    rows = seq // GRID_W
    n_mod_rows = ((bsz + 1 + 7) // 8) * 8
    cc = jnp.concatenate([c, c_ctx[None, :], jnp.zeros((n_mod_rows - bsz - 1, d), F32)], axis=0)
    mod_all = _ada(cc, w_ada, b_ada).reshape(depth, n_mod_rows, 6, 1, d)
    s = jnp.concatenate([ctx, x], axis=1)
    for i in range(depth):
        j = i // 2
        mod = mod_all[i]
        if i % 2 == 0:
            qkv = _na_qkv(s, g_norm_mix[i], mod, na_w_qkv[j].astype(BF16))
            y = _na_attn(qkv, _na_bias(na_rpb[j], rows), na_q_gain[j], na_k_gain[j], n_ctx)
            s = _oproj(y, na_w_o[j].astype(BF16), s, mod, 2)
        else:
            q, kt, v, og, gr, gc, gm = _ml_proj(s, g_norm_mix[i], mod, ml_w_in[j], ml_b_gate[j], n_ctx)
            y = _mlstm(q, kt, v, og, gr, gc, gm, ml_mh_gain[j])
            s = _oproj(y, ml_w_o[j].astype(BF16), s, mod, 2)
        skip = n_ctx // TM if i == depth - 1 else 0
        s = _moe(s, g_norm_ffn[i], mod, i, moe_w_router[i], moe_b_router[i], moe_w1, moe_b1, moe_w2, moe_b2, skip)
    return s
```

```python
import functools
import math

import numpy as np
import jax
import jax.numpy as jnp
from jax import lax
from jax.experimental import pallas as pl
from jax.experimental.pallas import tpu as pltpu
from jax.experimental.pallas import tpu_sc as plsc

F32 = jnp.float32
BF16 = jnp.bfloat16
I32 = jnp.int32

EPS = 1e-6
GRID_W = 64
NA_HEADS = 16
NA_HEAD_DIM = 64
NA_KR = 8
NA_KC = 16
NA_QROWS = 4
NA_KROWS = NA_QROWS + NA_KR - 1
NA_NB = 4
ML_HEADS = 8
ML_DV = 128
ML_DQK = 64
ML_CHUNK = 256
ML_GATE_CAP = 15.0
ROPE_THETA = 10000.0
N_EXPERTS = 32
TOP_K = 4
SWIGLU_LIMIT = 7.0
SWIGLU_ALPHA = 1.702

TM = 256
TMX = 512
LANES = 128
NEG = -1e30
VMEM_LIMIT = 48 * 1024 * 1024
SC_CORES = 2
SC_SUBCORES = 16
SC_CHUNK = 64
FFN_VMEM_LIMIT =56 * 1024 * 1024


def _dot(a, b):
    return jnp.dot(a, b, preferred_element_type=F32)


def _dot_nt(a, b):
    return lax.dot_general(a, b, (((1,), (1,)), ((), ())), preferred_element_type=F32)


def _split3(x):
    hi = x.astype(BF16)
    r1 = x - hi.astype(F32)
    mid = r1.astype(BF16)
    lo = (r1 - mid.astype(F32)).astype(BF16)
    return hi, mid, lo


def _params(*sem):
    return pltpu.CompilerParams(dimension_semantics=sem, vmem_limit_bytes=VMEM_LIMIT)


def _norm_mod(x, gain, shift, scale):
    r = lax.rsqrt(jnp.mean(x * x, axis=-1, keepdims=True) + EPS)
    return (x * r * gain) * (1.0 + scale) + shift


def _mod_spec_d(n_batch, which, d, skip=0):
    return pl.BlockSpec((None, None, 1, d), lambda b, t: (jnp.where(t + skip == 0, n_batch, b), which, 0, 0))


def _ada_kernel(c_ref, w_ref, b_ref, o_ref):
    c = c_ref[...]
    s = c * jax.nn.sigmoid(c)
    o_ref[0] = _dot(s.astype(BF16), w_ref[0].astype(BF16)) + b_ref[0]


def _ada(cc, w_ada, b_ada):
    depth, d, n = w_ada.shape
    bn = 1536
    return pl.pallas_call(
        _ada_kernel,
        grid=(depth, n // bn),
        in_specs=[pl.BlockSpec((cc.shape[0], d), lambda l, j: (0, 0)),
                  pl.BlockSpec((1, d, bn), lambda l, j: (l, 0, j)),
                  pl.BlockSpec((1, 1, bn), lambda l, j: (l, 0, j))],
        out_specs=pl.BlockSpec((1, cc.shape[0], bn), lambda l, j: (l, 0, j)),
        out_shape=jax.ShapeDtypeStruct((depth, cc.shape[0], n), F32),
        compiler_params=_params("arbitrary", "arbitrary"),
        name="ada",
    )(cc, w_ada, b_ada.reshape(depth, 1, n))


def _na_qkv_kernel(x_ref, g_ref, sh_ref, sc_ref, w_ref, o_ref):
    h = _norm_mod(x_ref[0], g_ref[...], sh_ref[...], sc_ref[...])
    o_ref[0] = _dot(h.astype(BF16), w_ref[...]).astype(BF16)


def _na_qkv(s, gain, mod, w):
    bsz, t, d = s.shape
    n = w.shape[1]
    return pl.pallas_call(
        _na_qkv_kernel,
        grid=(bsz, t // TM),
        in_specs=[pl.BlockSpec((1, TM, d), lambda b, i: (b, i, 0)),
                  pl.BlockSpec((1, d), lambda b, i: (0, 0)),
                  _mod_spec_d(bsz, 0, d), _mod_spec_d(bsz, 1, d),
                  pl.BlockSpec((d, n), lambda b, i: (0, 0))],
        out_specs=pl.BlockSpec((1, TM, n), lambda b, i: (b, i, 0)),
        out_shape=jax.ShapeDtypeStruct((bsz, t, n), BF16),
        compiler_params=_params("arbitrary", "arbitrary"),
        name="na_qkv",
    )(s, gain.reshape(1, d), mod, mod, w)


def _na_bias(rpb, rows):
    nblk = rows // NA_QROWS
    kmax = rows - NA_KROWS
    nrr, ncc = 2 * NA_KR - 1, 2 * NA_KC - 1
    qc = np.arange(GRID_W)[:, None]
    kc = np.arange(GRID_W)[None, :]
    c0 = np.clip(qc - NA_KC // 2, 0, GRID_W - NA_KC)
    col_ok = (kc >= c0) & (kc < c0 + NA_KC)
    col_pick = np.eye(ncc)[np.clip(kc - qc + NA_KC - 1, 0, ncc - 1)]
    row_pick, valid = [], []
    for blk in (0, 1, nblk - 1):
        rs = blk * NA_QROWS
        ks = min(max(rs - NA_KR // 2, 0), kmax)
        qr = rs + np.arange(NA_QROWS)[:, None]
        kr = ks + np.arange(NA_KROWS)[None, :]
        r0 = np.clip(qr - NA_KR // 2, 0, rows - NA_KR)
        row_ok = (kr >= r0) & (kr < r0 + NA_KR)
        row_pick.append(np.eye(nrr)[np.clip(kr - qr + NA_KR - 1, 0, nrr - 1)])
        valid.append(row_ok[:, None, :, None] & col_ok[None, :, None, :])
    bias = jnp.einsum('vrka,pxab,qcb->pvxrqkc', jnp.asarray(np.stack(row_pick), F32),
                      rpb.reshape(NA_HEADS // 2, 2, nrr, ncc), jnp.asarray(col_pick, F32),
                      precision=lax.Precision.HIGHEST)
    bias = jnp.where(np.stack(valid)[None, :, None], bias, NEG)
    return bias.reshape(NA_HEADS // 2, 3, 2, NA_QROWS * GRID_W, NA_KROWS * GRID_W).astype(F32)


def _na_attn_kernel(q_ref, k_ref, v_ref, bias_ref, qg_ref, kg_ref, o_ref, kn_ref, va_ref, vb_ref, *, n_ctx, rows):
    t = pl.program_id(2)
    first = lax.broadcasted_iota(I32, (1, LANES), 1) < NA_HEAD_DIM
    nk = NA_KROWS * GRID_W

    def head_rms(z, gain):
        z2 = z * z
        sa = jnp.sum(jnp.where(first, z2, 0.0), axis=-1, keepdims=True)
        sb = jnp.sum(jnp.where(first, 0.0, z2), axis=-1, keepdims=True)
        r = jnp.where(first, lax.rsqrt(sa / NA_HEAD_DIM + EPS), lax.rsqrt(sb / NA_HEAD_DIM + EPS))
        return z * r * gain

    nb = q_ref.shape[0]

    @pl.when(t == 0)
    def _():
        def body(i, carry):
            off = pl.multiple_of(i * TM, TM)
            for bb in range(nb):
                kn_ref[bb, pl.ds(off, TM), :] = head_rms(k_ref[bb, pl.ds(off, TM), :].astype(F32),
                                                         kg_ref[...]).astype(BF16)
                v = v_ref[bb, pl.ds(off, TM), :]
                va_ref[bb, pl.ds(off, TM), :] = jnp.where(first, v, jnp.ones_like(v))
                vb_ref[bb, pl.ds(off, TM), :] = jnp.where(first, jnp.ones_like(v), v)
            return carry
        lax.fori_loop(0, k_ref.shape[1] // TM, body, 0)

    def heads(bb):
        qn = (head_rms(q_ref[bb].astype(F32), qg_ref[...]) * (NA_HEAD_DIM ** -0.5)).astype(BF16)
        zero = jnp.zeros_like(qn)
        return (jnp.where(first, qn, zero), va_ref), (jnp.where(first, zero, qn), vb_ref)

    def finish(bb, outs):
        oa, ob = outs
        o_ref[bb] = jnp.where(first, oa / pltpu.roll(oa, NA_HEAD_DIM, 1),
                              ob / pltpu.roll(ob, NA_HEAD_DIM, 1)).astype(o_ref.dtype)

    @pl.when(t == 0)
    def _():
        for bb in range(nb):
            outs = []
            for qh, vh in heads(bb):
                s = _dot_nt(qh, kn_ref[bb, 0:n_ctx, :])
                m = jnp.max(s, axis=-1, keepdims=True)
                outs.append(_dot(jnp.exp((s - m).astype(BF16)), vh[bb, 0:n_ctx, :]))
            finish(bb, outs)

    @pl.when(t > 0)
    def _():
        blk = t - 1
        nblk = rows // NA_QROWS
        ks = jnp.clip(blk * NA_QROWS - NA_KR // 2, 0, rows - NA_KROWS)
        start = pl.multiple_of(n_ctx + ks * GRID_W, GRID_W)
        kind = jnp.where(blk == 0, 0, jnp.where(blk == nblk - 1, 2, 1))
        for bb in range(nb):
            kw = kn_ref[bb, pl.ds(start, nk), :]
            kc = kn_ref[bb, 0:n_ctx, :]
            outs = []
            for hh, (qh, vh) in enumerate(heads(bb)):
                sw = _dot_nt(qh, kw) + bias_ref[0, kind, hh]
                sc = _dot_nt(qh, kc)
                m = jnp.maximum(jnp.max(sw, axis=-1, keepdims=True), jnp.max(sc, axis=-1, keepdims=True))
                pw = jnp.exp((sw - m).astype(BF16))
                pc = jnp.exp((sc - m).astype(BF16))
                outs.append(_dot(pw, vh[bb, pl.ds(start, nk), :]) + _dot(pc, vh[bb, 0:n_ctx, :]))
            finish(bb, outs)


def _na_attn(qkv, bias, q_gain, k_gain, n_ctx):
    bsz, t, _ = qkv.shape
    d = NA_HEADS * NA_HEAD_DIM
    rows = (t - n_ctx) // GRID_W
    assert n_ctx == TM and NA_QROWS * GRID_W == TM and rows % NA_QROWS == 0 and rows >= NA_KROWS
    hp = NA_HEADS // 2
    nq, nk = bias.shape[-2:]
    gq = jnp.tile(q_gain, 2).reshape(1, LANES)
    gk = jnp.tile(k_gain, 2).reshape(1, LANES)
    nb = NA_NB if bsz % NA_NB == 0 else 1
    return pl.pallas_call(
        functools.partial(_na_attn_kernel, n_ctx=n_ctx, rows=rows),
        grid=(hp, bsz // nb, t // TM),
        in_specs=[pl.BlockSpec((nb, TM, LANES), lambda h, b, i: (b, i, h)),
                  pl.BlockSpec((nb, t, LANES), lambda h, b, i: (b, 0, hp + h)),
                  pl.BlockSpec((nb, t, LANES), lambda h, b, i: (b, 0, 2 * hp + h)),
                  pl.BlockSpec((1, 3, 2, nq, nk), lambda h, b, i: (h, 0, 0, 0, 0)),
                  pl.BlockSpec((1, LANES), lambda h, b, i: (0, 0)),
                  pl.BlockSpec((1, LANES), lambda h, b, i: (0, 0))],
        out_specs=pl.BlockSpec((nb, TM, LANES), lambda h, b, i: (b, i, h)),
        out_shape=jax.ShapeDtypeStruct((bsz, t, d), BF16),
        scratch_shapes=[pltpu.VMEM((nb, t, LANES), BF16)] * 3,
        compiler_params=_params("arbitrary", "arbitrary", "arbitrary"),
        name="na_attn",
    )(qkv, qkv, qkv, bias, gq, gk)


def _rope_tables(n_ctx, seq):
    lane = np.arange(LANES)
    dd = lane % ML_DQK
    grp = dd // (ML_DQK // 2)
    j = dd % (ML_DQK // 2)
    n = ML_DQK // 4
    inv = ROPE_THETA ** (-(j % n).astype(np.float64) / n)
    pos = np.arange(seq)
    p = np.where(grp[None, :] == 0, (pos // GRID_W)[:, None], (pos % GRID_W)[:, None]).astype(np.float64)
    ang = p * inv[None, :]
    cos = np.cos(ang)
    sin = np.where(j[None, :] < n, -np.sin(ang), np.sin(ang))
    cos = np.concatenate([np.ones((n_ctx, LANES)), cos], axis=0)
    sin = np.concatenate([np.zeros((n_ctx, LANES)), sin], axis=0)
    return jnp.asarray(cos, F32), jnp.asarray(sin, F32)


def _ml_proj_kernel(x_ref, g_ref, sh_ref, sc_ref, wm_ref, wg_ref, wgt_ref, bgr_ref, bgc_ref, cos_ref, sin_ref,
                    q_ref, kt_ref, v_ref, og_ref, gr_ref, gc_ref, gm_ref):
    h = _norm_mod(x_ref[0], g_ref[...], sh_ref[...], sc_ref[...]).astype(BF16)
    main = _dot(h, wm_ref[...])
    nq = q_ref.shape[-1]
    nv = v_ref.shape[-1]
    cos = cos_ref[...]
    sin = sin_ref[...]
    n = ML_DQK // 4
    lane = lax.broadcasted_iota(I32, (1, LANES), 1)
    low = (lane % (2 * n)) < n

    def rope(z):
        zr = jnp.where(low, pltpu.roll(z, LANES - n, 1), pltpu.roll(z, n, 1))
        return z * cos + zr * sin

    for j in range(nq // LANES):
        sl = slice(j * LANES, (j + 1) * LANES)
        q_ref[0, :, sl] = (rope(main[:, sl]) * (ML_DQK ** -0.5)).astype(BF16)
        kt_ref[0, sl, :] = rope(main[:, nq + j * LANES: nq + (j + 1) * LANES]).T.astype(BF16)
    v_ref[0] = main[:, 2 * nq: 2 * nq + nv].astype(BF16)
    og_ref[0] = main[:, 2 * nq + nv:].astype(BF16)

    nh = ML_HEADS
    ti = lax.broadcasted_iota(I32, (TM, TM), 0)
    si = lax.broadcasted_iota(I32, (TM, TM), 1)
    le = (si <= ti).astype(BF16)
    ge = (si >= ti).astype(BF16)

    def capped(pre):
        return ML_GATE_CAP * jnp.tanh(pre / ML_GATE_CAP)

    def exact_cum(tri, x, nt):
        parts = _split3(x)
        if nt:
            return sum(_dot_nt(p, tri) for p in parts)
        return sum(_dot(tri, p) for p in parts)

    gc = capped(_dot(h, wg_ref[...]) + bgr_ref[...])
    lf = jax.nn.log_sigmoid(gc)
    pre = exact_cum(le, lf, False)
    suf = exact_cum(ge, lf, False)
    cum = jnp.where(lane < 2 * nh, pre, suf)
    r = gc - pltpu.roll(cum, LANES - nh, 1)
    is_i = (lane // nh) % 2 == 0
    gc_ref[0] = jnp.where(is_i, r, cum)[:, :4 * nh]
    row = lax.broadcasted_iota(I32, (TM, 1), 0)
    pmax = r
    smax = r
    step = 1
    while step < TM:
        pmax = jnp.maximum(pmax, jnp.where(row >= step, pltpu.roll(pmax, step, 0), NEG))
        smax = jnp.maximum(smax, jnp.where(row < TM - step, pltpu.roll(smax, TM - step, 0), NEG))
        step *= 2
    gm_ref[0] = jnp.where(lane < 2 * nh, pmax, smax)[:, :4 * nh]

    gr = capped(_dot_nt(wgt_ref[...], h) + bgc_ref[...])
    lfr = jax.nn.log_sigmoid(gr)
    pre_r = exact_cum(le, lfr, True)
    suf_r = exact_cum(ge, lfr, True)
    gr_ref[0, 0 * nh:1 * nh, :] = gr[0 * nh:1 * nh] - pre_r[1 * nh:2 * nh]
    gr_ref[0, 1 * nh:2 * nh, :] = pre_r[1 * nh:2 * nh]
    gr_ref[0, 2 * nh:3 * nh, :] = gr[2 * nh:3 * nh] - suf_r[3 * nh:4 * nh]
    gr_ref[0, 3 * nh:4 * nh, :] = suf_r[3 * nh:4 * nh]


def _ml_proj(s, gain, mod, w_in, b_gate, n_ctx):
    bsz, t, d = s.shape
    nq = ML_HEADS * ML_DQK
    nv = ML_HEADS * ML_DV
    ng = 4 * ML_HEADS
    assert TM == ML_CHUNK
    wm = w_in[:, :2 * nq + nv + d].astype(BF16)
    wg = w_in[:, 2 * nq + nv + d:]
    wg_pad = jnp.pad(wg, ((0, 0), (0, LANES - ng))).astype(BF16)
    wgt = wg.T.astype(BF16)
    bgr = jnp.pad(b_gate, (0, LANES - ng)).reshape(1, LANES)
    bgc = b_gate.reshape(ng, 1)
    cos, sin = _rope_tables(n_ctx, t - n_ctx)
    tile = lambda n: pl.BlockSpec((1, TM, n), lambda b, i: (b, i, 0))
    const = lambda a: pl.BlockSpec(a.shape, lambda b, i: (0, 0))
    return pl.pallas_call(
        _ml_proj_kernel,
        grid=(bsz, t // TM),
        in_specs=[tile(d), pl.BlockSpec((1, d), lambda b, i: (0, 0)),
                  _mod_spec_d(bsz, 0, d), _mod_spec_d(bsz, 1, d),
                  const(wm), const(wg_pad), const(wgt), const(bgr), const(bgc),
                  pl.BlockSpec((TM, LANES), lambda b, i: (i, 0)),
                  pl.BlockSpec((TM, LANES), lambda b, i: (i, 0))],
        out_specs=[tile(nq), pl.BlockSpec((1, nq, TM), lambda b, i: (b, 0, i)), tile(nv), tile(d),
                   pl.BlockSpec((1, ng, TM), lambda b, i: (b, 0, i)),
                   tile(ng), tile(ng)],
        out_shape=[jax.ShapeDtypeStruct((bsz, t, nq), BF16), jax.ShapeDtypeStruct((bsz, nq, t), BF16),
                   jax.ShapeDtypeStruct((bsz, t, nv), BF16), jax.ShapeDtypeStruct((bsz, t, d), BF16),
                   jax.ShapeDtypeStruct((bsz, ng, t), F32), jax.ShapeDtypeStruct((bsz, t, ng), F32),
                   jax.ShapeDtypeStruct((bsz, t, ng), F32)],
        compiler_params=_params("arbitrary", "arbitrary"),
        name="ml_proj",
    )(s, gain.reshape(1, d), mod, mod, wm, wg_pad, wgt, bgr, bgc, cos, sin)


def _mlstm_kernel(q_ref, kt_ref, v_ref, og_ref, gr_ref, gc_ref, gm_ref, mg_ref, o_ref, hf_ref, hb_ref, c_ref):
    L = ML_CHUNK
    nchunk = q_ref.shape[1] // L
    first = lax.broadcasted_iota(I32, (1, LANES), 1) < ML_DQK
    first_row = lax.broadcasted_iota(I32, (LANES, 1), 0) < ML_DQK
    ti = lax.broadcasted_iota(I32, (L, L), 0)
    si = lax.broadcasted_iota(I32, (L, L), 1)
    masks = (si <= ti, si >= ti)
    ones_blk = jnp.ones((L, ML_DV), BF16)

    c_ref[...] = jnp.zeros_like(c_ref)

    def chunk(c, a, bw, m0):
        off = pl.multiple_of(c * L, L)
        q = q_ref[0, pl.ds(off, L), :]
        kt = kt_ref[0, :, pl.ds(off, L)]
        qa = jnp.where(first if a == 0 else jnp.logical_not(first), q, jnp.zeros_like(q))
        kta = jnp.where(first_row if a == 0 else jnp.logical_not(first_row), kt, jnp.zeros_like(kt))
        v = v_ref[0, pl.ds(off, L), a * ML_DV:(a + 1) * ML_DV]
        w = 2 if bw else 0
        r_row = gr_ref[0, 0, 2 * w + a: 2 * w + a + 1, pl.ds(off, L)]
        cum_row = gr_ref[0, 0, 2 * (w + 1) + a: 2 * (w + 1) + a + 1, pl.ds(off, L)]
        cum_col = gc_ref[0, 0, pl.ds(off, L), 2 * (w + 1) + a: 2 * (w + 1) + a + 1]
        tot = cum_row[:, 0:1] if bw else cum_row[:, L - 1:L]
        mask = masks[1 if bw else 0]
        mx = gm_ref[0, 0, pl.ds(off, L), w + a: w + a + 1]
        mm = jnp.broadcast_to(jnp.maximum(m0, mx), (L, LANES))
        cum_b = jnp.broadcast_to(cum_col, (L, LANES))
        e = jnp.exp(jnp.where(mask, r_row - jnp.concatenate([mm] * (L // LANES), axis=1), NEG))
        s = (_dot(qa, kt) * e).astype(BF16)
        v2 = jnp.concatenate([v, ones_blk], axis=1)
        idx = 2 * a + (1 if bw else 0)
        cst = c_ref[idx]
        w_inter = jnp.exp(m0 - mm)
        both = _dot(s, v2) + jnp.concatenate([w_inter, w_inter], axis=1) * _dot(qa, cst.astype(BF16))
        num = both[:, :ML_DV]
        den = both[:, ML_DV:]
        hout = num / jnp.maximum(jnp.abs(den), jnp.exp(-cum_b - mm))
        dst = hb_ref if bw else hf_ref
        dst[pl.ds(off, L), a * ML_DV:(a + 1) * ML_DV] = hout
        m_loc = tot + jnp.max(r_row, axis=1, keepdims=True)
        kte = (kta.astype(F32) * jnp.exp(tot + r_row - m_loc)).astype(BF16)
        u = _dot(kte, v2)
        m_new = jnp.maximum(tot + m0, m_loc)
        c_ref[idx] = jnp.exp(tot + m0 - m_new) * cst + jnp.exp(m_loc - m_new) * u
        return m_new

    def body(i, ms):
        cb = jnp.where(i == 0, 0, nchunk - i)
        return (chunk(i, 0, False, ms[0]), chunk(i, 1, False, ms[1]),
                chunk(cb, 0, True, ms[2]), chunk(cb, 1, True, ms[3]))

    z = jnp.zeros((1, 1), F32)
    lax.fori_loop(0, nchunk, body, (z, z, z, z))

    def epilogue(i, carry):
        off = pl.multiple_of(i * L, L)
        for a in range(2):
            sl = slice(a * ML_DV, (a + 1) * ML_DV)
            hs = hf_ref[pl.ds(off, L), sl] + hb_ref[pl.ds(off, L), sl]
            hn = hs * lax.rsqrt(jnp.mean(hs * hs, axis=-1, keepdims=True) + EPS) * mg_ref[:, sl]
            o_ref[0, pl.ds(off, L), sl] = (hn * jax.nn.sigmoid(og_ref[0, pl.ds(off, L), sl].astype(F32))).astype(BF16)
        return carry

    lax.fori_loop(0, nchunk, epilogue, 0)


def _mlstm(q, kt, v, og, gr, gc, gm, mh_gain):
    bsz, t, _ = q.shape
    hp = ML_HEADS // 2
    nv = ML_HEADS * ML_DV
    w2 = 2 * ML_DV
    gr = gr.reshape(bsz, 4, hp, 2, t).transpose(0, 2, 1, 3, 4).reshape(bsz, hp, 8, t)
    gc = gc.reshape(bsz, t, 4, hp, 2).transpose(0, 3, 1, 2, 4).reshape(bsz, hp, t, 8)
    gm = gm.reshape(bsz, t, 4, hp, 2)[:, :, 0::2].transpose(0, 3, 1, 2, 4).reshape(bsz, hp, t, 4)
    return pl.pallas_call(
        _mlstm_kernel,
        grid=(bsz, hp),
        in_specs=[pl.BlockSpec((1, t, LANES), lambda b, h: (b, 0, h)),
                  pl.BlockSpec((1, LANES, t), lambda b, h: (b, h, 0)),
                  pl.BlockSpec((1, t, w2), lambda b, h: (b, 0, h)),
                  pl.BlockSpec((1, t, w2), lambda b, h: (b, 0, h)),
                  pl.BlockSpec((1, 1, 8, t), lambda b, h: (b, h, 0, 0)),
                  pl.BlockSpec((1, 1, t, 8), lambda b, h: (b, h, 0, 0)),
                  pl.BlockSpec((1, 1, t, 4), lambda b, h: (b, h, 0, 0)),
                  pl.BlockSpec((1, w2), lambda b, h: (0, h))],
        out_specs=pl.BlockSpec((1, t, w2), lambda b, h: (b, 0, h)),
        out_shape=jax.ShapeDtypeStruct((bsz, t, nv), BF16),
        scratch_shapes=[pltpu.VMEM((t, w2), F32), pltpu.VMEM((t, w2), F32),
                        pltpu.VMEM((4, LANES, w2), F32)],
        compiler_params=_params("arbitrary", "arbitrary"),
        name="mlstm",
    )(q, kt, v, og, gr, gc, gm, mh_gain.reshape(1, nv))


def _moe_pre_kernel(y_ref, wo_ref, x_ref, g1_ref, g_ref, sh_ref, sc_ref, whi_ref, wlo_ref, br_ref,
                    s_ref, h_ref, id_ref, wt_ref, rk_ref, cnt_ref):
    step = pl.program_id(0) * pl.num_programs(1) + pl.program_id(1)

    @pl.when(step == 0)
    def _():
        cnt_ref[...] = jnp.zeros_like(cnt_ref)

    x = x_ref[0] + g1_ref[...] * _dot(y_ref[0], wo_ref[...])
    s_ref[0] = x
    h = _norm_mod(x, g_ref[...], sh_ref[...], sc_ref[...])
    h_ref[0] = _pack_pairs(h)
    hhi = h.astype(BF16)
    hlo = (h - hhi.astype(F32)).astype(BF16)
    whi = whi_ref[...]
    logits = _dot_nt(whi, hhi) + _dot_nt(whi, hlo) + _dot_nt(wlo_ref[...], hhi) + br_ref[...]
    ne = logits.shape[0]
    eidx = lax.broadcasted_iota(I32, logits.shape, 0)
    sels, vals, ids = [], [], []
    l = logits
    for _ in range(TOP_K):
        mval = jnp.max(l, axis=0, keepdims=True)
        idx = jnp.min(jnp.where(l == mval, eidx, ne), axis=0, keepdims=True)
        sel = eidx == idx
        l = jnp.where(sel, -jnp.inf, l)
        sels.append(sel)
        vals.append(mval)
        ids.append(idx)
    ex = [jnp.exp(v - vals[0]) for v in vals]
    tot = ex[0] + ex[1] + ex[2] + ex[3]
    member = sels[0] | sels[1] | sels[2] | sels[3]
    ti = lax.broadcasted_iota(I32, (TM, TM), 0)
    si = lax.broadcasted_iota(I32, (TM, TM), 1)
    before = (ti < si).astype(BF16)
    rank_e = _dot(member.astype(BF16), before) + cnt_ref[:, 0:1]
    for kk in range(TOP_K):
        id_ref[kk:kk + 1, :] = ids[kk]
        wt_ref[kk:kk + 1, :] = ex[kk] / tot
        rk_ref[kk:kk + 1, :] = jnp.sum(jnp.where(sels[kk], rank_e, 0.0), axis=0, keepdims=True).astype(I32)
    cnt_ref[...] = cnt_ref[...] + jnp.sum(member.astype(F32), axis=1, keepdims=True)


def _moe_pre(y, w_o, s, gain, mod, w_router, b_router, skip):
    bsz, t, d = s.shape
    nt = t // TM - skip
    n = bsz * nt * TM
    wt = w_router.T
    whi = wt.astype(BF16)
    wlo = (wt - whi.astype(F32)).astype(BF16)
    tok = lambda dt: jax.ShapeDtypeStruct((TOP_K, n), dt)
    tok_spec = pl.BlockSpec((TOP_K, TM), lambda b, i: (0, b * nt + i))
    tile = lambda w: pl.BlockSpec((1, TM, w), lambda b, i: (b, i + skip, 0))
    const = lambda a: pl.BlockSpec(a.shape, lambda b, i: (0, 0))
    return pl.pallas_call(
        _moe_pre_kernel,
        grid=(bsz, nt),
        in_specs=[tile(y.shape[-1]), const(w_o), tile(d), _mod_spec_d(bsz, 2, d, skip),
                  pl.BlockSpec((1, d), lambda b, i: (0, 0)),
                  _mod_spec_d(bsz, 3, d, skip), _mod_spec_d(bsz, 4, d, skip),
                  const(whi), const(wlo), pl.BlockSpec((N_EXPERTS, 1), lambda b, i: (0, 0))],
        out_specs=[tile(d), pl.BlockSpec((1, TM, d // 2), lambda b, i: (b, i, 0)), tok_spec, tok_spec, tok_spec,
                   pl.BlockSpec((N_EXPERTS, LANES), lambda b, i: (0, 0))],
        out_shape=[jax.ShapeDtypeStruct(s.shape, F32),
                   jax.ShapeDtypeStruct((bsz, nt * TM, d // 2), I32), tok(I32), tok(F32), tok(I32),
                   jax.ShapeDtypeStruct((N_EXPERTS, LANES), F32)],
        input_output_aliases={2: 0},
        compiler_params=_params("arbitrary", "arbitrary"),
        name="moe_pre",
    )(y, w_o, s, mod, gain.reshape(1, d), mod, mod, whi, wlo, b_router.reshape(N_EXPERTS, 1))


def _sc_mesh():
    return plsc.VectorSubcoreMesh(core_axis_name="c", subcore_axis_name="s",
                                  num_cores=SC_CORES, num_subcores=SC_SUBCORES)


def _sc_worker_chunks(nchunks, body):
    n_workers = SC_CORES * SC_SUBCORES
    per_worker = -(-nchunks // n_workers)
    wid = lax.axis_index("s") * SC_CORES + lax.axis_index("c")

    @pl.loop(0, per_worker)
    def _(i):
        c = wid * per_worker + i

        @pl.when(c < nchunks)
        def _():
            body(c)


def _dispatch(hp, idx3, n_slots):
    n, w = hp.shape
    nchunks, kk, ch = idx3.shape

    @functools.partial(pl.kernel, mesh=_sc_mesh(), out_type=jax.ShapeDtypeStruct((n_slots, w), hp.dtype),
                       scratch_types=[pltpu.VMEM((kk, ch), I32), pltpu.VMEM((ch, w), hp.dtype)],
                       name="moe_dispatch")
    def scatter(src_hbm, idx_hbm, out_hbm, idx_v, rows_v):
        def body(c):
            pltpu.sync_copy(idx_hbm.at[c], idx_v)
            pltpu.sync_copy(src_hbm.at[pl.ds(c * ch, ch)], rows_v)
            for j in range(kk):
                pltpu.sync_copy(rows_v, out_hbm.at[idx_v.at[j]])
        _sc_worker_chunks(nchunks, body)

    return scatter(hp, idx3)


def _gather(ys, idx3):
    nchunks, kk, ch = idx3.shape
    w = ys.shape[1]

    @functools.partial(pl.kernel, mesh=_sc_mesh(), out_type=jax.ShapeDtypeStruct((kk, nchunks * ch, w), ys.dtype),
                       scratch_types=[pltpu.VMEM((kk, ch), I32), pltpu.VMEM((ch, w), ys.dtype)],
                       name="moe_gather")
    def gather(tab_hbm, idx_hbm, out_hbm, idx_v, rows_v):
        def body(c):
            pltpu.sync_copy(idx_hbm.at[c], idx_v)
            for j in range(kk):
                pltpu.sync_copy(tab_hbm.at[idx_v.at[j]], rows_v)
                pltpu.sync_copy(rows_v, out_hbm.at[j, pl.ds(c * ch, ch)])
        _sc_worker_chunks(nchunks, body)

    return gather(ys, idx3)


def _pack_pairs(x):
    w = x.shape[1] // 2
    bits = lax.bitcast_convert_type(x.astype(BF16).astype(F32), jnp.uint32)
    return lax.bitcast_convert_type((bits[:, :w] >> 16) | (bits[:, w:] & jnp.uint32(0xFFFF0000)), I32)


def _unpack_pairs(p):
    u = lax.bitcast_convert_type(p, jnp.uint32)
    lo = lax.bitcast_convert_type(u << 16, F32)
    hi = lax.bitcast_convert_type(u & jnp.uint32(0xFFFF0000), F32)
    return jnp.concatenate([lo, hi], axis=1)


def _ffn_kernel(te_ref, nv_ref, x_ref, w1_ref, b1_ref, w2_ref, b2_ref, y_ref, w1b_ref, w2b_ref):
    i = pl.program_id(0)

    @pl.when(jnp.logical_or(i == 0, te_ref[i] != te_ref[jnp.maximum(i - 1, 0)]))
    def _():
        w1b_ref[...] = w1_ref[0].astype(BF16)
        w2b_ref[...] = w2_ref[0].astype(BF16)

    @pl.when(nv_ref[i] > 0)
    def _():
        dff = w2_ref.shape[1]
        row = lax.broadcasted_iota(I32, (x_ref.shape[0], 1), 0)
        x = _unpack_pairs(jnp.where(row < nv_ref[i], x_ref[...], 0)).astype(BF16)
        u = _dot(x, w1b_ref[...]) + b1_ref[0]
        glu = jnp.minimum(u[:, :dff], SWIGLU_LIMIT)
        lin = jnp.clip(u[:, dff:], -SWIGLU_LIMIT, SWIGLU_LIMIT)
        act = glu * jax.nn.sigmoid(SWIGLU_ALPHA * glu) * (lin + 1.0)
        y_ref[...] = _pack_pairs(_dot(act.astype(BF16), w2b_ref[...]) + b2_ref[0])

    @pl.when(nv_ref[i] <= 0)
    def _():
        y_ref[...] = jnp.zeros_like(y_ref)


def _ffn(xs, tile_expert, n_valid, layer, w1, b1, w2, b2):
    ns, wd = xs.shape
    depth, ne, d, n1 = w1.shape
    dff = w2.shape[2]
    row = lambda i, te, nv: (i, 0)
    exp4 = lambda i, te, nv: (layer, te[i], 0, 0)
    return pl.pallas_call(
        _ffn_kernel,
        grid_spec=pltpu.PrefetchScalarGridSpec(
            num_scalar_prefetch=2,
            grid=(ns // TMX,),
            in_specs=[pl.BlockSpec((TMX, wd), row),
                      pl.BlockSpec((None, 1, d, n1), exp4), pl.BlockSpec((None, 1, 1, n1), exp4),
                      pl.BlockSpec((None, 1, dff, d), exp4), pl.BlockSpec((None, 1, 1, d), exp4)],
            out_specs=pl.BlockSpec((TMX, wd), row),
            scratch_shapes=[pltpu.VMEM((d, n1), BF16), pltpu.VMEM((dff, d), BF16)]),
        out_shape=jax.ShapeDtypeStruct((ns, wd), I32),
        compiler_params=pltpu.CompilerParams(dimension_semantics=("arbitrary",), vmem_limit_bytes=FFN_VMEM_LIMIT),
        name="moe_ffn",
    )(tile_expert, n_valid, xs, w1, b1.reshape(depth, ne, 1, n1), w2, b2.reshape(depth, ne, 1, d))


def _combine_kernel(wt_ref, s_ref, g_ref, yk_ref, o_ref):
    wt = wt_ref[...]
    f = wt[:, 0:1] * _unpack_pairs(yk_ref[0])
    for kk in range(1, TOP_K):
        f = f + wt[:, kk:kk + 1] * _unpack_pairs(yk_ref[kk])
    o_ref[0] = s_ref[0] + g_ref[...] * f


def _combine(yk, wts, s, mod, which, skip):
    bsz, t, d = s.shape
    nt = t // TM - skip
    return pl.pallas_call(
        _combine_kernel,
        grid=(bsz, nt),
        in_specs=[pl.BlockSpec((TM, TOP_K), lambda b, i: (b * nt + i, 0)),
                  pl.BlockSpec((1, TM, d), lambda b, i: (b, i + skip, 0)),
                  _mod_spec_d(bsz, which, d, skip),
                  pl.BlockSpec((TOP_K, TM, yk.shape[-1]), lambda b, i: (0, b * nt + i, 0))],
        out_specs=pl.BlockSpec((1, TM, d), lambda b, i: (b, i, 0)),
        out_shape=jax.ShapeDtypeStruct((bsz, nt * TM, d), F32),
        input_output_aliases={} if skip else {1: 0},
        compiler_params=_params("arbitrary", "arbitrary"),
        name="moe_combine",
    )(wts.T, s, mod, yk)


def _moe(y, w_o, s, gain, mod, layer, w_router, b_router, w1, b1, w2, b2, skip=0):
    bsz, t, d = s.shape
    n = bsz * (t - skip * TM)
    assert n % SC_CHUNK == 0
    s, hp, ids, wts, ranks, counts = _moe_pre(y, w_o, s, gain, mod, w_router, b_router, skip)
    cnt = counts[:, 0].astype(I32)
    padded = ((cnt + TMX - 1) // TMX) * TMX
    ends = jnp.cumsum(padded)
    offs = ends - padded
    n_tiles = (TOP_K * n + N_EXPERTS * (TMX - 1) + TMX - 1) // TMX
    tile_start = jnp.arange(n_tiles, dtype=I32) * TMX
    te = jnp.sum(tile_start[:, None] >= ends[None, :], axis=1).astype(I32)
    te_last = jnp.max(jnp.where(cnt > 0, jnp.arange(N_EXPERTS, dtype=I32), 0))
    te = jnp.minimum(te, te_last)
    group_end = jnp.sum(jnp.where(te[:, None] == jnp.arange(N_EXPERTS, dtype=I32), offs + cnt, 0), axis=-1)
    n_valid = jnp.where(tile_start < ends[-1], jnp.clip(group_end - tile_start, 0, TMX), 0).astype(I32)
    slots = ranks + jnp.sum(jnp.where(ids[..., None] == jnp.arange(N_EXPERTS, dtype=I32), offs, 0), axis=-1)
    idx3 = slots.reshape(TOP_K, n // SC_CHUNK, SC_CHUNK).transpose(1, 0, 2)
    xs = _dispatch(hp.reshape(n, d // 2), idx3, n_tiles * TMX)
    ys = _ffn(xs, te, n_valid, layer, w1, b1, w2, b2)
    return _combine(_gather(ys, idx3), wts, s, mod, 5, skip)


def kernel(x, c, ctx, c_ctx, w_ada, b_ada, g_norm_mix, g_norm_ffn, na_w_qkv, na_q_gain, na_k_gain, na_rpb,
           na_w_o, ml_w_in, ml_b_gate, ml_mh_gain, ml_w_o, moe_w_router, moe_b_router, moe_w1, moe_b1,
           moe_w2, moe_b2):
    bsz, seq, d = x.shape
    n_ctx = ctx.shape[1]
    depth = w_ada.shape[0]
    rows = seq // GRID_W
    n_mod_rows = ((bsz + 1 + 7) // 8) * 8
    cc = jnp.concatenate([c, c_ctx[None, :], jnp.zeros((n_mod_rows - bsz - 1, d), F32)], axis=0)
    mod_all = _ada(cc, w_ada, b_ada).reshape(depth, n_mod_rows, 6, 1, d)
    s = jnp.concatenate([ctx, x], axis=1)
    for i in range(depth):
        j = i // 2
        mod = mod_all[i]
        if i % 2 == 0:
            qkv = _na_qkv(s, g_norm_mix[i], mod, na_w_qkv[j].astype(BF16))
            y = _na_attn(qkv, _na_bias(na_rpb[j], rows), na_q_gain[j], na_k_gain[j], n_ctx)
            w_o = na_w_o[j]
        else:
            q, kt, v, og, gr, gc, gm = _ml_proj(s, g_norm_mix[i], mod, ml_w_in[j], ml_b_gate[j], n_ctx)
            y = _mlstm(q, kt, v, og, gr, gc, gm, ml_mh_gain[j])
            w_o = ml_w_o[j]
        skip = n_ctx // TM if i == depth - 1 else 0
        s = _moe(y, w_o.astype(BF16), s, g_norm_ffn[i], mod, i, moe_w_router[i], moe_b_router[i],
                 moe_w1, moe_b1, moe_w2, moe_b2, skip)
    return s
```

```python
import functools
import math

import numpy as np
import jax
import jax.numpy as jnp
from jax import lax
from jax.experimental import pallas as pl
from jax.experimental.pallas import tpu as pltpu
from jax.experimental.pallas import tpu_sc as plsc

F32 = jnp.float32
BF16 = jnp.bfloat16
I32 = jnp.int32

EPS = 1e-6
GRID_W = 64
NA_HEADS = 16
NA_HEAD_DIM = 64
NA_KR = 8
NA_KC = 16
NA_QROWS = 4
NA_KROWS = NA_QROWS + NA_KR - 1
NA_NB = 4
ML_HEADS = 8
ML_DV = 128
ML_DQK = 64
ML_CHUNK = 256
ML_GATE_CAP = 15.0
ROPE_THETA = 10000.0
N_EXPERTS = 32
TOP_K = 4
SWIGLU_LIMIT = 7.0
SWIGLU_ALPHA = 1.702

TM = 256
TMX = 512
LANES = 128
NEG = -1e30
VMEM_LIMIT = 48 * 1024 * 1024
SC_CORES = 2
SC_SUBCORES = 16
SC_CHUNK = 64
FFN_VMEM_LIMIT =56 * 1024 * 1024


def _dot(a, b):
    return jnp.dot(a, b, preferred_element_type=F32)


def _dot_nt(a, b):
    return lax.dot_general(a, b, (((1,), (1,)), ((), ())), preferred_element_type=F32)


def _split3(x):
    hi = x.astype(BF16)
    r1 = x - hi.astype(F32)
    mid = r1.astype(BF16)
    lo = (r1 - mid.astype(F32)).astype(BF16)
    return hi, mid, lo


def _params(*sem):
    return pltpu.CompilerParams(dimension_semantics=sem, vmem_limit_bytes=VMEM_LIMIT)


def _norm_mod(x, gain, shift, scale):
    r = lax.rsqrt(jnp.mean(x * x, axis=-1, keepdims=True) + EPS)
    return (x * r * gain) * (1.0 + scale) + shift


def _mod_spec_d(n_batch, which, d, skip=0):
    return pl.BlockSpec((None, None, 1, d), lambda b, t: (jnp.where(t + skip == 0, n_batch, b), which, 0, 0))


def _ada_kernel(c_ref, w_ref, b_ref, o_ref):
    c = c_ref[...]
    s = c * jax.nn.sigmoid(c)
    o_ref[0] = _dot(s.astype(BF16), w_ref[0].astype(BF16)) + b_ref[0]


def _ada(cc, w_ada, b_ada):
    depth, d, n = w_ada.shape
    bn = 1536
    return pl.pallas_call(
        _ada_kernel,
        grid=(depth, n // bn),
        in_specs=[pl.BlockSpec((cc.shape[0], d), lambda l, j: (0, 0)),
                  pl.BlockSpec((1, d, bn), lambda l, j: (l, 0, j)),
                  pl.BlockSpec((1, 1, bn), lambda l, j: (l, 0, j))],
        out_specs=pl.BlockSpec((1, cc.shape[0], bn), lambda l, j: (l, 0, j)),
        out_shape=jax.ShapeDtypeStruct((depth, cc.shape[0], n), F32),
        compiler_params=_params("arbitrary", "arbitrary"),
        name="ada",
    )(cc, w_ada, b_ada.reshape(depth, 1, n))


def _na_qkv_kernel(x_ref, g_ref, sh_ref, sc_ref, w_ref, o_ref):
    h = _norm_mod(x_ref[0], g_ref[...], sh_ref[...], sc_ref[...])
    o_ref[0] = _dot(h.astype(BF16), w_ref[...]).astype(BF16)


def _na_qkv(s, gain, mod, w):
    bsz, t, d = s.shape
    n = w.shape[1]
    return pl.pallas_call(
        _na_qkv_kernel,
        grid=(bsz, t // TM),
        in_specs=[pl.BlockSpec((1, TM, d), lambda b, i: (b, i, 0)),
                  pl.BlockSpec((1, d), lambda b, i: (0, 0)),
                  _mod_spec_d(bsz, 0, d), _mod_spec_d(bsz, 1, d),
                  pl.BlockSpec((d, n), lambda b, i: (0, 0))],
        out_specs=pl.BlockSpec((1, TM, n), lambda b, i: (b, i, 0)),
        out_shape=jax.ShapeDtypeStruct((bsz, t, n), BF16),
        compiler_params=_params("arbitrary", "arbitrary"),
        name="na_qkv",
    )(s, gain.reshape(1, d), mod, mod, w)


def _na_bias(rpb, rows):
    nblk = rows // NA_QROWS
    kmax = rows - NA_KROWS
    nrr, ncc = 2 * NA_KR - 1, 2 * NA_KC - 1
    qc = np.arange(GRID_W)[:, None]
    kc = np.arange(GRID_W)[None, :]
    c0 = np.clip(qc - NA_KC // 2, 0, GRID_W - NA_KC)
    col_ok = (kc >= c0) & (kc < c0 + NA_KC)
    col_pick = np.eye(ncc)[np.clip(kc - qc + NA_KC - 1, 0, ncc - 1)]
    row_pick, valid = [], []
    for blk in (0, 1, nblk - 1):
        rs = blk * NA_QROWS
        ks = min(max(rs - NA_KR // 2, 0), kmax)
        qr = rs + np.arange(NA_QROWS)[:, None]
        kr = ks + np.arange(NA_KROWS)[None, :]
        r0 = np.clip(qr - NA_KR // 2, 0, rows - NA_KR)
        row_ok = (kr >= r0) & (kr < r0 + NA_KR)
        row_pick.append(np.eye(nrr)[np.clip(kr - qr + NA_KR - 1, 0, nrr - 1)])
        valid.append(row_ok[:, None, :, None] & col_ok[None, :, None, :])
    bias = jnp.einsum('vrka,pxab,qcb->pvxrqkc', jnp.asarray(np.stack(row_pick), F32),
                      rpb.reshape(NA_HEADS // 2, 2, nrr, ncc), jnp.asarray(col_pick, F32),
                      precision=lax.Precision.HIGHEST)
    bias = jnp.where(np.stack(valid)[None, :, None], bias, NEG)
    return bias.reshape(NA_HEADS // 2, 3, 2, NA_QROWS * GRID_W, NA_KROWS * GRID_W).astype(F32)


def _na_attn_kernel(q_ref, k_ref, v_ref, bias_ref, qg_ref, kg_ref, o_ref, kn_ref, v1_ref, *, n_ctx, rows):
    t = pl.program_id(2)
    first = lax.broadcasted_iota(I32, (1, LANES), 1) < NA_HEAD_DIM
    nk = NA_KROWS * GRID_W

    def head_rms(z, gain):
        z2 = z * z
        sa = jnp.sum(jnp.where(first, z2, 0.0), axis=-1, keepdims=True)
        sb = jnp.sum(jnp.where(first, 0.0, z2), axis=-1, keepdims=True)
        r = jnp.where(first, lax.rsqrt(sa / NA_HEAD_DIM + EPS), lax.rsqrt(sb / NA_HEAD_DIM + EPS))
        return z * r * gain

    nb = q_ref.shape[0]

    @pl.when(t == 0)
    def _():
        def body(i, carry):
            off = pl.multiple_of(i * TM, TM)
            for bb in range(nb):
                kn_ref[bb, pl.ds(off, TM), :] = head_rms(k_ref[bb, pl.ds(off, TM), :].astype(F32),
                                                         kg_ref[...]).astype(BF16)
                v = v_ref[bb, pl.ds(off, TM), :]
                v1_ref[bb, pl.ds(off, TM), :] = jnp.concatenate([v, jnp.ones_like(v)], axis=1)
            return carry
        lax.fori_loop(0, k_ref.shape[1] // TM, body, 0)

    def stacked_q(bb):
        qn = (head_rms(q_ref[bb].astype(F32), qg_ref[...]) * (NA_HEAD_DIM ** -0.5)).astype(BF16)
        zero = jnp.zeros_like(qn)
        return jnp.concatenate([jnp.where(first, qn, zero), jnp.where(first, zero, qn)], axis=0)

    def finish(bb, o):
        oa, ob = o[:TM], o[TM:]
        o_ref[bb] = jnp.where(first, oa[:, :LANES] / oa[:, LANES:],
                              ob[:, :LANES] / ob[:, LANES:]).astype(o_ref.dtype)

    @pl.when(t == 0)
    def _():
        for bb in range(nb):
            s = _dot_nt(stacked_q(bb), kn_ref[bb, 0:n_ctx, :])
            m = jnp.max(s, axis=-1, keepdims=True)
            finish(bb, _dot(jnp.exp((s - m).astype(BF16)), v1_ref[bb, 0:n_ctx, :]))

    @pl.when(t > 0)
    def _():
        blk = t - 1
        nblk = rows // NA_QROWS
        ks = jnp.clip(blk * NA_QROWS - NA_KR // 2, 0, rows - NA_KROWS)
        start = pl.multiple_of(n_ctx + ks * GRID_W, GRID_W)
        kind = jnp.where(blk == 0, 0, jnp.where(blk == nblk - 1, 2, 1))
        bias = bias_ref[0, kind].reshape(2 * TM, nk)
        for bb in range(nb):
            q2 = stacked_q(bb)
            sw = _dot_nt(q2, kn_ref[bb, pl.ds(start, nk), :]) + bias
            sc = _dot_nt(q2, kn_ref[bb, 0:n_ctx, :])
            m = jnp.maximum(jnp.max(sw, axis=-1, keepdims=True), jnp.max(sc, axis=-1, keepdims=True))
            pw = jnp.exp((sw - m).astype(BF16))
            pc = jnp.exp((sc - m).astype(BF16))
            finish(bb, _dot(pw, v1_ref[bb, pl.ds(start, nk), :]) + _dot(pc, v1_ref[bb, 0:n_ctx, :]))


def _na_attn(qkv, bias, q_gain, k_gain, n_ctx):
    bsz, t, _ = qkv.shape
    d = NA_HEADS * NA_HEAD_DIM
    rows = (t - n_ctx) // GRID_W
    assert n_ctx == TM and NA_QROWS * GRID_W == TM and rows % NA_QROWS == 0 and rows >= NA_KROWS
    hp = NA_HEADS // 2
    nq, nk = bias.shape[-2:]
    gq = jnp.tile(q_gain, 2).reshape(1, LANES)
    gk = jnp.tile(k_gain, 2).reshape(1, LANES)
    nb = NA_NB if bsz % NA_NB == 0 else 1
    return pl.pallas_call(
        functools.partial(_na_attn_kernel, n_ctx=n_ctx, rows=rows),
        grid=(hp, bsz // nb, t // TM),
        in_specs=[pl.BlockSpec((nb, TM, LANES), lambda h, b, i: (b, i, h)),
                  pl.BlockSpec((nb, t, LANES), lambda h, b, i: (b, 0, hp + h)),
                  pl.BlockSpec((nb, t, LANES), lambda h, b, i: (b, 0, 2 * hp + h)),
                  pl.BlockSpec((1, 3, 2, nq, nk), lambda h, b, i: (h, 0, 0, 0, 0)),
                  pl.BlockSpec((1, LANES), lambda h, b, i: (0, 0)),
                  pl.BlockSpec((1, LANES), lambda h, b, i: (0, 0))],
        out_specs=pl.BlockSpec((nb, TM, LANES), lambda h, b, i: (b, i, h)),
        out_shape=jax.ShapeDtypeStruct((bsz, t, d), BF16),
        scratch_shapes=[pltpu.VMEM((nb, t, LANES), BF16), pltpu.VMEM((nb, t, 2 * LANES), BF16)],
        compiler_params=_params("arbitrary", "arbitrary", "arbitrary"),
        name="na_attn",
    )(qkv, qkv, qkv, bias, gq, gk)


def _rope_tables(n_ctx, seq):
    lane = np.arange(LANES)
    dd = lane % ML_DQK
    grp = dd // (ML_DQK // 2)
    j = dd % (ML_DQK // 2)
    n = ML_DQK // 4
    inv = ROPE_THETA ** (-(j % n).astype(np.float64) / n)
    pos = np.arange(seq)
    p = np.where(grp[None, :] == 0, (pos // GRID_W)[:, None], (pos % GRID_W)[:, None]).astype(np.float64)
    ang = p * inv[None, :]
    cos = np.cos(ang)
    sin = np.where(j[None, :] < n, -np.sin(ang), np.sin(ang))
    cos = np.concatenate([np.ones((n_ctx, LANES)), cos], axis=0)
    sin = np.concatenate([np.zeros((n_ctx, LANES)), sin], axis=0)
    return jnp.asarray(cos, F32), jnp.asarray(sin, F32)


def _ml_proj_kernel(x_ref, g_ref, sh_ref, sc_ref, wm_ref, wg_ref, wgt_ref, bgr_ref, bgc_ref, cos_ref, sin_ref,
                    q_ref, kt_ref, v_ref, og_ref, gr_ref, gc_ref, gm_ref):
    h = _norm_mod(x_ref[0], g_ref[...], sh_ref[...], sc_ref[...]).astype(BF16)
    main = _dot(h, wm_ref[...])
    nq = q_ref.shape[-1]
    nv = v_ref.shape[-1]
    cos = cos_ref[...]
    sin = sin_ref[...]
    n = ML_DQK // 4
    lane = lax.broadcasted_iota(I32, (1, LANES), 1)
    low = (lane % (2 * n)) < n

    def rope(z):
        zr = jnp.where(low, pltpu.roll(z, LANES - n, 1), pltpu.roll(z, n, 1))
        return z * cos + zr * sin

    for j in range(nq // LANES):
        sl = slice(j * LANES, (j + 1) * LANES)
        q_ref[0, :, sl] = (rope(main[:, sl]) * (ML_DQK ** -0.5)).astype(BF16)
        kt_ref[0, sl, :] = rope(main[:, nq + j * LANES: nq + (j + 1) * LANES]).T.astype(BF16)
    v_ref[0] = main[:, 2 * nq: 2 * nq + nv].astype(BF16)
    og_ref[0] = main[:, 2 * nq + nv:].astype(BF16)

    nh = ML_HEADS
    ti = lax.broadcasted_iota(I32, (TM, TM), 0)
    si = lax.broadcasted_iota(I32, (TM, TM), 1)
    le = (si <= ti).astype(BF16)
    ge = (si >= ti).astype(BF16)

    def capped(pre):
        return ML_GATE_CAP * jnp.tanh(pre / ML_GATE_CAP)

    def exact_cum(tri, x, nt):
        parts = _split3(x)
        if nt:
            return sum(_dot_nt(p, tri) for p in parts)
        return sum(_dot(tri, p) for p in parts)

    gc = capped(_dot(h, wg_ref[...]) + bgr_ref[...])
    lf = jax.nn.log_sigmoid(gc)
    pre = exact_cum(le, lf, False)
    suf = exact_cum(ge, lf, False)
    cum = jnp.where(lane < 2 * nh, pre, suf)
    r = gc - pltpu.roll(cum, LANES - nh, 1)
    is_i = (lane // nh) % 2 == 0
    gc_ref[0] = jnp.where(is_i, r, cum)[:, :4 * nh]
    row = lax.broadcasted_iota(I32, (TM, 1), 0)
    pmax = r
    smax = r
    step = 1
    while step < TM:
        pmax = jnp.maximum(pmax, jnp.where(row >= step, pltpu.roll(pmax, step, 0), NEG))
        smax = jnp.maximum(smax, jnp.where(row < TM - step, pltpu.roll(smax, TM - step, 0), NEG))
        step *= 2
    gm_ref[0] = jnp.where(lane < 2 * nh, pmax, smax)[:, :4 * nh]

    gr = capped(_dot_nt(wgt_ref[...], h) + bgc_ref[...])
    lfr = jax.nn.log_sigmoid(gr)
    pre_r = exact_cum(le, lfr, True)
    suf_r = exact_cum(ge, lfr, True)
    gr_ref[0, 0 * nh:1 * nh, :] = gr[0 * nh:1 * nh] - pre_r[1 * nh:2 * nh]
    gr_ref[0, 1 * nh:2 * nh, :] = pre_r[1 * nh:2 * nh]
    gr_ref[0, 2 * nh:3 * nh, :] = gr[2 * nh:3 * nh] - suf_r[3 * nh:4 * nh]
    gr_ref[0, 3 * nh:4 * nh, :] = suf_r[3 * nh:4 * nh]


def _ml_proj(s, gain, mod, w_in, b_gate, n_ctx):
    bsz, t, d = s.shape
    nq = ML_HEADS * ML_DQK
    nv = ML_HEADS * ML_DV
    ng = 4 * ML_HEADS
    assert TM == ML_CHUNK
    wm = w_in[:, :2 * nq + nv + d].astype(BF16)
    wg = w_in[:, 2 * nq + nv + d:]
    wg_pad = jnp.pad(wg, ((0, 0), (0, LANES - ng))).astype(BF16)
    wgt = wg.T.astype(BF16)
    bgr = jnp.pad(b_gate, (0, LANES - ng)).reshape(1, LANES)
    bgc = b_gate.reshape(ng, 1)
    cos, sin = _rope_tables(n_ctx, t - n_ctx)
    tile = lambda n: pl.BlockSpec((1, TM, n), lambda b, i: (b, i, 0))
    const = lambda a: pl.BlockSpec(a.shape, lambda b, i: (0, 0))
    return pl.pallas_call(
        _ml_proj_kernel,
        grid=(bsz, t // TM),
        in_specs=[tile(d), pl.BlockSpec((1, d), lambda b, i: (0, 0)),
                  _mod_spec_d(bsz, 0, d), _mod_spec_d(bsz, 1, d),
                  const(wm), const(wg_pad), const(wgt), const(bgr), const(bgc),
                  pl.BlockSpec((TM, LANES), lambda b, i: (i, 0)),
                  pl.BlockSpec((TM, LANES), lambda b, i: (i, 0))],
        out_specs=[tile(nq), pl.BlockSpec((1, nq, TM), lambda b, i: (b, 0, i)), tile(nv), tile(d),
                   pl.BlockSpec((1, ng, TM), lambda b, i: (b, 0, i)),
                   tile(ng), tile(ng)],
        out_shape=[jax.ShapeDtypeStruct((bsz, t, nq), BF16), jax.ShapeDtypeStruct((bsz, nq, t), BF16),
                   jax.ShapeDtypeStruct((bsz, t, nv), BF16), jax.ShapeDtypeStruct((bsz, t, d), BF16),
                   jax.ShapeDtypeStruct((bsz, ng, t), F32), jax.ShapeDtypeStruct((bsz, t, ng), F32),
                   jax.ShapeDtypeStruct((bsz, t, ng), F32)],
        compiler_params=_params("arbitrary", "arbitrary"),
        name="ml_proj",
    )(s, gain.reshape(1, d), mod, mod, wm, wg_pad, wgt, bgr, bgc, cos, sin)


def _mlstm_kernel(q_ref, kt_ref, v_ref, og_ref, gr_ref, gc_ref, gm_ref, mg_ref, o_ref, hf_ref, hb_ref, c_ref):
    L = ML_CHUNK
    nchunk = q_ref.shape[1] // L
    first = lax.broadcasted_iota(I32, (1, LANES), 1) < ML_DQK
    first_row = lax.broadcasted_iota(I32, (LANES, 1), 0) < ML_DQK
    ti = lax.broadcasted_iota(I32, (L, L), 0)
    si = lax.broadcasted_iota(I32, (L, L), 1)
    masks = (si <= ti, si >= ti)
    ones_blk = jnp.ones((L, ML_DV), BF16)

    c_ref[...] = jnp.zeros_like(c_ref)

    def chunk(c, a, bw, m0):
        off = pl.multiple_of(c * L, L)
        q = q_ref[0, pl.ds(off, L), :]
        kt = kt_ref[0, :, pl.ds(off, L)]
        qa = jnp.where(first if a == 0 else jnp.logical_not(first), q, jnp.zeros_like(q))
        kta = jnp.where(first_row if a == 0 else jnp.logical_not(first_row), kt, jnp.zeros_like(kt))
        v = v_ref[0, pl.ds(off, L), a * ML_DV:(a + 1) * ML_DV]
        w = 2 if bw else 0
        r_row = gr_ref[0, 0, 2 * w + a: 2 * w + a + 1, pl.ds(off, L)]
        cum_row = gr_ref[0, 0, 2 * (w + 1) + a: 2 * (w + 1) + a + 1, pl.ds(off, L)]
        cum_col = gc_ref[0, 0, pl.ds(off, L), 2 * (w + 1) + a: 2 * (w + 1) + a + 1]
        tot = cum_row[:, 0:1] if bw else cum_row[:, L - 1:L]
        mask = masks[1 if bw else 0]
        mx = gm_ref[0, 0, pl.ds(off, L), w + a: w + a + 1]
        mm = jnp.broadcast_to(jnp.maximum(m0, mx), (L, LANES))
        cum_b = jnp.broadcast_to(cum_col, (L, LANES))
        e = jnp.exp(jnp.where(mask, r_row - jnp.concatenate([mm] * (L // LANES), axis=1), NEG))
        s = (_dot(qa, kt) * e).astype(BF16)
        v2 = jnp.concatenate([v, ones_blk], axis=1)
        idx = 2 * a + (1 if bw else 0)
        cst = c_ref[idx]
        w_inter = jnp.exp(m0 - mm)
        both = _dot(s, v2) + jnp.concatenate([w_inter, w_inter], axis=1) * _dot(qa, cst.astype(BF16))
        num = both[:, :ML_DV]
        den = both[:, ML_DV:]
        hout = num / jnp.maximum(jnp.abs(den), jnp.exp(-cum_b - mm))
        dst = hb_ref if bw else hf_ref
        dst[pl.ds(off, L), a * ML_DV:(a + 1) * ML_DV] = hout
        m_loc = tot + jnp.max(r_row, axis=1, keepdims=True)
        kte = (kta.astype(F32) * jnp.exp(tot + r_row - m_loc)).astype(BF16)
        u = _dot(kte, v2)
        m_new = jnp.maximum(tot + m0, m_loc)
        c_ref[idx] = jnp.exp(tot + m0 - m_new) * cst + jnp.exp(m_loc - m_new) * u
        return m_new

    def body(i, ms):
        cb = jnp.where(i == 0, 0, nchunk - i)
        return (chunk(i, 0, False, ms[0]), chunk(i, 1, False, ms[1]),
                chunk(cb, 0, True, ms[2]), chunk(cb, 1, True, ms[3]))

    z = jnp.zeros((1, 1), F32)
    lax.fori_loop(0, nchunk, body, (z, z, z, z))

    def epilogue(i, carry):
        off = pl.multiple_of(i * L, L)
        for a in range(2):
            sl = slice(a * ML_DV, (a + 1) * ML_DV)
            hs = hf_ref[pl.ds(off, L), sl] + hb_ref[pl.ds(off, L), sl]
            hn = hs * lax.rsqrt(jnp.mean(hs * hs, axis=-1, keepdims=True) + EPS) * mg_ref[:, sl]
            o_ref[0, pl.ds(off, L), sl] = (hn * jax.nn.sigmoid(og_ref[0, pl.ds(off, L), sl].astype(F32))).astype(BF16)
        return carry

    lax.fori_loop(0, nchunk, epilogue, 0)


def _mlstm(q, kt, v, og, gr, gc, gm, mh_gain):
    bsz, t, _ = q.shape
    hp = ML_HEADS // 2
    nv = ML_HEADS * ML_DV
    w2 = 2 * ML_DV
    gr = gr.reshape(bsz, 4, hp, 2, t).transpose(0, 2, 1, 3, 4).reshape(bsz, hp, 8, t)
    gc = gc.reshape(bsz, t, 4, hp, 2).transpose(0, 3, 1, 2, 4).reshape(bsz, hp, t, 8)
    gm = gm.reshape(bsz, t, 4, hp, 2)[:, :, 0::2].transpose(0, 3, 1, 2, 4).reshape(bsz, hp, t, 4)
    return pl.pallas_call(
        _mlstm_kernel,
        grid=(bsz, hp),
        in_specs=[pl.BlockSpec((1, t, LANES), lambda b, h: (b, 0, h)),
                  pl.BlockSpec((1, LANES, t), lambda b, h: (b, h, 0)),
                  pl.BlockSpec((1, t, w2), lambda b, h: (b, 0, h)),
                  pl.BlockSpec((1, t, w2), lambda b, h: (b, 0, h)),
                  pl.BlockSpec((1, 1, 8, t), lambda b, h: (b, h, 0, 0)),
                  pl.BlockSpec((1, 1, t, 8), lambda b, h: (b, h, 0, 0)),
                  pl.BlockSpec((1, 1, t, 4), lambda b, h: (b, h, 0, 0)),
                  pl.BlockSpec((1, w2), lambda b, h: (0, h))],
        out_specs=pl.BlockSpec((1, t, w2), lambda b, h: (b, 0, h)),
        out_shape=jax.ShapeDtypeStruct((bsz, t, nv), BF16),
        scratch_shapes=[pltpu.VMEM((t, w2), F32), pltpu.VMEM((t, w2), F32),
                        pltpu.VMEM((4, LANES, w2), F32)],
        compiler_params=_params("arbitrary", "arbitrary"),
        name="mlstm",
    )(q, kt, v, og, gr, gc, gm, mh_gain.reshape(1, nv))


def _moe_pre_kernel(y_ref, wo_ref, x_ref, g1_ref, g_ref, sh_ref, sc_ref, whi_ref, wlo_ref, br_ref,
                    s_ref, h_ref, id_ref, wt_ref, rk_ref, cnt_ref):
    step = pl.program_id(0) * pl.num_programs(1) + pl.program_id(1)

    @pl.when(step == 0)
    def _():
        cnt_ref[...] = jnp.zeros_like(cnt_ref)

    x = x_ref[0] + g1_ref[...] * _dot(y_ref[0], wo_ref[...])
    s_ref[0] = x
    h = _norm_mod(x, g_ref[...], sh_ref[...], sc_ref[...])
    h_ref[0] = _pack_pairs(h)
    hhi = h.astype(BF16)
    hlo = (h - hhi.astype(F32)).astype(BF16)
    whi = whi_ref[...]
    logits = _dot_nt(whi, hhi) + _dot_nt(whi, hlo) + _dot_nt(wlo_ref[...], hhi) + br_ref[...]
    ne = logits.shape[0]
    eidx = lax.broadcasted_iota(I32, logits.shape, 0)
    sels, vals, ids = [], [], []
    l = logits
    for _ in range(TOP_K):
        mval = jnp.max(l, axis=0, keepdims=True)
        idx = jnp.min(jnp.where(l == mval, eidx, ne), axis=0, keepdims=True)
        sel = eidx == idx
        l = jnp.where(sel, -jnp.inf, l)
        sels.append(sel)
        vals.append(mval)
        ids.append(idx)
    ex = [jnp.exp(v - vals[0]) for v in vals]
    tot = ex[0] + ex[1] + ex[2] + ex[3]
    member = sels[0] | sels[1] | sels[2] | sels[3]
    ti = lax.broadcasted_iota(I32, (TM, TM), 0)
    si = lax.broadcasted_iota(I32, (TM, TM), 1)
    before = (ti < si).astype(BF16)
    rank_e = _dot(member.astype(BF16), before) + cnt_ref[:, 0:1]
    for kk in range(TOP_K):
        id_ref[kk:kk + 1, :] = ids[kk]
        wt_ref[kk:kk + 1, :] = ex[kk] / tot
        rk_ref[kk:kk + 1, :] = jnp.sum(jnp.where(sels[kk], rank_e, 0.0), axis=0, keepdims=True).astype(I32)
    cnt_ref[...] = cnt_ref[...] + jnp.sum(member.astype(F32), axis=1, keepdims=True)


def _moe_pre(y, w_o, s, gain, mod, w_router, b_router, skip):
    bsz, t, d = s.shape
    nt = t // TM - skip
    n = bsz * nt * TM
    wt = w_router.T
    whi = wt.astype(BF16)
    wlo = (wt - whi.astype(F32)).astype(BF16)
    tok = lambda dt: jax.ShapeDtypeStruct((TOP_K, n), dt)
    tok_spec = pl.BlockSpec((TOP_K, TM), lambda b, i: (0, b * nt + i))
    tile = lambda w: pl.BlockSpec((1, TM, w), lambda b, i: (b, i + skip, 0))
    const = lambda a: pl.BlockSpec(a.shape, lambda b, i: (0, 0))
    return pl.pallas_call(
        _moe_pre_kernel,
        grid=(bsz, nt),
        in_specs=[tile(y.shape[-1]), const(w_o), tile(d), _mod_spec_d(bsz, 2, d, skip),
                  pl.BlockSpec((1, d), lambda b, i: (0, 0)),
                  _mod_spec_d(bsz, 3, d, skip), _mod_spec_d(bsz, 4, d, skip),
                  const(whi), const(wlo), pl.BlockSpec((N_EXPERTS, 1), lambda b, i: (0, 0))],
        out_specs=[tile(d), pl.BlockSpec((1, TM, d // 2), lambda b, i: (b, i, 0)), tok_spec, tok_spec, tok_spec,
                   pl.BlockSpec((N_EXPERTS, LANES), lambda b, i: (0, 0))],
        out_shape=[jax.ShapeDtypeStruct(s.shape, F32),
                   jax.ShapeDtypeStruct((bsz, nt * TM, d // 2), I32), tok(I32), tok(F32), tok(I32),
                   jax.ShapeDtypeStruct((N_EXPERTS, LANES), F32)],
        input_output_aliases={2: 0},
        compiler_params=_params("arbitrary", "arbitrary"),
        name="moe_pre",
    )(y, w_o, s, mod, gain.reshape(1, d), mod, mod, whi, wlo, b_router.reshape(N_EXPERTS, 1))


def _sc_mesh():
    return plsc.VectorSubcoreMesh(core_axis_name="c", subcore_axis_name="s",
                                  num_cores=SC_CORES, num_subcores=SC_SUBCORES)


def _sc_worker_chunks(nchunks, body):
    n_workers = SC_CORES * SC_SUBCORES
    per_worker = -(-nchunks // n_workers)
    wid = lax.axis_index("s") * SC_CORES + lax.axis_index("c")

    @pl.loop(0, per_worker)
    def _(i):
        c = wid * per_worker + i

        @pl.when(c < nchunks)
        def _():
            body(c)


def _dispatch(hp, idx3, n_slots):
    n, w = hp.shape
    nchunks, kk, ch = idx3.shape

    @functools.partial(pl.kernel, mesh=_sc_mesh(), out_type=jax.ShapeDtypeStruct((n_slots, w), hp.dtype),
                       scratch_types=[pltpu.VMEM((kk, ch), I32), pltpu.VMEM((ch, w), hp.dtype)],
                       name="moe_dispatch")
    def scatter(src_hbm, idx_hbm, out_hbm, idx_v, rows_v):
        def body(c):
            pltpu.sync_copy(idx_hbm.at[c], idx_v)
            pltpu.sync_copy(src_hbm.at[pl.ds(c * ch, ch)], rows_v)
            for j in range(kk):
                pltpu.sync_copy(rows_v, out_hbm.at[idx_v.at[j]])
        _sc_worker_chunks(nchunks, body)

    return scatter(hp, idx3)


def _gather(ys, idx3):
    nchunks, kk, ch = idx3.shape
    w = ys.shape[1]

    @functools.partial(pl.kernel, mesh=_sc_mesh(), out_type=jax.ShapeDtypeStruct((kk, nchunks * ch, w), ys.dtype),
                       scratch_types=[pltpu.VMEM((kk, ch), I32), pltpu.VMEM((ch, w), ys.dtype)],
                       name="moe_gather")
    def gather(tab_hbm, idx_hbm, out_hbm, idx_v, rows_v):
        def body(c):
            pltpu.sync_copy(idx_hbm.at[c], idx_v)
            for j in range(kk):
                pltpu.sync_copy(tab_hbm.at[idx_v.at[j]], rows_v)
                pltpu.sync_copy(rows_v, out_hbm.at[j, pl.ds(c * ch, ch)])
        _sc_worker_chunks(nchunks, body)

    return gather(ys, idx3)


def _pack_pairs(x):
    w = x.shape[1] // 2
    bits = lax.bitcast_convert_type(x.astype(BF16).astype(F32), jnp.uint32)
    return lax.bitcast_convert_type((bits[:, :w] >> 16) | (bits[:, w:] & jnp.uint32(0xFFFF0000)), I32)


def _unpack_pairs(p):
    u = lax.bitcast_convert_type(p, jnp.uint32)
    lo = lax.bitcast_convert_type(u << 16, F32)
    hi = lax.bitcast_convert_type(u & jnp.uint32(0xFFFF0000), F32)
    return jnp.concatenate([lo, hi], axis=1)


def _ffn_kernel(te_ref, nv_ref, x_ref, w1_ref, b1_ref, w2_ref, b2_ref, y_ref, w1b_ref, w2b_ref):
    i = pl.program_id(0)

    @pl.when(jnp.logical_or(i == 0, te_ref[i] != te_ref[jnp.maximum(i - 1, 0)]))
    def _():
        w1b_ref[...] = w1_ref[0].astype(BF16)
        w2b_ref[...] = w2_ref[0].astype(BF16)

    nv = nv_ref[i]
    half = x_ref.shape[0] // 2

    def expert(rows):
        dff = w2_ref.shape[1]
        row = lax.broadcasted_iota(I32, (rows, 1), 0)
        x = _unpack_pairs(jnp.where(row < nv, x_ref[0:rows, :], 0)).astype(BF16)
        u = _dot(x, w1b_ref[...]) + b1_ref[0]
        glu = jnp.minimum(u[:, :dff], SWIGLU_LIMIT)
        lin = jnp.clip(u[:, dff:], -SWIGLU_LIMIT, SWIGLU_LIMIT)
        act = glu * jax.nn.sigmoid(SWIGLU_ALPHA * glu) * (lin + 1.0)
        y_ref[0:rows, :] = _pack_pairs(_dot(act.astype(BF16), w2b_ref[...]) + b2_ref[0])

    @pl.when(nv > half)
    def _():
        expert(2 * half)

    @pl.when(jnp.logical_and(nv > 0, nv <= half))
    def _():
        expert(half)
        y_ref[half:, :] = jnp.zeros((half, y_ref.shape[1]), y_ref.dtype)

    @pl.when(nv <= 0)
    def _():
        y_ref[...] = jnp.zeros_like(y_ref)


def _ffn(xs, tile_expert, n_valid, layer, w1, b1, w2, b2):
    ns, wd = xs.shape
    depth, ne, d, n1 = w1.shape
    dff = w2.shape[2]
    row = lambda i, te, nv: (i, 0)
    exp4 = lambda i, te, nv: (layer, te[i], 0, 0)
    return pl.pallas_call(
        _ffn_kernel,
        grid_spec=pltpu.PrefetchScalarGridSpec(
            num_scalar_prefetch=2,
            grid=(ns // TMX,),
            in_specs=[pl.BlockSpec((TMX, wd), row),
                      pl.BlockSpec((None, 1, d, n1), exp4), pl.BlockSpec((None, 1, 1, n1), exp4),
                      pl.BlockSpec((None, 1, dff, d), exp4), pl.BlockSpec((None, 1, 1, d), exp4)],
            out_specs=pl.BlockSpec((TMX, wd), row),
            scratch_shapes=[pltpu.VMEM((d, n1), BF16), pltpu.VMEM((dff, d), BF16)]),
        out_shape=jax.ShapeDtypeStruct((ns, wd), I32),
        compiler_params=pltpu.CompilerParams(dimension_semantics=("arbitrary",), vmem_limit_bytes=FFN_VMEM_LIMIT),
        name="moe_ffn",
    )(tile_expert, n_valid, xs, w1, b1.reshape(depth, ne, 1, n1), w2, b2.reshape(depth, ne, 1, d))


def _combine_kernel(wt_ref, s_ref, g_ref, yk_ref, o_ref):
    wt = wt_ref[...]
    f = wt[:, 0:1] * _unpack_pairs(yk_ref[0])
    for kk in range(1, TOP_K):
        f = f + wt[:, kk:kk + 1] * _unpack_pairs(yk_ref[kk])
    o_ref[0] = s_ref[0] + g_ref[...] * f


def _combine(yk, wts, s, mod, which, skip):
    bsz, t, d = s.shape
    nt = t // TM - skip
    return pl.pallas_call(
        _combine_kernel,
        grid=(bsz, nt),
        in_specs=[pl.BlockSpec((TM, TOP_K), lambda b, i: (b * nt + i, 0)),
                  pl.BlockSpec((1, TM, d), lambda b, i: (b, i + skip, 0)),
                  _mod_spec_d(bsz, which, d, skip),
                  pl.BlockSpec((TOP_K, TM, yk.shape[-1]), lambda b, i: (0, b * nt + i, 0))],
        out_specs=pl.BlockSpec((1, TM, d), lambda b, i: (b, i, 0)),
        out_shape=jax.ShapeDtypeStruct((bsz, nt * TM, d), F32),
        input_output_aliases={} if skip else {1: 0},
        compiler_params=_params("arbitrary", "arbitrary"),
        name="moe_combine",
    )(wts.T, s, mod, yk)


def _moe(y, w_o, s, gain, mod, layer, w_router, b_router, w1, b1, w2, b2, skip=0):
    bsz, t, d = s.shape
    n = bsz * (t - skip * TM)
    assert n % SC_CHUNK == 0
    s, hp, ids, wts, ranks, counts = _moe_pre(y, w_o, s, gain, mod, w_router, b_router, skip)
    cnt = counts[:, 0].astype(I32)
    padded = ((cnt + TMX - 1) // TMX) * TMX
    ends = jnp.cumsum(padded)
    offs = ends - padded
    n_tiles = (TOP_K * n + N_EXPERTS * (TMX - 1) + TMX - 1) // TMX
    tile_start = jnp.arange(n_tiles, dtype=I32) * TMX
    te = jnp.sum(tile_start[:, None] >= ends[None, :], axis=1).astype(I32)
    te_last = jnp.max(jnp.where(cnt > 0, jnp.arange(N_EXPERTS, dtype=I32), 0))
    te = jnp.minimum(te, te_last)
    group_end = jnp.sum(jnp.where(te[:, None] == jnp.arange(N_EXPERTS, dtype=I32), offs + cnt, 0), axis=-1)
    n_valid = jnp.where(tile_start < ends[-1], jnp.clip(group_end - tile_start, 0, TMX), 0).astype(I32)
    slots = ranks + jnp.sum(jnp.where(ids[..., None] == jnp.arange(N_EXPERTS, dtype=I32), offs, 0), axis=-1)
    idx3 = slots.reshape(TOP_K, n // SC_CHUNK, SC_CHUNK).transpose(1, 0, 2)
    xs = _dispatch(hp.reshape(n, d // 2), idx3, n_tiles * TMX)
    ys = _ffn(xs, te, n_valid, layer, w1, b1, w2, b2)
    return _combine(_gather(ys, idx3), wts, s, mod, 5, skip)


def kernel(x, c, ctx, c_ctx, w_ada, b_ada, g_norm_mix, g_norm_ffn, na_w_qkv, na_q_gain, na_k_gain, na_rpb,
           na_w_o, ml_w_in, ml_b_gate, ml_mh_gain, ml_w_o, moe_w_router, moe_b_router, moe_w1, moe_b1,
           moe_w2, moe_b2):
    bsz, seq, d = x.shape
    n_ctx = ctx.shape[1]
    depth = w_ada.shape[0]
    rows = seq // GRID_W
    n_mod_rows = ((bsz + 1 + 7) // 8) * 8
    cc = jnp.concatenate([c, c_ctx[None, :], jnp.zeros((n_mod_rows - bsz - 1, d), F32)], axis=0)
    mod_all = _ada(cc, w_ada, b_ada).reshape(depth, n_mod_rows, 6, 1, d)
    s = jnp.concatenate([ctx, x], axis=1)
    for i in range(depth):
        j = i // 2
        mod = mod_all[i]
        if i % 2 == 0:
            qkv = _na_qkv(s, g_norm_mix[i], mod, na_w_qkv[j].astype(BF16))
            y = _na_attn(qkv, _na_bias(na_rpb[j], rows), na_q_gain[j], na_k_gain[j], n_ctx)
            w_o = na_w_o[j]
        else:
            q, kt, v, og, gr, gc, gm = _ml_proj(s, g_norm_mix[i], mod, ml_w_in[j], ml_b_gate[j], n_ctx)
            y = _mlstm(q, kt, v, og, gr, gc, gm, ml_mh_gain[j])
            w_o = ml_w_o[j]
        skip = n_ctx // TM if i == depth - 1 else 0
        s = _moe(y, w_o.astype(BF16), s, g_norm_ffn[i], mod, i, moe_w_router[i], moe_b_router[i],
                 moe_w1, moe_b1, moe_w2, moe_b2, skip)
    return s
```

```python
import functools
import math

import numpy as np
import jax
import jax.numpy as jnp
from jax import lax
from jax.experimental import pallas as pl
from jax.experimental.pallas import tpu as pltpu
from jax.experimental.pallas import tpu_sc as plsc

F32 = jnp.float32
BF16 = jnp.bfloat16
I32 = jnp.int32

EPS = 1e-6
GRID_W = 64
NA_HEADS = 16
NA_HEAD_DIM = 64
NA_KR = 8
NA_KC = 16
NA_QROWS = 4
NA_KROWS = NA_QROWS + NA_KR - 1
NA_NB = 4
PRE_NB = 2
ML_HEADS = 8
ML_DV = 128
ML_DQK = 64
ML_CHUNK = 256
ML_GATE_CAP = 15.0
ROPE_THETA = 10000.0
N_EXPERTS = 32
TOP_K = 4
SWIGLU_LIMIT = 7.0
SWIGLU_ALPHA = 1.702

TM = 256
TMX = 512
LANES = 128
NEG = -1e30
VMEM_LIMIT = 48 * 1024 * 1024
SC_CORES = 2
SC_SUBCORES = 16
SC_CHUNK = 128
FFN_VMEM_LIMIT =56 * 1024 * 1024


def _dot(a, b):
    return jnp.dot(a, b, preferred_element_type=F32)


def _dot_nt(a, b):
    return lax.dot_general(a, b, (((1,), (1,)), ((), ())), preferred_element_type=F32)


def _split3(x):
    hi = x.astype(BF16)
    r1 = x - hi.astype(F32)
    mid = r1.astype(BF16)
    lo = (r1 - mid.astype(F32)).astype(BF16)
    return hi, mid, lo


def _params(*sem):
    return pltpu.CompilerParams(dimension_semantics=sem, vmem_limit_bytes=VMEM_LIMIT)


def _norm_mod(x, gain, shift, scale):
    r = lax.rsqrt(jnp.mean(x * x, axis=-1, keepdims=True) + EPS)
    return (x * r * gain) * (1.0 + scale) + shift


def _mod_spec_d(n_batch, which, d, skip=0):
    return pl.BlockSpec((None, None, 1, d), lambda b, t: (jnp.where(t + skip == 0, n_batch, b), which, 0, 0))


def _ada_kernel(c_ref, w_ref, b_ref, o_ref):
    c = c_ref[...]
    s = c * jax.nn.sigmoid(c)
    o_ref[0] = _dot(s.astype(BF16), w_ref[0].astype(BF16)) + b_ref[0]


def _ada(cc, w_ada, b_ada):
    depth, d, n = w_ada.shape
    bn = 1536
    return pl.pallas_call(
        _ada_kernel,
        grid=(depth, n // bn),
        in_specs=[pl.BlockSpec((cc.shape[0], d), lambda l, j: (0, 0)),
                  pl.BlockSpec((1, d, bn), lambda l, j: (l, 0, j)),
                  pl.BlockSpec((1, 1, bn), lambda l, j: (l, 0, j))],
        out_specs=pl.BlockSpec((1, cc.shape[0], bn), lambda l, j: (l, 0, j)),
        out_shape=jax.ShapeDtypeStruct((depth, cc.shape[0], n), F32),
        compiler_params=_params("arbitrary", "arbitrary"),
        name="ada",
    )(cc, w_ada, b_ada.reshape(depth, 1, n))


def _na_qkv_kernel(x_ref, g_ref, sh_ref, sc_ref, w_ref, o_ref):
    h = _norm_mod(x_ref[0], g_ref[...], sh_ref[...], sc_ref[...])
    o_ref[0] = _dot(h.astype(BF16), w_ref[...]).astype(BF16)


def _na_qkv(s, gain, mod, w):
    bsz, t, d = s.shape
    n = w.shape[1]
    return pl.pallas_call(
        _na_qkv_kernel,
        grid=(bsz, t // TM),
        in_specs=[pl.BlockSpec((1, TM, d), lambda b, i: (b, i, 0)),
                  pl.BlockSpec((1, d), lambda b, i: (0, 0)),
                  _mod_spec_d(bsz, 0, d), _mod_spec_d(bsz, 1, d),
                  pl.BlockSpec((d, n), lambda b, i: (0, 0))],
        out_specs=pl.BlockSpec((1, TM, n), lambda b, i: (b, i, 0)),
        out_shape=jax.ShapeDtypeStruct((bsz, t, n), BF16),
        compiler_params=_params("arbitrary", "arbitrary"),
        name="na_qkv",
    )(s, gain.reshape(1, d), mod, mod, w)


def _na_bias(rpb, rows):
    nblk = rows // NA_QROWS
    kmax = rows - NA_KROWS
    nrr, ncc = 2 * NA_KR - 1, 2 * NA_KC - 1
    qc = np.arange(GRID_W)[:, None]
    kc = np.arange(GRID_W)[None, :]
    c0 = np.clip(qc - NA_KC // 2, 0, GRID_W - NA_KC)
    col_ok = (kc >= c0) & (kc < c0 + NA_KC)
    col_pick = np.eye(ncc)[np.clip(kc - qc + NA_KC - 1, 0, ncc - 1)]
    row_pick, valid = [], []
    for blk in (0, 1, nblk - 1):
        rs = blk * NA_QROWS
        ks = min(max(rs - NA_KR // 2, 0), kmax)
        qr = rs + np.arange(NA_QROWS)[:, None]
        kr = ks + np.arange(NA_KROWS)[None, :]
        r0 = np.clip(qr - NA_KR // 2, 0, rows - NA_KR)
        row_ok = (kr >= r0) & (kr < r0 + NA_KR)
        row_pick.append(np.eye(nrr)[np.clip(kr - qr + NA_KR - 1, 0, nrr - 1)])
        valid.append(row_ok[:, None, :, None] & col_ok[None, :, None, :])
    bias = jnp.einsum('vrka,pxab,qcb->pvxrqkc', jnp.asarray(np.stack(row_pick), F32),
                      rpb.reshape(NA_HEADS // 2, 2, nrr, ncc), jnp.asarray(col_pick, F32),
                      precision=lax.Precision.HIGHEST)
    bias = jnp.where(np.stack(valid)[None, :, None], bias, NEG)
    return bias.reshape(NA_HEADS // 2, 3, 2, NA_QROWS * GRID_W, NA_KROWS * GRID_W).astype(F32)


def _na_attn_kernel(q_ref, k_ref, v_ref, bias_ref, qg_ref, kg_ref, o_ref, kn_ref, v1_ref, *, n_ctx, rows):
    t = pl.program_id(2)
    first = lax.broadcasted_iota(I32, (1, LANES), 1) < NA_HEAD_DIM
    nk = NA_KROWS * GRID_W

    def head_rms(z, gain):
        z2 = z * z
        sa = jnp.sum(jnp.where(first, z2, 0.0), axis=-1, keepdims=True)
        sb = jnp.sum(jnp.where(first, 0.0, z2), axis=-1, keepdims=True)
        r = jnp.where(first, lax.rsqrt(sa / NA_HEAD_DIM + EPS), lax.rsqrt(sb / NA_HEAD_DIM + EPS))
        return z * r * gain

    nb = q_ref.shape[0]

    @pl.when(t == 0)
    def _():
        def body(i, carry):
            off = pl.multiple_of(i * TM, TM)
            for bb in range(nb):
                kn_ref[bb, pl.ds(off, TM), :] = head_rms(k_ref[bb, pl.ds(off, TM), :].astype(F32),
                                                         kg_ref[...]).astype(BF16)
                v = v_ref[bb, pl.ds(off, TM), :]
                v1_ref[bb, pl.ds(off, TM), :] = jnp.concatenate([v, jnp.ones_like(v)], axis=1)
            return carry
        lax.fori_loop(0, k_ref.shape[1] // TM, body, 0)

    def stacked_q(bb):
        qn = (head_rms(q_ref[bb].astype(F32), qg_ref[...]) * (NA_HEAD_DIM ** -0.5)).astype(BF16)
        zero = jnp.zeros_like(qn)
        return jnp.concatenate([jnp.where(first, qn, zero), jnp.where(first, zero, qn)], axis=0)

    def finish(bb, o):
        oa, ob = o[:TM], o[TM:]
        o_ref[bb] = jnp.where(first, oa[:, :LANES] / oa[:, LANES:],
                              ob[:, :LANES] / ob[:, LANES:]).astype(o_ref.dtype)

    @pl.when(t == 0)
    def _():
        for bb in range(nb):
            s = _dot_nt(stacked_q(bb), kn_ref[bb, 0:n_ctx, :])
            m = jnp.max(s, axis=-1, keepdims=True)
            finish(bb, _dot(jnp.exp((s - m).astype(BF16)), v1_ref[bb, 0:n_ctx, :]))

    @pl.when(t > 0)
    def _():
        blk = t - 1
        nblk = rows // NA_QROWS
        ks = jnp.clip(blk * NA_QROWS - NA_KR // 2, 0, rows - NA_KROWS)
        start = pl.multiple_of(n_ctx + ks * GRID_W, GRID_W)
        kind = jnp.where(blk == 0, 0, jnp.where(blk == nblk - 1, 2, 1))
        bias = bias_ref[0, kind].reshape(2 * TM, nk)
        for bb in range(nb):
            q2 = stacked_q(bb)
            sw = _dot_nt(q2, kn_ref[bb, pl.ds(start, nk), :]) + bias
            sc = _dot_nt(q2, kn_ref[bb, 0:n_ctx, :])
            m = jnp.maximum(jnp.max(sw, axis=-1, keepdims=True), jnp.max(sc, axis=-1, keepdims=True))
            pw = jnp.exp((sw - m).astype(BF16))
            pc = jnp.exp((sc - m).astype(BF16))
            finish(bb, _dot(pw, v1_ref[bb, pl.ds(start, nk), :]) + _dot(pc, v1_ref[bb, 0:n_ctx, :]))


def _na_attn(qkv, bias, q_gain, k_gain, n_ctx):
    bsz, t, _ = qkv.shape
    d = NA_HEADS * NA_HEAD_DIM
    rows = (t - n_ctx) // GRID_W
    assert n_ctx == TM and NA_QROWS * GRID_W == TM and rows % NA_QROWS == 0 and rows >= NA_KROWS
    hp = NA_HEADS // 2
    nq, nk = bias.shape[-2:]
    gq = jnp.tile(q_gain, 2).reshape(1, LANES)
    gk = jnp.tile(k_gain, 2).reshape(1, LANES)
    nb = NA_NB if bsz % NA_NB == 0 else 1
    return pl.pallas_call(
        functools.partial(_na_attn_kernel, n_ctx=n_ctx, rows=rows),
        grid=(hp, bsz // nb, t // TM),
        in_specs=[pl.BlockSpec((nb, TM, LANES), lambda h, b, i: (b, i, h)),
                  pl.BlockSpec((nb, t, LANES), lambda h, b, i: (b, 0, hp + h)),
                  pl.BlockSpec((nb, t, LANES), lambda h, b, i: (b, 0, 2 * hp + h)),
                  pl.BlockSpec((1, 3, 2, nq, nk), lambda h, b, i: (h, 0, 0, 0, 0)),
                  pl.BlockSpec((1, LANES), lambda h, b, i: (0, 0)),
                  pl.BlockSpec((1, LANES), lambda h, b, i: (0, 0))],
        out_specs=pl.BlockSpec((nb, TM, LANES), lambda h, b, i: (b, i, h)),
        out_shape=jax.ShapeDtypeStruct((bsz, t, d), BF16),
        scratch_shapes=[pltpu.VMEM((nb, t, LANES), BF16), pltpu.VMEM((nb, t, 2 * LANES), BF16)],
        compiler_params=_params("arbitrary", "arbitrary", "arbitrary"),
        name="na_attn",
    )(qkv, qkv, qkv, bias, gq, gk)


def _rope_tables(n_ctx, seq):
    lane = np.arange(LANES)
    dd = lane % ML_DQK
    grp = dd // (ML_DQK // 2)
    j = dd % (ML_DQK // 2)
    n = ML_DQK // 4
    inv = ROPE_THETA ** (-(j % n).astype(np.float64) / n)
    pos = np.arange(seq)
    p = np.where(grp[None, :] == 0, (pos // GRID_W)[:, None], (pos % GRID_W)[:, None]).astype(np.float64)
    ang = p * inv[None, :]
    cos = np.cos(ang)
    sin = np.where(j[None, :] < n, -np.sin(ang), np.sin(ang))
    cos = np.concatenate([np.ones((n_ctx, LANES)), cos], axis=0)
    sin = np.concatenate([np.zeros((n_ctx, LANES)), sin], axis=0)
    return jnp.asarray(cos, F32), jnp.asarray(sin, F32)


def _ml_proj_kernel(x_ref, g_ref, sh_ref, sc_ref, wm_ref, wg_ref, wgt_ref, bgr_ref, bgc_ref, cos_ref, sin_ref,
                    q_ref, kt_ref, v_ref, og_ref, gr_ref, gc_ref, gm_ref):
    h = _norm_mod(x_ref[0], g_ref[...], sh_ref[...], sc_ref[...]).astype(BF16)
    main = _dot(h, wm_ref[...])
    nq = q_ref.shape[-1]
    nv = v_ref.shape[-1]
    cos = cos_ref[...]
    sin = sin_ref[...]
    n = ML_DQK // 4
    lane = lax.broadcasted_iota(I32, (1, LANES), 1)
    low = (lane % (2 * n)) < n

    def rope(z):
        zr = jnp.where(low, pltpu.roll(z, LANES - n, 1), pltpu.roll(z, n, 1))
        return z * cos + zr * sin

    for j in range(nq // LANES):
        sl = slice(j * LANES, (j + 1) * LANES)
        q_ref[0, :, sl] = (rope(main[:, sl]) * (ML_DQK ** -0.5)).astype(BF16)
        kt_ref[0, sl, :] = rope(main[:, nq + j * LANES: nq + (j + 1) * LANES]).T.astype(BF16)
    v_ref[0] = main[:, 2 * nq: 2 * nq + nv].astype(BF16)
    og_ref[0] = main[:, 2 * nq + nv:].astype(BF16)

    nh = ML_HEADS
    ti = lax.broadcasted_iota(I32, (TM, TM), 0)
    si = lax.broadcasted_iota(I32, (TM, TM), 1)
    le = (si <= ti).astype(BF16)
    ge = (si >= ti).astype(BF16)

    def capped(pre):
        return ML_GATE_CAP * jnp.tanh(pre / ML_GATE_CAP)

    def exact_cum(tri, x, nt):
        parts = _split3(x)
        if nt:
            return sum(_dot_nt(p, tri) for p in parts)
        return sum(_dot(tri, p) for p in parts)

    gc = capped(_dot(h, wg_ref[...]) + bgr_ref[...])
    lf = jax.nn.log_sigmoid(gc)
    pre = exact_cum(le, lf, False)
    suf = exact_cum(ge, lf, False)
    cum = jnp.where(lane < 2 * nh, pre, suf)
    r = gc - pltpu.roll(cum, LANES - nh, 1)
    is_i = (lane // nh) % 2 == 0
    gc_ref[0] = jnp.where(is_i, r, cum)[:, :4 * nh]
    row = lax.broadcasted_iota(I32, (TM, 1), 0)
    pmax = r
    smax = r
    step = 1
    while step < TM:
        pmax = jnp.maximum(pmax, jnp.where(row >= step, pltpu.roll(pmax, step, 0), NEG))
        smax = jnp.maximum(smax, jnp.where(row < TM - step, pltpu.roll(smax, TM - step, 0), NEG))
        step *= 2
    gm_ref[0] = jnp.where(lane < 2 * nh, pmax, smax)[:, :4 * nh]

    gr = capped(_dot_nt(wgt_ref[...], h) + bgc_ref[...])
    lfr = jax.nn.log_sigmoid(gr)
    pre_r = exact_cum(le, lfr, True)
    suf_r = exact_cum(ge, lfr, True)
    gr_ref[0, 0 * nh:1 * nh, :] = gr[0 * nh:1 * nh] - pre_r[1 * nh:2 * nh]
    gr_ref[0, 1 * nh:2 * nh, :] = pre_r[1 * nh:2 * nh]
    gr_ref[0, 2 * nh:3 * nh, :] = gr[2 * nh:3 * nh] - suf_r[3 * nh:4 * nh]
    gr_ref[0, 3 * nh:4 * nh, :] = suf_r[3 * nh:4 * nh]


def _ml_proj(s, gain, mod, w_in, b_gate, n_ctx):
    bsz, t, d = s.shape
    nq = ML_HEADS * ML_DQK
    nv = ML_HEADS * ML_DV
    ng = 4 * ML_HEADS
    assert TM == ML_CHUNK
    wm = w_in[:, :2 * nq + nv + d].astype(BF16)
    wg = w_in[:, 2 * nq + nv + d:]
    wg_pad = jnp.pad(wg, ((0, 0), (0, LANES - ng))).astype(BF16)
    wgt = wg.T.astype(BF16)
    bgr = jnp.pad(b_gate, (0, LANES - ng)).reshape(1, LANES)
    bgc = b_gate.reshape(ng, 1)
    cos, sin = _rope_tables(n_ctx, t - n_ctx)
    tile = lambda n: pl.BlockSpec((1, TM, n), lambda b, i: (b, i, 0))
    const = lambda a: pl.BlockSpec(a.shape, lambda b, i: (0, 0))
    return pl.pallas_call(
        _ml_proj_kernel,
        grid=(bsz, t // TM),
        in_specs=[tile(d), pl.BlockSpec((1, d), lambda b, i: (0, 0)),
                  _mod_spec_d(bsz, 0, d), _mod_spec_d(bsz, 1, d),
                  const(wm), const(wg_pad), const(wgt), const(bgr), const(bgc),
                  pl.BlockSpec((TM, LANES), lambda b, i: (i, 0)),
                  pl.BlockSpec((TM, LANES), lambda b, i: (i, 0))],
        out_specs=[tile(nq), pl.BlockSpec((1, nq, TM), lambda b, i: (b, 0, i)), tile(nv), tile(d),
                   pl.BlockSpec((1, ng, TM), lambda b, i: (b, 0, i)),
                   tile(ng), tile(ng)],
        out_shape=[jax.ShapeDtypeStruct((bsz, t, nq), BF16), jax.ShapeDtypeStruct((bsz, nq, t), BF16),
                   jax.ShapeDtypeStruct((bsz, t, nv), BF16), jax.ShapeDtypeStruct((bsz, t, d), BF16),
                   jax.ShapeDtypeStruct((bsz, ng, t), F32), jax.ShapeDtypeStruct((bsz, t, ng), F32),
                   jax.ShapeDtypeStruct((bsz, t, ng), F32)],
        compiler_params=_params("arbitrary", "arbitrary"),
        name="ml_proj",
    )(s, gain.reshape(1, d), mod, mod, wm, wg_pad, wgt, bgr, bgc, cos, sin)


def _mlstm_kernel(q_ref, kt_ref, v_ref, og_ref, gr_ref, gc_ref, gm_ref, mg_ref, o_ref, hf_ref, hb_ref, c_ref):
    L = ML_CHUNK
    nchunk = q_ref.shape[1] // L
    first = lax.broadcasted_iota(I32, (1, LANES), 1) < ML_DQK
    first_row = lax.broadcasted_iota(I32, (LANES, 1), 0) < ML_DQK
    ti = lax.broadcasted_iota(I32, (L, L), 0)
    si = lax.broadcasted_iota(I32, (L, L), 1)
    masks = (si <= ti, si >= ti)
    ones_blk = jnp.ones((L, ML_DV), BF16)

    c_ref[...] = jnp.zeros_like(c_ref)

    def chunk(c, a, bw, m0):
        off = pl.multiple_of(c * L, L)
        q = q_ref[0, pl.ds(off, L), :]
        kt = kt_ref[0, :, pl.ds(off, L)]
        qa = jnp.where(first if a == 0 else jnp.logical_not(first), q, jnp.zeros_like(q))
        kta = jnp.where(first_row if a == 0 else jnp.logical_not(first_row), kt, jnp.zeros_like(kt))
        v = v_ref[0, pl.ds(off, L), a * ML_DV:(a + 1) * ML_DV]
        w = 2 if bw else 0
        r_row = gr_ref[0, 0, 2 * w + a: 2 * w + a + 1, pl.ds(off, L)]
        cum_row = gr_ref[0, 0, 2 * (w + 1) + a: 2 * (w + 1) + a + 1, pl.ds(off, L)]
        cum_col = gc_ref[0, 0, pl.ds(off, L), 2 * (w + 1) + a: 2 * (w + 1) + a + 1]
        tot = cum_row[:, 0:1] if bw else cum_row[:, L - 1:L]
        mask = masks[1 if bw else 0]
        mx = gm_ref[0, 0, pl.ds(off, L), w + a: w + a + 1]
        mm = jnp.broadcast_to(jnp.maximum(m0, mx), (L, LANES))
        cum_b = jnp.broadcast_to(cum_col, (L, LANES))
        e = jnp.exp(jnp.where(mask, r_row - jnp.concatenate([mm] * (L // LANES), axis=1), NEG))
        s = (_dot(qa, kt) * e).astype(BF16)
        v2 = jnp.concatenate([v, ones_blk], axis=1)
        idx = 2 * a + (1 if bw else 0)
        cst = c_ref[idx]
        w_inter = jnp.exp(m0 - mm)
        both = _dot(s, v2) + jnp.concatenate([w_inter, w_inter], axis=1) * _dot(qa, cst.astype(BF16))
        num = both[:, :ML_DV]
        den = both[:, ML_DV:]
        hout = num / jnp.maximum(jnp.abs(den), jnp.exp(-cum_b - mm))
        dst = hb_ref if bw else hf_ref
        dst[pl.ds(off, L), a * ML_DV:(a + 1) * ML_DV] = hout
        m_loc = tot + jnp.max(r_row, axis=1, keepdims=True)
        kte = (kta.astype(F32) * jnp.exp(tot + r_row - m_loc)).astype(BF16)
        u = _dot(kte, v2)
        m_new = jnp.maximum(tot + m0, m_loc)
        c_ref[idx] = jnp.exp(tot + m0 - m_new) * cst + jnp.exp(m_loc - m_new) * u
        return m_new

    def body(i, ms):
        cb = jnp.where(i == 0, 0, nchunk - i)
        return (chunk(i, 0, False, ms[0]), chunk(i, 1, False, ms[1]),
                chunk(cb, 0, True, ms[2]), chunk(cb, 1, True, ms[3]))

    z = jnp.zeros((1, 1), F32)
    lax.fori_loop(0, nchunk, body, (z, z, z, z))

    def epilogue(i, carry):
        off = pl.multiple_of(i * L, L)
        for a in range(2):
            sl = slice(a * ML_DV, (a + 1) * ML_DV)
            hs = hf_ref[pl.ds(off, L), sl] + hb_ref[pl.ds(off, L), sl]
            hn = hs * lax.rsqrt(jnp.mean(hs * hs, axis=-1, keepdims=True) + EPS) * mg_ref[:, sl]
            o_ref[0, pl.ds(off, L), sl] = (hn * jax.nn.sigmoid(og_ref[0, pl.ds(off, L), sl].astype(F32))).astype(BF16)
        return carry

    lax.fori_loop(0, nchunk, epilogue, 0)


def _mlstm(q, kt, v, og, gr, gc, gm, mh_gain):
    bsz, t, _ = q.shape
    hp = ML_HEADS // 2
    nv = ML_HEADS * ML_DV
    w2 = 2 * ML_DV
    gr = gr.reshape(bsz, 4, hp, 2, t).transpose(0, 2, 1, 3, 4).reshape(bsz, hp, 8, t)
    gc = gc.reshape(bsz, t, 4, hp, 2).transpose(0, 3, 1, 2, 4).reshape(bsz, hp, t, 8)
    gm = gm.reshape(bsz, t, 4, hp, 2)[:, :, 0::2].transpose(0, 3, 1, 2, 4).reshape(bsz, hp, t, 4)
    return pl.pallas_call(
        _mlstm_kernel,
        grid=(bsz, hp),
        in_specs=[pl.BlockSpec((1, t, LANES), lambda b, h: (b, 0, h)),
                  pl.BlockSpec((1, LANES, t), lambda b, h: (b, h, 0)),
                  pl.BlockSpec((1, t, w2), lambda b, h: (b, 0, h)),
                  pl.BlockSpec((1, t, w2), lambda b, h: (b, 0, h)),
                  pl.BlockSpec((1, 1, 8, t), lambda b, h: (b, h, 0, 0)),
                  pl.BlockSpec((1, 1, t, 8), lambda b, h: (b, h, 0, 0)),
                  pl.BlockSpec((1, 1, t, 4), lambda b, h: (b, h, 0, 0)),
                  pl.BlockSpec((1, w2), lambda b, h: (0, h))],
        out_specs=pl.BlockSpec((1, t, w2), lambda b, h: (b, 0, h)),
        out_shape=jax.ShapeDtypeStruct((bsz, t, nv), BF16),
        scratch_shapes=[pltpu.VMEM((t, w2), F32), pltpu.VMEM((t, w2), F32),
                        pltpu.VMEM((4, LANES, w2), F32)],
        compiler_params=_params("arbitrary", "arbitrary"),
        name="mlstm",
    )(q, kt, v, og, gr, gc, gm, mh_gain.reshape(1, nv))


def _moe_pre_kernel(y_ref, wo_ref, x_ref, g1_ref, g_ref, sh_ref, sc_ref, whi_ref, wlo_ref, br_ref,
                    s_ref, h_ref, id_ref, wt_ref, rk_ref, cnt_ref):
    step = pl.program_id(0) * pl.num_programs(1) + pl.program_id(1)

    @pl.when(step == 0)
    def _():
        cnt_ref[...] = jnp.zeros_like(cnt_ref)

    ti = lax.broadcasted_iota(I32, (TM, TM), 0)
    si = lax.broadcasted_iota(I32, (TM, TM), 1)
    before = (ti < si).astype(BF16)
    whi = whi_ref[...]
    for bb in range(x_ref.shape[0]):
        x = x_ref[bb] + g1_ref[bb] * _dot(y_ref[bb], wo_ref[...])
        s_ref[bb] = x
        h = _norm_mod(x, g_ref[...], sh_ref[bb], sc_ref[bb])
        h_ref[bb] = _pack_pairs(h)
        hhi = h.astype(BF16)
        hlo = (h - hhi.astype(F32)).astype(BF16)
        logits = _dot_nt(whi, hhi) + _dot_nt(whi, hlo) + _dot_nt(wlo_ref[...], hhi) + br_ref[...]
        ne = logits.shape[0]
        eidx = lax.broadcasted_iota(I32, logits.shape, 0)
        sels, vals, ids = [], [], []
        l = logits
        for _ in range(TOP_K):
            mval = jnp.max(l, axis=0, keepdims=True)
            idx = jnp.min(jnp.where(l == mval, eidx, ne), axis=0, keepdims=True)
            sel = eidx == idx
            l = jnp.where(sel, -jnp.inf, l)
            sels.append(sel)
            vals.append(mval)
            ids.append(idx)
        ex = [jnp.exp(v - vals[0]) for v in vals]
        tot = ex[0] + ex[1] + ex[2] + ex[3]
        member = sels[0] | sels[1] | sels[2] | sels[3]
        rank_e = _dot(member.astype(BF16), before) + cnt_ref[:, 0:1]
        for kk in range(TOP_K):
            id_ref[bb, kk:kk + 1, :] = ids[kk]
            wt_ref[bb, kk:kk + 1, :] = ex[kk] / tot
            rk_ref[bb, kk:kk + 1, :] = jnp.sum(jnp.where(sels[kk], rank_e, 0.0), axis=0, keepdims=True).astype(I32)
        cnt_ref[...] = cnt_ref[...] + jnp.sum(member.astype(F32), axis=1, keepdims=True)


def _moe_pre(y, w_o, s, gain, mod, w_router, b_router, skip):
    bsz, t, d = s.shape
    nt = t // TM - skip
    nb = PRE_NB if bsz % PRE_NB == 0 else 1
    wt = w_router.T
    whi = wt.astype(BF16)
    wlo = (wt - whi.astype(F32)).astype(BF16)
    tok = lambda dt: jax.ShapeDtypeStruct((bsz, TOP_K, nt * TM), dt)
    tok_spec = pl.BlockSpec((nb, TOP_K, TM), lambda b, i: (b, 0, i))
    tile = lambda w: pl.BlockSpec((nb, TM, w), lambda b, i: (b, i + skip, 0))
    const = lambda a: pl.BlockSpec(a.shape, lambda b, i: (0, 0))
    mods = lambda which: pl.BlockSpec((nb, None, 1, d),
                                      lambda b, i: (jnp.where(i + skip == 0, bsz // nb, b), which, 0, 0))
    return pl.pallas_call(
        _moe_pre_kernel,
        grid=(bsz // nb, nt),
        in_specs=[tile(y.shape[-1]), const(w_o), tile(d), mods(2),
                  pl.BlockSpec((1, d), lambda b, i: (0, 0)), mods(3), mods(4),
                  const(whi), const(wlo), pl.BlockSpec((N_EXPERTS, 1), lambda b, i: (0, 0))],
        out_specs=[tile(d), pl.BlockSpec((nb, TM, d // 2), lambda b, i: (b, i, 0)), tok_spec, tok_spec, tok_spec,
                   pl.BlockSpec((N_EXPERTS, LANES), lambda b, i: (0, 0))],
        out_shape=[jax.ShapeDtypeStruct(s.shape, F32),
                   jax.ShapeDtypeStruct((bsz, nt * TM, d // 2), I32), tok(I32), tok(F32), tok(I32),
                   jax.ShapeDtypeStruct((N_EXPERTS, LANES), F32)],
        input_output_aliases={2: 0},
        compiler_params=_params("arbitrary", "arbitrary"),
        name="moe_pre",
    )(y, w_o, s, mod, gain.reshape(1, d), mod, mod, whi, wlo, b_router.reshape(N_EXPERTS, 1))


def _sc_mesh():
    return plsc.VectorSubcoreMesh(core_axis_name="c", subcore_axis_name="s",
                                  num_cores=SC_CORES, num_subcores=SC_SUBCORES)


def _sc_worker_chunks(nchunks, body):
    n_workers = SC_CORES * SC_SUBCORES
    per_worker = -(-nchunks // n_workers)
    wid = lax.axis_index("s") * SC_CORES + lax.axis_index("c")

    @pl.loop(0, per_worker)
    def _(i):
        c = wid * per_worker + i

        @pl.when(c < nchunks)
        def _():
            body(c)


def _dispatch(hp, idx3, n_slots):
    n, w = hp.shape
    nchunks, kk, ch = idx3.shape

    @functools.partial(pl.kernel, mesh=_sc_mesh(), out_type=jax.ShapeDtypeStruct((n_slots, w), hp.dtype),
                       scratch_types=[pltpu.VMEM((kk, ch), I32), pltpu.VMEM((ch, w), hp.dtype)],
                       name="moe_dispatch")
    def scatter(src_hbm, idx_hbm, out_hbm, idx_v, rows_v):
        def body(c):
            pltpu.sync_copy(idx_hbm.at[c], idx_v)
            pltpu.sync_copy(src_hbm.at[pl.ds(c * ch, ch)], rows_v)
            for j in range(kk):
                pltpu.sync_copy(rows_v, out_hbm.at[idx_v.at[j]])
        _sc_worker_chunks(nchunks, body)

    return scatter(hp, idx3)


def _gather(ys, idx3):
    nchunks, kk, ch = idx3.shape
    w = ys.shape[1]

    @functools.partial(pl.kernel, mesh=_sc_mesh(), out_type=jax.ShapeDtypeStruct((kk, nchunks * ch, w), ys.dtype),
                       scratch_types=[pltpu.VMEM((kk, ch), I32), pltpu.VMEM((ch, w), ys.dtype)],
                       name="moe_gather")
    def gather(tab_hbm, idx_hbm, out_hbm, idx_v, rows_v):
        def body(c):
            pltpu.sync_copy(idx_hbm.at[c], idx_v)
            for j in range(kk):
                pltpu.sync_copy(tab_hbm.at[idx_v.at[j]], rows_v)
                pltpu.sync_copy(rows_v, out_hbm.at[j, pl.ds(c * ch, ch)])
        _sc_worker_chunks(nchunks, body)

    return gather(ys, idx3)


def _pack_pairs(x):
    w = x.shape[1] // 2
    bits = lax.bitcast_convert_type(x.astype(BF16).astype(F32), jnp.uint32)
    return lax.bitcast_convert_type((bits[:, :w] >> 16) | (bits[:, w:] & jnp.uint32(0xFFFF0000)), I32)


def _unpack_pairs(p):
    u = lax.bitcast_convert_type(p, jnp.uint32)
    lo = lax.bitcast_convert_type(u << 16, F32)
    hi = lax.bitcast_convert_type(u & jnp.uint32(0xFFFF0000), F32)
    return jnp.concatenate([lo, hi], axis=1)


def _ffn_kernel(te_ref, nv_ref, x_ref, w1_ref, b1_ref, w2_ref, b2_ref, y_ref, w1b_ref, w2b_ref):
    i = pl.program_id(0)

    @pl.when(jnp.logical_or(i == 0, te_ref[i] != te_ref[jnp.maximum(i - 1, 0)]))
    def _():
        w1b_ref[...] = w1_ref[0].astype(BF16)
        w2b_ref[...] = w2_ref[0].astype(BF16)

    nv = nv_ref[i]
    half = x_ref.shape[0] // 2

    def expert(rows):
        dff = w2_ref.shape[1]
        row = lax.broadcasted_iota(I32, (rows, 1), 0)
        x = _unpack_pairs(jnp.where(row < nv, x_ref[0:rows, :], 0)).astype(BF16)
        u = _dot(x, w1b_ref[...]) + b1_ref[0]
        glu = jnp.minimum(u[:, :dff], SWIGLU_LIMIT)
        lin = jnp.clip(u[:, dff:], -SWIGLU_LIMIT, SWIGLU_LIMIT)
        act = glu * jax.nn.sigmoid(SWIGLU_ALPHA * glu) * (lin + 1.0)
        y_ref[0:rows, :] = _pack_pairs(_dot(act.astype(BF16), w2b_ref[...]) + b2_ref[0])

    @pl.when(nv > half)
    def _():
        expert(2 * half)

    @pl.when(jnp.logical_and(nv > 0, nv <= half))
    def _():
        expert(half)
        y_ref[half:, :] = jnp.zeros((half, y_ref.shape[1]), y_ref.dtype)

    @pl.when(nv <= 0)
    def _():
        y_ref[...] = jnp.zeros_like(y_ref)


def _ffn(xs, tile_expert, n_valid, layer, w1, b1, w2, b2):
    ns, wd = xs.shape
    depth, ne, d, n1 = w1.shape
    dff = w2.shape[2]
    row = lambda i, te, nv: (i, 0)
    exp4 = lambda i, te, nv: (layer, te[i], 0, 0)
    return pl.pallas_call(
        _ffn_kernel,
        grid_spec=pltpu.PrefetchScalarGridSpec(
            num_scalar_prefetch=2,
            grid=(ns // TMX,),
            in_specs=[pl.BlockSpec((TMX, wd), row),
                      pl.BlockSpec((None, 1, d, n1), exp4), pl.BlockSpec((None, 1, 1, n1), exp4),
                      pl.BlockSpec((None, 1, dff, d), exp4), pl.BlockSpec((None, 1, 1, d), exp4)],
            out_specs=pl.BlockSpec((TMX, wd), row),
            scratch_shapes=[pltpu.VMEM((d, n1), BF16), pltpu.VMEM((dff, d), BF16)]),
        out_shape=jax.ShapeDtypeStruct((ns, wd), I32),
        compiler_params=pltpu.CompilerParams(dimension_semantics=("arbitrary",), vmem_limit_bytes=FFN_VMEM_LIMIT),
        name="moe_ffn",
    )(tile_expert, n_valid, xs, w1, b1.reshape(depth, ne, 1, n1), w2, b2.reshape(depth, ne, 1, d))


def _combine_kernel(wt_ref, s_ref, g_ref, yk_ref, o_ref):
    wt = wt_ref[...]
    f = wt[:, 0:1] * _unpack_pairs(yk_ref[0])
    for kk in range(1, TOP_K):
        f = f + wt[:, kk:kk + 1] * _unpack_pairs(yk_ref[kk])
    o_ref[0] = s_ref[0] + g_ref[...] * f


def _combine(yk, wts, s, mod, which, skip):
    bsz, t, d = s.shape
    nt = t // TM - skip
    return pl.pallas_call(
        _combine_kernel,
        grid=(bsz, nt),
        in_specs=[pl.BlockSpec((TM, TOP_K), lambda b, i: (b * nt + i, 0)),
                  pl.BlockSpec((1, TM, d), lambda b, i: (b, i + skip, 0)),
                  _mod_spec_d(bsz, which, d, skip),
                  pl.BlockSpec((TOP_K, TM, yk.shape[-1]), lambda b, i: (0, b * nt + i, 0))],
        out_specs=pl.BlockSpec((1, TM, d), lambda b, i: (b, i, 0)),
        out_shape=jax.ShapeDtypeStruct((bsz, nt * TM, d), F32),
        input_output_aliases={} if skip else {1: 0},
        compiler_params=_params("arbitrary", "arbitrary"),
        name="moe_combine",
    )(wts, s, mod, yk)


def _moe(y, w_o, s, gain, mod, layer, w_router, b_router, w1, b1, w2, b2, skip=0):
    bsz, t, d = s.shape
    n = bsz * (t - skip * TM)
    assert n % SC_CHUNK == 0
    s, hp, ids, wts, ranks, counts = _moe_pre(y, w_o, s, gain, mod, w_router, b_router, skip)
    cnt = counts[:, 0].astype(I32)
    padded = ((cnt + TMX - 1) // TMX) * TMX
    ends = jnp.cumsum(padded)
    offs = ends - padded
    n_tiles = (TOP_K * n + N_EXPERTS * (TMX - 1) + TMX - 1) // TMX
    tile_start = jnp.arange(n_tiles, dtype=I32) * TMX
    te = jnp.sum(tile_start[:, None] >= ends[None, :], axis=1).astype(I32)
    te_last = jnp.max(jnp.where(cnt > 0, jnp.arange(N_EXPERTS, dtype=I32), 0))
    te = jnp.minimum(te, te_last)
    group_end = jnp.sum(jnp.where(te[:, None] == jnp.arange(N_EXPERTS, dtype=I32), offs + cnt, 0), axis=-1)
    n_valid = jnp.where(tile_start < ends[-1], jnp.clip(group_end - tile_start, 0, TMX), 0).astype(I32)
    slots = ranks + jnp.sum(jnp.where(ids[..., None] == jnp.arange(N_EXPERTS, dtype=I32), offs, 0), axis=-1)
    idx3 = slots.reshape(bsz, TOP_K, -1, SC_CHUNK).transpose(0, 2, 1, 3).reshape(n // SC_CHUNK, TOP_K, SC_CHUNK)
    xs = _dispatch(hp.reshape(n, d // 2), idx3, n_tiles * TMX)
    ys = _ffn(xs, te, n_valid, layer, w1, b1, w2, b2)
    return _combine(_gather(ys, idx3), wts.transpose(0, 2, 1).reshape(n, TOP_K), s, mod, 5, skip)


def kernel(x, c, ctx, c_ctx, w_ada, b_ada, g_norm_mix, g_norm_ffn, na_w_qkv, na_q_gain, na_k_gain, na_rpb,
           na_w_o, ml_w_in, ml_b_gate, ml_mh_gain, ml_w_o, moe_w_router, moe_b_router, moe_w1, moe_b1,
           moe_w2, moe_b2):
    bsz, seq, d = x.shape
    n_ctx = ctx.shape[1]
    depth = w_ada.shape[0]
    rows = seq // GRID_W
    n_mod_rows = ((bsz + PRE_NB + 7) // 8) * 8
    cc = jnp.concatenate([c, jnp.tile(c_ctx[None, :], (PRE_NB, 1)),
                          jnp.zeros((n_mod_rows - bsz - PRE_NB, d), F32)], axis=0)
    mod_all = _ada(cc, w_ada, b_ada).reshape(depth, n_mod_rows, 6, 1, d)
    s = jnp.concatenate([ctx, x], axis=1)
    for i in range(depth):
        j = i // 2
        mod = mod_all[i]
        if i % 2 == 0:
            qkv = _na_qkv(s, g_norm_mix[i], mod, na_w_qkv[j].astype(BF16))
            y = _na_attn(qkv, _na_bias(na_rpb[j], rows), na_q_gain[j], na_k_gain[j], n_ctx)
            w_o = na_w_o[j]
        else:
            q, kt, v, og, gr, gc, gm = _ml_proj(s, g_norm_mix[i], mod, ml_w_in[j], ml_b_gate[j], n_ctx)
            y = _mlstm(q, kt, v, og, gr, gc, gm, ml_mh_gain[j])
            w_o = ml_w_o[j]
        skip = n_ctx // TM if i == depth - 1 else 0
        s = _moe(y, w_o.astype(BF16), s, g_norm_ffn[i], mod, i, moe_w_router[i], moe_b_router[i],
                 moe_w1, moe_b1, moe_w2, moe_b2, skip)
    return s
```

```python
import functools
import math

import numpy as np
import jax
import jax.numpy as jnp
from jax import lax
from jax.experimental import pallas as pl
from jax.experimental.pallas import tpu as pltpu
from jax.experimental.pallas import tpu_sc as plsc

F32 = jnp.float32
BF16 = jnp.bfloat16
I32 = jnp.int32

EPS = 1e-6
GRID_W = 64
NA_HEADS = 16
NA_HEAD_DIM = 64
NA_KR = 8
NA_KC = 16
NA_QROWS = 4
NA_KROWS = NA_QROWS + NA_KR - 1
NA_NB = 4
PRE_NB = 4
ML_HEADS = 8
ML_DV = 128
ML_DQK = 64
ML_CHUNK = 256
ML_GATE_CAP = 15.0
ROPE_THETA = 10000.0
N_EXPERTS = 32
TOP_K = 4
SWIGLU_LIMIT = 7.0
SWIGLU_ALPHA = 1.702

TM = 256
TMX = 512
LANES = 128
NEG = -1e30
VMEM_LIMIT = 48 * 1024 * 1024
SC_CORES = 2
SC_SUBCORES = 16
SC_CHUNK = 128
FFN_VMEM_LIMIT =56 * 1024 * 1024


def _dot(a, b):
    return jnp.dot(a, b, preferred_element_type=F32)


def _dot_nt(a, b):
    return lax.dot_general(a, b, (((1,), (1,)), ((), ())), preferred_element_type=F32)


def _split3(x):
    hi = x.astype(BF16)
    r1 = x - hi.astype(F32)
    mid = r1.astype(BF16)
    lo = (r1 - mid.astype(F32)).astype(BF16)
    return hi, mid, lo


def _params(*sem):
    return pltpu.CompilerParams(dimension_semantics=sem, vmem_limit_bytes=VMEM_LIMIT)


def _norm_mod(x, gain, shift, scale):
    r = lax.rsqrt(jnp.mean(x * x, axis=-1, keepdims=True) + EPS)
    return (x * r * gain) * (1.0 + scale) + shift


def _mod_spec_d(n_batch, which, d, skip=0):
    return pl.BlockSpec((None, None, 1, d), lambda b, t: (jnp.where(t + skip == 0, n_batch, b), which, 0, 0))


def _ada_kernel(c_ref, w_ref, b_ref, o_ref):
    c = c_ref[...]
    s = c * jax.nn.sigmoid(c)
    o_ref[0] = _dot(s.astype(BF16), w_ref[0].astype(BF16)) + b_ref[0]


def _ada(cc, w_ada, b_ada):
    depth, d, n = w_ada.shape
    bn = 1536
    return pl.pallas_call(
        _ada_kernel,
        grid=(depth, n // bn),
        in_specs=[pl.BlockSpec((cc.shape[0], d), lambda l, j: (0, 0)),
                  pl.BlockSpec((1, d, bn), lambda l, j: (l, 0, j)),
                  pl.BlockSpec((1, 1, bn), lambda l, j: (l, 0, j))],
        out_specs=pl.BlockSpec((1, cc.shape[0], bn), lambda l, j: (l, 0, j)),
        out_shape=jax.ShapeDtypeStruct((depth, cc.shape[0], n), F32),
        compiler_params=_params("arbitrary", "arbitrary"),
        name="ada",
    )(cc, w_ada, b_ada.reshape(depth, 1, n))


def _na_qkv_kernel(x_ref, g_ref, sh_ref, sc_ref, w_ref, o_ref):
    h = _norm_mod(x_ref[0], g_ref[...], sh_ref[...], sc_ref[...])
    o_ref[0] = _dot(h.astype(BF16), w_ref[...]).astype(BF16)


def _na_qkv(s, gain, mod, w):
    bsz, t, d = s.shape
    n = w.shape[1]
    return pl.pallas_call(
        _na_qkv_kernel,
        grid=(bsz, t // TM),
        in_specs=[pl.BlockSpec((1, TM, d), lambda b, i: (b, i, 0)),
                  pl.BlockSpec((1, d), lambda b, i: (0, 0)),
                  _mod_spec_d(bsz, 0, d), _mod_spec_d(bsz, 1, d),
                  pl.BlockSpec((d, n), lambda b, i: (0, 0))],
        out_specs=pl.BlockSpec((1, TM, n), lambda b, i: (b, i, 0)),
        out_shape=jax.ShapeDtypeStruct((bsz, t, n), BF16),
        compiler_params=_params("arbitrary", "arbitrary"),
        name="na_qkv",
    )(s, gain.reshape(1, d), mod, mod, w)


def _na_bias(rpb, rows):
    nblk = rows // NA_QROWS
    kmax = rows - NA_KROWS
    nrr, ncc = 2 * NA_KR - 1, 2 * NA_KC - 1
    qc = np.arange(GRID_W)[:, None]
    kc = np.arange(GRID_W)[None, :]
    c0 = np.clip(qc - NA_KC // 2, 0, GRID_W - NA_KC)
    col_ok = (kc >= c0) & (kc < c0 + NA_KC)
    col_pick = np.eye(ncc)[np.clip(kc - qc + NA_KC - 1, 0, ncc - 1)]
    row_pick, valid = [], []
    for blk in (0, 1, nblk - 1):
        rs = blk * NA_QROWS
        ks = min(max(rs - NA_KR // 2, 0), kmax)
        qr = rs + np.arange(NA_QROWS)[:, None]
        kr = ks + np.arange(NA_KROWS)[None, :]
        r0 = np.clip(qr - NA_KR // 2, 0, rows - NA_KR)
        row_ok = (kr >= r0) & (kr < r0 + NA_KR)
        row_pick.append(np.eye(nrr)[np.clip(kr - qr + NA_KR - 1, 0, nrr - 1)])
        valid.append(row_ok[:, None, :, None] & col_ok[None, :, None, :])
    bias = jnp.einsum('vrka,pxab,qcb->pvxrqkc', jnp.asarray(np.stack(row_pick), F32),
                      rpb.reshape(NA_HEADS // 2, 2, nrr, ncc), jnp.asarray(col_pick, F32),
                      precision=lax.Precision.HIGHEST)
    bias = jnp.where(np.stack(valid)[None, :, None], bias, NEG)
    return bias.reshape(NA_HEADS // 2, 3, 2, NA_QROWS * GRID_W, NA_KROWS * GRID_W).astype(BF16)


def _na_attn_kernel(q_ref, k_ref, v_ref, bias_ref, qg_ref, kg_ref, o_ref, kn_ref, v1_ref, *, n_ctx, rows):
    t = pl.program_id(2)
    first = lax.broadcasted_iota(I32, (1, LANES), 1) < NA_HEAD_DIM
    nk = NA_KROWS * GRID_W

    def head_rms(z, gain):
        z2 = z * z
        sa = jnp.sum(jnp.where(first, z2, 0.0), axis=-1, keepdims=True)
        sb = jnp.sum(jnp.where(first, 0.0, z2), axis=-1, keepdims=True)
        r = jnp.where(first, lax.rsqrt(sa / NA_HEAD_DIM + EPS), lax.rsqrt(sb / NA_HEAD_DIM + EPS))
        return z * r * gain

    nb = q_ref.shape[0]

    @pl.when(t == 0)
    def _():
        def body(i, carry):
            off = pl.multiple_of(i * TM, TM)
            for bb in range(nb):
                kn_ref[bb, pl.ds(off, TM), :] = head_rms(k_ref[bb, pl.ds(off, TM), :].astype(F32),
                                                         kg_ref[...]).astype(BF16)
                v = v_ref[bb, pl.ds(off, TM), :]
                v1_ref[bb, pl.ds(off, TM), :] = jnp.concatenate([v, jnp.ones_like(v)], axis=1)
            return carry
        lax.fori_loop(0, k_ref.shape[1] // TM, body, 0)

    def stacked_q(bb):
        qn = (head_rms(q_ref[bb].astype(F32), qg_ref[...]) * (NA_HEAD_DIM ** -0.5)).astype(BF16)
        zero = jnp.zeros_like(qn)
        return jnp.concatenate([jnp.where(first, qn, zero), jnp.where(first, zero, qn)], axis=0)

    def finish(bb, o):
        oa, ob = o[:TM], o[TM:]
        o_ref[bb] = jnp.where(first, oa[:, :LANES] / oa[:, LANES:],
                              ob[:, :LANES] / ob[:, LANES:]).astype(o_ref.dtype)

    @pl.when(t == 0)
    def _():
        for bb in range(nb):
            s = _dot_nt(stacked_q(bb), kn_ref[bb, 0:n_ctx, :])
            m = jnp.max(s, axis=-1, keepdims=True)
            finish(bb, _dot(jnp.exp((s - m).astype(BF16)), v1_ref[bb, 0:n_ctx, :]))

    @pl.when(t > 0)
    def _():
        blk = t - 1
        nblk = rows // NA_QROWS
        ks = jnp.clip(blk * NA_QROWS - NA_KR // 2, 0, rows - NA_KROWS)
        start = pl.multiple_of(n_ctx + ks * GRID_W, GRID_W)
        kind = jnp.where(blk == 0, 0, jnp.where(blk == nblk - 1, 2, 1))
        bias = bias_ref[0, kind].reshape(2 * TM, nk).astype(F32)
        for bb in range(nb):
            q2 = stacked_q(bb)
            sw = _dot_nt(q2, kn_ref[bb, pl.ds(start, nk), :]) + bias
            sc = _dot_nt(q2, kn_ref[bb, 0:n_ctx, :])
            m = jnp.maximum(jnp.max(sw, axis=-1, keepdims=True), jnp.max(sc, axis=-1, keepdims=True))
            pw = jnp.exp((sw - m).astype(BF16))
            pc = jnp.exp((sc - m).astype(BF16))
            finish(bb, _dot(pw, v1_ref[bb, pl.ds(start, nk), :]) + _dot(pc, v1_ref[bb, 0:n_ctx, :]))


def _na_attn(qkv, bias, q_gain, k_gain, n_ctx):
    bsz, t, _ = qkv.shape
    d = NA_HEADS * NA_HEAD_DIM
    rows = (t - n_ctx) // GRID_W
    assert n_ctx == TM and NA_QROWS * GRID_W == TM and rows % NA_QROWS == 0 and rows >= NA_KROWS
    hp = NA_HEADS // 2
    nq, nk = bias.shape[-2:]
    gq = jnp.tile(q_gain, 2).reshape(1, LANES)
    gk = jnp.tile(k_gain, 2).reshape(1, LANES)
    nb = NA_NB if bsz % NA_NB == 0 else 1
    return pl.pallas_call(
        functools.partial(_na_attn_kernel, n_ctx=n_ctx, rows=rows),
        grid=(hp, bsz // nb, t // TM),
        in_specs=[pl.BlockSpec((nb, TM, LANES), lambda h, b, i: (b, i, h)),
                  pl.BlockSpec((nb, t, LANES), lambda h, b, i: (b, 0, hp + h)),
                  pl.BlockSpec((nb, t, LANES), lambda h, b, i: (b, 0, 2 * hp + h)),
                  pl.BlockSpec((1, 3, 2, nq, nk), lambda h, b, i: (h, 0, 0, 0, 0)),
                  pl.BlockSpec((1, LANES), lambda h, b, i: (0, 0)),
                  pl.BlockSpec((1, LANES), lambda h, b, i: (0, 0))],
        out_specs=pl.BlockSpec((nb, TM, LANES), lambda h, b, i: (b, i, h)),
        out_shape=jax.ShapeDtypeStruct((bsz, t, d), BF16),
        scratch_shapes=[pltpu.VMEM((nb, t, LANES), BF16), pltpu.VMEM((nb, t, 2 * LANES), BF16)],
        compiler_params=_params("arbitrary", "arbitrary", "arbitrary"),
        name="na_attn",
    )(qkv, qkv, qkv, bias, gq, gk)


def _rope_tables(n_ctx, seq):
    lane = np.arange(LANES)
    dd = lane % ML_DQK
    grp = dd // (ML_DQK // 2)
    j = dd % (ML_DQK // 2)
    n = ML_DQK // 4
    inv = ROPE_THETA ** (-(j % n).astype(np.float64) / n)
    pos = np.arange(seq)
    p = np.where(grp[None, :] == 0, (pos // GRID_W)[:, None], (pos % GRID_W)[:, None]).astype(np.float64)
    ang = p * inv[None, :]
    cos = np.cos(ang)
    sin = np.where(j[None, :] < n, -np.sin(ang), np.sin(ang))
    cos = np.concatenate([np.ones((n_ctx, LANES)), cos], axis=0)
    sin = np.concatenate([np.zeros((n_ctx, LANES)), sin], axis=0)
    return jnp.asarray(cos, F32), jnp.asarray(sin, F32)


def _ml_proj_kernel(x_ref, g_ref, sh_ref, sc_ref, wm_ref, wg_ref, wgt_ref, bgr_ref, bgc_ref, cos_ref, sin_ref,
                    q_ref, kt_ref, v_ref, og_ref, gr_ref, gc_ref, gm_ref):
    h = _norm_mod(x_ref[0], g_ref[...], sh_ref[...], sc_ref[...]).astype(BF16)
    main = _dot(h, wm_ref[...])
    nq = q_ref.shape[-1]
    nv = v_ref.shape[-1]
    cos = cos_ref[...]
    sin = sin_ref[...]
    n = ML_DQK // 4
    lane = lax.broadcasted_iota(I32, (1, LANES), 1)
    low = (lane % (2 * n)) < n

    def rope(z):
        zr = jnp.where(low, pltpu.roll(z, LANES - n, 1), pltpu.roll(z, n, 1))
        return z * cos + zr * sin

    for j in range(nq // LANES):
        sl = slice(j * LANES, (j + 1) * LANES)
        q_ref[0, :, sl] = (rope(main[:, sl]) * (ML_DQK ** -0.5)).astype(BF16)
        kt_ref[0, sl, :] = rope(main[:, nq + j * LANES: nq + (j + 1) * LANES]).T.astype(BF16)
    v_ref[0] = main[:, 2 * nq: 2 * nq + nv].astype(BF16)
    og_ref[0] = main[:, 2 * nq + nv:].astype(BF16)

    nh = ML_HEADS
    ti = lax.broadcasted_iota(I32, (TM, TM), 0)
    si = lax.broadcasted_iota(I32, (TM, TM), 1)
    le = (si <= ti).astype(BF16)
    ge = (si >= ti).astype(BF16)

    def capped(pre):
        return ML_GATE_CAP * jnp.tanh(pre / ML_GATE_CAP)

    def exact_cum(tri, x, nt):
        parts = _split3(x)
        if nt:
            return sum(_dot_nt(p, tri) for p in parts)
        return sum(_dot(tri, p) for p in parts)

    gc = capped(_dot(h, wg_ref[...]) + bgr_ref[...])
    lf = jax.nn.log_sigmoid(gc)
    pre = exact_cum(le, lf, False)
    suf = exact_cum(ge, lf, False)
    cum = jnp.where(lane < 2 * nh, pre, suf)
    r = gc - pltpu.roll(cum, LANES - nh, 1)
    is_i = (lane // nh) % 2 == 0
    gc_ref[0] = jnp.where(is_i, r, cum)[:, :4 * nh]
    row = lax.broadcasted_iota(I32, (TM, 1), 0)
    pmax = r
    smax = r
    step = 1
    while step < TM:
        pmax = jnp.maximum(pmax, jnp.where(row >= step, pltpu.roll(pmax, step, 0), NEG))
        smax = jnp.maximum(smax, jnp.where(row < TM - step, pltpu.roll(smax, TM - step, 0), NEG))
        step *= 2
    gm_ref[0] = jnp.where(lane < 2 * nh, pmax, smax)[:, :4 * nh]

    gr = capped(_dot_nt(wgt_ref[...], h) + bgc_ref[...])
    lfr = jax.nn.log_sigmoid(gr)
    pre_r = exact_cum(le, lfr, True)
    suf_r = exact_cum(ge, lfr, True)
    gr_ref[0, 0 * nh:1 * nh, :] = gr[0 * nh:1 * nh] - pre_r[1 * nh:2 * nh]
    gr_ref[0, 1 * nh:2 * nh, :] = pre_r[1 * nh:2 * nh]
    gr_ref[0, 2 * nh:3 * nh, :] = gr[2 * nh:3 * nh] - suf_r[3 * nh:4 * nh]
    gr_ref[0, 3 * nh:4 * nh, :] = suf_r[3 * nh:4 * nh]


def _ml_proj(s, gain, mod, w_in, b_gate, n_ctx):
    bsz, t, d = s.shape
    nq = ML_HEADS * ML_DQK
    nv = ML_HEADS * ML_DV
    ng = 4 * ML_HEADS
    assert TM == ML_CHUNK
    wm = w_in[:, :2 * nq + nv + d].astype(BF16)
    wg = w_in[:, 2 * nq + nv + d:]
    wg_pad = jnp.pad(wg, ((0, 0), (0, LANES - ng))).astype(BF16)
    wgt = wg.T.astype(BF16)
    bgr = jnp.pad(b_gate, (0, LANES - ng)).reshape(1, LANES)
    bgc = b_gate.reshape(ng, 1)
    cos, sin = _rope_tables(n_ctx, t - n_ctx)
    tile = lambda n: pl.BlockSpec((1, TM, n), lambda b, i: (b, i, 0))
    const = lambda a: pl.BlockSpec(a.shape, lambda b, i: (0, 0))
    return pl.pallas_call(
        _ml_proj_kernel,
        grid=(bsz, t // TM),
        in_specs=[tile(d), pl.BlockSpec((1, d), lambda b, i: (0, 0)),
                  _mod_spec_d(bsz, 0, d), _mod_spec_d(bsz, 1, d),
                  const(wm), const(wg_pad), const(wgt), const(bgr), const(bgc),
                  pl.BlockSpec((TM, LANES), lambda b, i: (i, 0)),
                  pl.BlockSpec((TM, LANES), lambda b, i: (i, 0))],
        out_specs=[tile(nq), pl.BlockSpec((1, nq, TM), lambda b, i: (b, 0, i)), tile(nv), tile(d),
                   pl.BlockSpec((1, ng, TM), lambda b, i: (b, 0, i)),
                   tile(ng), tile(ng)],
        out_shape=[jax.ShapeDtypeStruct((bsz, t, nq), BF16), jax.ShapeDtypeStruct((bsz, nq, t), BF16),
                   jax.ShapeDtypeStruct((bsz, t, nv), BF16), jax.ShapeDtypeStruct((bsz, t, d), BF16),
                   jax.ShapeDtypeStruct((bsz, ng, t), F32), jax.ShapeDtypeStruct((bsz, t, ng), F32),
                   jax.ShapeDtypeStruct((bsz, t, ng), F32)],
        compiler_params=_params("arbitrary", "arbitrary"),
        name="ml_proj",
    )(s, gain.reshape(1, d), mod, mod, wm, wg_pad, wgt, bgr, bgc, cos, sin)


def _mlstm_kernel(q_ref, kt_ref, v_ref, og_ref, gr_ref, gc_ref, gm_ref, mg_ref, o_ref, hf_ref, hb_ref, c_ref):
    L = ML_CHUNK
    nchunk = q_ref.shape[1] // L
    first = lax.broadcasted_iota(I32, (1, LANES), 1) < ML_DQK
    first_row = lax.broadcasted_iota(I32, (LANES, 1), 0) < ML_DQK
    ti = lax.broadcasted_iota(I32, (L, L), 0)
    si = lax.broadcasted_iota(I32, (L, L), 1)
    masks = (si <= ti, si >= ti)
    ones_blk = jnp.ones((L, ML_DV), BF16)

    c_ref[...] = jnp.zeros_like(c_ref)

    def chunk(c, a, bw, m0):
        off = pl.multiple_of(c * L, L)
        q = q_ref[0, pl.ds(off, L), :]
        kt = kt_ref[0, :, pl.ds(off, L)]
        qa = jnp.where(first if a == 0 else jnp.logical_not(first), q, jnp.zeros_like(q))
        kta = jnp.where(first_row if a == 0 else jnp.logical_not(first_row), kt, jnp.zeros_like(kt))
        v = v_ref[0, pl.ds(off, L), a * ML_DV:(a + 1) * ML_DV]
        w = 2 if bw else 0
        r_row = gr_ref[0, 0, 2 * w + a: 2 * w + a + 1, pl.ds(off, L)]
        cum_row = gr_ref[0, 0, 2 * (w + 1) + a: 2 * (w + 1) + a + 1, pl.ds(off, L)]
        cum_col = gc_ref[0, 0, pl.ds(off, L), 2 * (w + 1) + a: 2 * (w + 1) + a + 1]
        tot = cum_row[:, 0:1] if bw else cum_row[:, L - 1:L]
        mask = masks[1 if bw else 0]
        mx = gm_ref[0, 0, pl.ds(off, L), w + a: w + a + 1]
        mm = jnp.broadcast_to(jnp.maximum(m0, mx), (L, LANES))
        cum_b = jnp.broadcast_to(cum_col, (L, LANES))
        e = jnp.exp(jnp.where(mask, r_row - jnp.concatenate([mm] * (L // LANES), axis=1), NEG))
        s = (_dot(qa, kt) * e).astype(BF16)
        v2 = jnp.concatenate([v, ones_blk], axis=1)
        idx = 2 * a + (1 if bw else 0)
        cst = c_ref[idx]
        w_inter = jnp.exp(m0 - mm)
        both = _dot(s, v2) + jnp.concatenate([w_inter, w_inter], axis=1) * _dot(qa, cst.astype(BF16))
        num = both[:, :ML_DV]
        den = both[:, ML_DV:]
        hout = num / jnp.maximum(jnp.abs(den), jnp.exp(-cum_b - mm))
        dst = hb_ref if bw else hf_ref
        dst[pl.ds(off, L), a * ML_DV:(a + 1) * ML_DV] = hout
        m_loc = tot + jnp.max(r_row, axis=1, keepdims=True)
        kte = (kta.astype(F32) * jnp.exp(tot + r_row - m_loc)).astype(BF16)
        u = _dot(kte, v2)
        m_new = jnp.maximum(tot + m0, m_loc)
        c_ref[idx] = jnp.exp(tot + m0 - m_new) * cst + jnp.exp(m_loc - m_new) * u
        return m_new

    def body(i, ms):
        cb = jnp.where(i == 0, 0, nchunk - i)
        return (chunk(i, 0, False, ms[0]), chunk(i, 1, False, ms[1]),
                chunk(cb, 0, True, ms[2]), chunk(cb, 1, True, ms[3]))

    z = jnp.zeros((1, 1), F32)
    lax.fori_loop(0, nchunk, body, (z, z, z, z))

    def epilogue(i, carry):
        off = pl.multiple_of(i * L, L)
        for a in range(2):
            sl = slice(a * ML_DV, (a + 1) * ML_DV)
            hs = hf_ref[pl.ds(off, L), sl] + hb_ref[pl.ds(off, L), sl]
            hn = hs * lax.rsqrt(jnp.mean(hs * hs, axis=-1, keepdims=True) + EPS) * mg_ref[:, sl]
            o_ref[0, pl.ds(off, L), sl] = (hn * jax.nn.sigmoid(og_ref[0, pl.ds(off, L), sl].astype(F32))).astype(BF16)
        return carry

    lax.fori_loop(0, nchunk, epilogue, 0)


def _mlstm(q, kt, v, og, gr, gc, gm, mh_gain):
    bsz, t, _ = q.shape
    hp = ML_HEADS // 2
    nv = ML_HEADS * ML_DV
    w2 = 2 * ML_DV
    gr = gr.reshape(bsz, 4, hp, 2, t).transpose(0, 2, 1, 3, 4).reshape(bsz, hp, 8, t)
    gc = gc.reshape(bsz, t, 4, hp, 2).transpose(0, 3, 1, 2, 4).reshape(bsz, hp, t, 8)
    gm = gm.reshape(bsz, t, 4, hp, 2)[:, :, 0::2].transpose(0, 3, 1, 2, 4).reshape(bsz, hp, t, 4)
    return pl.pallas_call(
        _mlstm_kernel,
        grid=(bsz, hp),
        in_specs=[pl.BlockSpec((1, t, LANES), lambda b, h: (b, 0, h)),
                  pl.BlockSpec((1, LANES, t), lambda b, h: (b, h, 0)),
                  pl.BlockSpec((1, t, w2), lambda b, h: (b, 0, h)),
                  pl.BlockSpec((1, t, w2), lambda b, h: (b, 0, h)),
                  pl.BlockSpec((1, 1, 8, t), lambda b, h: (b, h, 0, 0)),
                  pl.BlockSpec((1, 1, t, 8), lambda b, h: (b, h, 0, 0)),
                  pl.BlockSpec((1, 1, t, 4), lambda b, h: (b, h, 0, 0)),
                  pl.BlockSpec((1, w2), lambda b, h: (0, h))],
        out_specs=pl.BlockSpec((1, t, w2), lambda b, h: (b, 0, h)),
        out_shape=jax.ShapeDtypeStruct((bsz, t, nv), BF16),
        scratch_shapes=[pltpu.VMEM((t, w2), F32), pltpu.VMEM((t, w2), F32),
                        pltpu.VMEM((4, LANES, w2), F32)],
        compiler_params=_params("arbitrary", "arbitrary"),
        name="mlstm",
    )(q, kt, v, og, gr, gc, gm, mh_gain.reshape(1, nv))


def _moe_pre_kernel(y_ref, wo_ref, x_ref, g1_ref, g_ref, sh_ref, sc_ref, whi_ref, wlo_ref, br_ref,
                    s_ref, h_ref, id_ref, wt_ref, rk_ref, cnt_ref):
    step = pl.program_id(0) * pl.num_programs(1) + pl.program_id(1)

    @pl.when(step == 0)
    def _():
        cnt_ref[...] = jnp.zeros_like(cnt_ref)

    ti = lax.broadcasted_iota(I32, (TM, TM), 0)
    si = lax.broadcasted_iota(I32, (TM, TM), 1)
    before = (ti < si).astype(BF16)
    whi = whi_ref[...]
    for bb in range(x_ref.shape[0]):
        x = x_ref[bb] + g1_ref[bb] * _dot(y_ref[bb], wo_ref[...])
        s_ref[bb] = x
        h = _norm_mod(x, g_ref[...], sh_ref[bb], sc_ref[bb])
        h_ref[bb] = _pack_pairs(h)
        hhi = h.astype(BF16)
        hlo = (h - hhi.astype(F32)).astype(BF16)
        logits = _dot_nt(whi, hhi) + _dot_nt(whi, hlo) + _dot_nt(wlo_ref[...], hhi) + br_ref[...]
        ne = logits.shape[0]
        eidx = lax.broadcasted_iota(I32, logits.shape, 0)
        sels, vals, ids = [], [], []
        l = logits
        for _ in range(TOP_K):
            mval = jnp.max(l, axis=0, keepdims=True)
            idx = jnp.min(jnp.where(l == mval, eidx, ne), axis=0, keepdims=True)
            sel = eidx == idx
            l = jnp.where(sel, -jnp.inf, l)
            sels.append(sel)
            vals.append(mval)
            ids.append(idx)
        ex = [jnp.exp(v - vals[0]) for v in vals]
        tot = ex[0] + ex[1] + ex[2] + ex[3]
        member = sels[0] | sels[1] | sels[2] | sels[3]
        rank_e = _dot(member.astype(BF16), before) + cnt_ref[:, 0:1]
        for kk in range(TOP_K):
            id_ref[bb, kk:kk + 1, :] = ids[kk]
            wt_ref[bb, kk:kk + 1, :] = ex[kk] / tot
            rk_ref[bb, kk:kk + 1, :] = jnp.sum(jnp.where(sels[kk], rank_e, 0.0), axis=0, keepdims=True).astype(I32)
        cnt_ref[...] = cnt_ref[...] + jnp.sum(member.astype(F32), axis=1, keepdims=True)


def _moe_pre(y, w_o, s, gain, mod, w_router, b_router, skip):
    bsz, t, d = s.shape
    nt = t // TM - skip
    nb = PRE_NB if bsz % PRE_NB == 0 else 1
    wt = w_router.T
    whi = wt.astype(BF16)
    wlo = (wt - whi.astype(F32)).astype(BF16)
    tok = lambda dt: jax.ShapeDtypeStruct((bsz, TOP_K, nt * TM), dt)
    tok_spec = pl.BlockSpec((nb, TOP_K, TM), lambda b, i: (b, 0, i))
    tile = lambda w: pl.BlockSpec((nb, TM, w), lambda b, i: (b, i + skip, 0))
    const = lambda a: pl.BlockSpec(a.shape, lambda b, i: (0, 0))
    mods = lambda which: pl.BlockSpec((nb, None, 1, d),
                                      lambda b, i: (jnp.where(i + skip == 0, bsz // nb, b), which, 0, 0))
    return pl.pallas_call(
        _moe_pre_kernel,
        grid=(bsz // nb, nt),
        in_specs=[tile(y.shape[-1]), const(w_o), tile(d), mods(2),
                  pl.BlockSpec((1, d), lambda b, i: (0, 0)), mods(3), mods(4),
                  const(whi), const(wlo), pl.BlockSpec((N_EXPERTS, 1), lambda b, i: (0, 0))],
        out_specs=[tile(d), pl.BlockSpec((nb, TM, d // 2), lambda b, i: (b, i, 0)), tok_spec, tok_spec, tok_spec,
                   pl.BlockSpec((N_EXPERTS, LANES), lambda b, i: (0, 0))],
        out_shape=[jax.ShapeDtypeStruct(s.shape, F32),
                   jax.ShapeDtypeStruct((bsz, nt * TM, d // 2), I32), tok(I32), tok(F32), tok(I32),
                   jax.ShapeDtypeStruct((N_EXPERTS, LANES), F32)],
        input_output_aliases={2: 0},
        compiler_params=_params("arbitrary", "arbitrary"),
        name="moe_pre",
    )(y, w_o, s, mod, gain.reshape(1, d), mod, mod, whi, wlo, b_router.reshape(N_EXPERTS, 1))


def _sc_mesh():
    return plsc.VectorSubcoreMesh(core_axis_name="c", subcore_axis_name="s",
                                  num_cores=SC_CORES, num_subcores=SC_SUBCORES)


def _sc_worker_chunks(nchunks, body):
    n_workers = SC_CORES * SC_SUBCORES
    per_worker = -(-nchunks // n_workers)
    wid = lax.axis_index("s") * SC_CORES + lax.axis_index("c")

    @pl.loop(0, per_worker)
    def _(i):
        c = wid * per_worker + i

        @pl.when(c < nchunks)
        def _():
            body(c)


def _dispatch(hp, idx3, n_slots):
    n, w = hp.shape
    nchunks, kk, ch = idx3.shape

    @functools.partial(pl.kernel, mesh=_sc_mesh(), out_type=jax.ShapeDtypeStruct((n_slots, w), hp.dtype),
                       scratch_types=[pltpu.VMEM((kk, ch), I32), pltpu.VMEM((ch, w), hp.dtype)],
                       name="moe_dispatch")
    def scatter(src_hbm, idx_hbm, out_hbm, idx_v, rows_v):
        def body(c):
            pltpu.sync_copy(idx_hbm.at[c], idx_v)
            pltpu.sync_copy(src_hbm.at[pl.ds(c * ch, ch)], rows_v)
            for j in range(kk):
                pltpu.sync_copy(rows_v, out_hbm.at[idx_v.at[j]])
        _sc_worker_chunks(nchunks, body)

    return scatter(hp, idx3)


def _gather(ys, idx3):
    nchunks, kk, ch = idx3.shape
    w = ys.shape[1]

    @functools.partial(pl.kernel, mesh=_sc_mesh(), out_type=jax.ShapeDtypeStruct((kk, nchunks * ch, w), ys.dtype),
                       scratch_types=[pltpu.VMEM((kk, ch), I32), pltpu.VMEM((ch, w), ys.dtype)],
                       name="moe_gather")
    def gather(tab_hbm, idx_hbm, out_hbm, idx_v, rows_v):
        def body(c):
            pltpu.sync_copy(idx_hbm.at[c], idx_v)
            for j in range(kk):
                pltpu.sync_copy(tab_hbm.at[idx_v.at[j]], rows_v)
                pltpu.sync_copy(rows_v, out_hbm.at[j, pl.ds(c * ch, ch)])
        _sc_worker_chunks(nchunks, body)

    return gather(ys, idx3)


def _pack_pairs(x):
    w = x.shape[1] // 2
    bits = lax.bitcast_convert_type(x.astype(BF16).astype(F32), jnp.uint32)
    return lax.bitcast_convert_type((bits[:, :w] >> 16) | (bits[:, w:] & jnp.uint32(0xFFFF0000)), I32)


def _unpack_pairs(p):
    u = lax.bitcast_convert_type(p, jnp.uint32)
    lo = lax.bitcast_convert_type(u << 16, F32)
    hi = lax.bitcast_convert_type(u & jnp.uint32(0xFFFF0000), F32)
    return jnp.concatenate([lo, hi], axis=1)


def _ffn_kernel(te_ref, nv_ref, x_ref, w1_ref, b1_ref, w2_ref, b2_ref, y_ref, w1b_ref, w2b_ref):
    i = pl.program_id(0)

    @pl.when(jnp.logical_or(i == 0, te_ref[i] != te_ref[jnp.maximum(i - 1, 0)]))
    def _():
        w1b_ref[...] = w1_ref[0].astype(BF16)
        w2b_ref[...] = w2_ref[0].astype(BF16)

    nv = nv_ref[i]
    half = x_ref.shape[0] // 2

    def expert(rows):
        dff = w2_ref.shape[1]
        row = lax.broadcasted_iota(I32, (rows, 1), 0)
        x = _unpack_pairs(jnp.where(row < nv, x_ref[0:rows, :], 0)).astype(BF16)
        u = _dot(x, w1b_ref[...]) + b1_ref[0]
        glu = jnp.minimum(u[:, :dff], SWIGLU_LIMIT)
        lin = jnp.clip(u[:, dff:], -SWIGLU_LIMIT, SWIGLU_LIMIT)
        act = glu * jax.nn.sigmoid(SWIGLU_ALPHA * glu) * (lin + 1.0)
        y_ref[0:rows, :] = _pack_pairs(_dot(act.astype(BF16), w2b_ref[...]) + b2_ref[0])

    @pl.when(nv > half)
    def _():
        expert(2 * half)

    @pl.when(jnp.logical_and(nv > 0, nv <= half))
    def _():
        expert(half)
        y_ref[half:, :] = jnp.zeros((half, y_ref.shape[1]), y_ref.dtype)

    @pl.when(nv <= 0)
    def _():
        y_ref[...] = jnp.zeros_like(y_ref)


def _ffn(xs, tile_expert, n_valid, layer, w1, b1, w2, b2):
    ns, wd = xs.shape
    depth, ne, d, n1 = w1.shape
    dff = w2.shape[2]
    row = lambda i, te, nv: (i, 0)
    exp4 = lambda i, te, nv: (layer, te[i], 0, 0)
    return pl.pallas_call(
        _ffn_kernel,
        grid_spec=pltpu.PrefetchScalarGridSpec(
            num_scalar_prefetch=2,
            grid=(ns // TMX,),
            in_specs=[pl.BlockSpec((TMX, wd), row),
                      pl.BlockSpec((None, 1, d, n1), exp4), pl.BlockSpec((None, 1, 1, n1), exp4),
                      pl.BlockSpec((None, 1, dff, d), exp4), pl.BlockSpec((None, 1, 1, d), exp4)],
            out_specs=pl.BlockSpec((TMX, wd), row),
            scratch_shapes=[pltpu.VMEM((d, n1), BF16), pltpu.VMEM((dff, d), BF16)]),
        out_shape=jax.ShapeDtypeStruct((ns, wd), I32),
        compiler_params=pltpu.CompilerParams(dimension_semantics=("arbitrary",), vmem_limit_bytes=FFN_VMEM_LIMIT),
        name="moe_ffn",
    )(tile_expert, n_valid, xs, w1, b1.reshape(depth, ne, 1, n1), w2, b2.reshape(depth, ne, 1, d))


def _combine_kernel(wt_ref, s_ref, g_ref, yk_ref, o_ref):
    wt = wt_ref[...]
    f = wt[:, 0:1] * _unpack_pairs(yk_ref[0])
    for kk in range(1, TOP_K):
        f = f + wt[:, kk:kk + 1] * _unpack_pairs(yk_ref[kk])
    o_ref[0] = s_ref[0] + g_ref[...] * f


def _combine(yk, wts, s, mod, which, skip):
    bsz, t, d = s.shape
    nt = t // TM - skip
    return pl.pallas_call(
        _combine_kernel,
        grid=(bsz, nt),
        in_specs=[pl.BlockSpec((TM, TOP_K), lambda b, i: (b * nt + i, 0)),
                  pl.BlockSpec((1, TM, d), lambda b, i: (b, i + skip, 0)),
                  _mod_spec_d(bsz, which, d, skip),
                  pl.BlockSpec((TOP_K, TM, yk.shape[-1]), lambda b, i: (0, b * nt + i, 0))],
        out_specs=pl.BlockSpec((1, TM, d), lambda b, i: (b, i, 0)),
        out_shape=jax.ShapeDtypeStruct((bsz, nt * TM, d), F32),
        input_output_aliases={} if skip else {1: 0},
        compiler_params=_params("arbitrary", "arbitrary"),
        name="moe_combine",
    )(wts, s, mod, yk)


def _moe(y, w_o, s, gain, mod, layer, w_router, b_router, w1, b1, w2, b2, skip=0):
    bsz, t, d = s.shape
    n = bsz * (t - skip * TM)
    assert n % SC_CHUNK == 0
    s, hp, ids, wts, ranks, counts = _moe_pre(y, w_o, s, gain, mod, w_router, b_router, skip)
    cnt = counts[:, 0].astype(I32)
    padded = ((cnt + TMX - 1) // TMX) * TMX
    ends = jnp.cumsum(padded)
    offs = ends - padded
    n_tiles = (TOP_K * n + N_EXPERTS * (TMX - 1) + TMX - 1) // TMX
    tile_start = jnp.arange(n_tiles, dtype=I32) * TMX
    te = jnp.sum(tile_start[:, None] >= ends[None, :], axis=1).astype(I32)
    te_last = jnp.max(jnp.where(cnt > 0, jnp.arange(N_EXPERTS, dtype=I32), 0))
    te = jnp.minimum(te, te_last)
    group_end = jnp.sum(jnp.where(te[:, None] == jnp.arange(N_EXPERTS, dtype=I32), offs + cnt, 0), axis=-1)
    n_valid = jnp.where(tile_start < ends[-1], jnp.clip(group_end - tile_start, 0, TMX), 0).astype(I32)
    slots = ranks + jnp.sum(jnp.where(ids[..., None] == jnp.arange(N_EXPERTS, dtype=I32), offs, 0), axis=-1)
    idx3 = slots.reshape(bsz, TOP_K, -1, SC_CHUNK).transpose(0, 2, 1, 3).reshape(n // SC_CHUNK, TOP_K, SC_CHUNK)
    xs = _dispatch(hp.reshape(n, d // 2), idx3, n_tiles * TMX)
    ys = _ffn(xs, te, n_valid, layer, w1, b1, w2, b2)
    return _combine(_gather(ys, idx3), wts.transpose(0, 2, 1).reshape(n, TOP_K), s, mod, 5, skip)


def kernel(x, c, ctx, c_ctx, w_ada, b_ada, g_norm_mix, g_norm_ffn, na_w_qkv, na_q_gain, na_k_gain, na_rpb,
           na_w_o, ml_w_in, ml_b_gate, ml_mh_gain, ml_w_o, moe_w_router, moe_b_router, moe_w1, moe_b1,
           moe_w2, moe_b2):
    bsz, seq, d = x.shape
    n_ctx = ctx.shape[1]
    depth = w_ada.shape[0]
    rows = seq // GRID_W
    n_mod_rows = ((bsz + PRE_NB + 7) // 8) * 8
    cc = jnp.concatenate([c, jnp.tile(c_ctx[None, :], (PRE_NB, 1)),
                          jnp.zeros((n_mod_rows - bsz - PRE_NB, d), F32)], axis=0)
    mod_all = _ada(cc, w_ada, b_ada).reshape(depth, n_mod_rows, 6, 1, d)
    s = jnp.concatenate([ctx, x], axis=1)
    for i in range(depth):
        j = i // 2
        mod = mod_all[i]
        if i % 2 == 0:
            qkv = _na_qkv(s, g_norm_mix[i], mod, na_w_qkv[j].astype(BF16))
            y = _na_attn(qkv, _na_bias(na_rpb[j], rows), na_q_gain[j], na_k_gain[j], n_ctx)
            w_o = na_w_o[j]
        else:
            q, kt, v, og, gr, gc, gm = _ml_proj(s, g_norm_mix[i], mod, ml_w_in[j], ml_b_gate[j], n_ctx)
            y = _mlstm(q, kt, v, og, gr, gc, gm, ml_mh_gain[j])
            w_o = ml_w_o[j]
        skip = n_ctx // TM if i == depth - 1 else 0
        s = _moe(y, w_o.astype(BF16), s, g_norm_ffn[i], mod, i, moe_w_router[i], moe_b_router[i],
                 moe_w1, moe_b1, moe_w2, moe_b2, skip)
    return s
```

```python
import functools
import math

import numpy as np
import jax
import jax.numpy as jnp
from jax import lax
from jax.experimental import pallas as pl
from jax.experimental.pallas import tpu as pltpu
from jax.experimental.pallas import tpu_sc as plsc

F32 = jnp.float32
BF16 = jnp.bfloat16
I32 = jnp.int32

EPS = 1e-6
GRID_W = 64
NA_HEADS = 16
NA_HEAD_DIM = 64
NA_KR = 8
NA_KC = 16
NA_QROWS = 4
NA_KROWS = NA_QROWS + NA_KR - 1
NA_NB = 4
PRE_NB = 2
ML_HEADS = 8
ML_DV = 128
ML_DQK = 64
ML_CHUNK = 256
ML_GATE_CAP = 15.0
ROPE_THETA = 10000.0
N_EXPERTS = 32
TOP_K = 4
SWIGLU_LIMIT = 7.0
SWIGLU_ALPHA = 1.702

TM = 256
TMX = 512
LANES = 128
NEG = -1e30
VMEM_LIMIT = 48 * 1024 * 1024
SC_CORES = 2
SC_SUBCORES = 16
SC_CHUNK = 128
FFN_VMEM_LIMIT =56 * 1024 * 1024


def _dot(a, b):
    return jnp.dot(a, b, preferred_element_type=F32)


def _dot_nt(a, b):
    return lax.dot_general(a, b, (((1,), (1,)), ((), ())), preferred_element_type=F32)


def _split3(x):
    hi = x.astype(BF16)
    r1 = x - hi.astype(F32)
    mid = r1.astype(BF16)
    lo = (r1 - mid.astype(F32)).astype(BF16)
    return hi, mid, lo


def _params(*sem):
    return pltpu.CompilerParams(dimension_semantics=sem, vmem_limit_bytes=VMEM_LIMIT)


def _norm_mod(x, gain, shift, scale):
    r = lax.rsqrt(jnp.mean(x * x, axis=-1, keepdims=True) + EPS)
    return (x * r * gain) * (1.0 + scale) + shift


def _mod_spec_d(n_batch, which, d, skip=0):
    return pl.BlockSpec((None, None, 1, d), lambda b, t: (jnp.where(t + skip == 0, n_batch, b), which, 0, 0))


def _ada_kernel(c_ref, w_ref, b_ref, o_ref):
    c = c_ref[...]
    s = c * jax.nn.sigmoid(c)
    o_ref[0] = _dot(s.astype(BF16), w_ref[0].astype(BF16)) + b_ref[0]


def _ada(cc, w_ada, b_ada):
    depth, d, n = w_ada.shape
    bn = n // 4
    return pl.pallas_call(
        _ada_kernel,
        grid=(depth, n // bn),
        in_specs=[pl.BlockSpec((cc.shape[0], d), lambda l, j: (0, 0)),
                  pl.BlockSpec((1, d, bn), lambda l, j: (l, 0, j)),
                  pl.BlockSpec((1, 1, bn), lambda l, j: (l, 0, j))],
        out_specs=pl.BlockSpec((1, cc.shape[0], bn), lambda l, j: (l, 0, j)),
        out_shape=jax.ShapeDtypeStruct((depth, cc.shape[0], n), F32),
        compiler_params=_params("arbitrary", "arbitrary"),
        name="ada",
    )(cc, w_ada, b_ada.reshape(depth, 1, n))


def _na_qkv_kernel(x_ref, g_ref, sh_ref, sc_ref, w_ref, o_ref):
    h = _norm_mod(x_ref[0], g_ref[...], sh_ref[...], sc_ref[...])
    o_ref[0] = _dot(h.astype(BF16), w_ref[...]).astype(BF16)


def _na_qkv(s, gain, mod, w):
    bsz, t, d = s.shape
    n = w.shape[1]
    return pl.pallas_call(
        _na_qkv_kernel,
        grid=(bsz, t // TM),
        in_specs=[pl.BlockSpec((1, TM, d), lambda b, i: (b, i, 0)),
                  pl.BlockSpec((1, d), lambda b, i: (0, 0)),
                  _mod_spec_d(bsz, 0, d), _mod_spec_d(bsz, 1, d),
                  pl.BlockSpec((d, n), lambda b, i: (0, 0))],
        out_specs=pl.BlockSpec((1, TM, n), lambda b, i: (b, i, 0)),
        out_shape=jax.ShapeDtypeStruct((bsz, t, n), BF16),
        compiler_params=_params("arbitrary", "arbitrary"),
        name="na_qkv",
    )(s, gain.reshape(1, d), mod, mod, w)


def _na_bias(rpb, rows):
    nblk = rows // NA_QROWS
    kmax = rows - NA_KROWS
    nrr, ncc = 2 * NA_KR - 1, 2 * NA_KC - 1
    qc = np.arange(GRID_W)[:, None]
    kc = np.arange(GRID_W)[None, :]
    c0 = np.clip(qc - NA_KC // 2, 0, GRID_W - NA_KC)
    col_ok = (kc >= c0) & (kc < c0 + NA_KC)
    col_pick = np.eye(ncc)[np.clip(kc - qc + NA_KC - 1, 0, ncc - 1)]
    row_pick, valid = [], []
    for blk in (0, 1, nblk - 1):
        rs = blk * NA_QROWS
        ks = min(max(rs - NA_KR // 2, 0), kmax)
        qr = rs + np.arange(NA_QROWS)[:, None]
        kr = ks + np.arange(NA_KROWS)[None, :]
        r0 = np.clip(qr - NA_KR // 2, 0, rows - NA_KR)
        row_ok = (kr >= r0) & (kr < r0 + NA_KR)
        row_pick.append(np.eye(nrr)[np.clip(kr - qr + NA_KR - 1, 0, nrr - 1)])
        valid.append(row_ok[:, None, :, None] & col_ok[None, :, None, :])
    bias = jnp.einsum('vrka,pxab,qcb->pvxrqkc', jnp.asarray(np.stack(row_pick), F32),
                      rpb.reshape(NA_HEADS // 2, 2, nrr, ncc), jnp.asarray(col_pick, F32),
                      precision=lax.Precision.HIGHEST)
    bias = jnp.where(np.stack(valid)[None, :, None], bias, NEG)
    return bias.reshape(NA_HEADS // 2, 3, 2, NA_QROWS * GRID_W, NA_KROWS * GRID_W).astype(F32)


def _na_attn_kernel(q_ref, k_ref, v_ref, bias_ref, qg_ref, kg_ref, o_ref, kn_ref, v1_ref, *, n_ctx, rows):
    t = pl.program_id(2)
    first = lax.broadcasted_iota(I32, (1, LANES), 1) < NA_HEAD_DIM
    nk = NA_KROWS * GRID_W

    def head_rms(z, gain):
        z2 = z * z
        sa = jnp.sum(jnp.where(first, z2, 0.0), axis=-1, keepdims=True)
        sb = jnp.sum(jnp.where(first, 0.0, z2), axis=-1, keepdims=True)
        r = jnp.where(first, lax.rsqrt(sa / NA_HEAD_DIM + EPS), lax.rsqrt(sb / NA_HEAD_DIM + EPS))
        return z * r * gain

    nb = q_ref.shape[0]

    @pl.when(t == 0)
    def _():
        def body(i, carry):
            off = pl.multiple_of(i * TM, TM)
            for bb in range(nb):
                kn_ref[bb, pl.ds(off, TM), :] = head_rms(k_ref[bb, pl.ds(off, TM), :].astype(F32),
                                                         kg_ref[...]).astype(BF16)
                v = v_ref[bb, pl.ds(off, TM), :]
                v1_ref[bb, pl.ds(off, TM), :] = jnp.concatenate([v, jnp.ones_like(v)], axis=1)
            return carry
        lax.fori_loop(0, k_ref.shape[1] // TM, body, 0)

    def stacked_q(bb):
        qn = (head_rms(q_ref[bb].astype(F32), qg_ref[...]) * (NA_HEAD_DIM ** -0.5)).astype(BF16)
        zero = jnp.zeros_like(qn)
        return jnp.concatenate([jnp.where(first, qn, zero), jnp.where(first, zero, qn)], axis=0)

    def finish(bb, o):
        oa, ob = o[:TM], o[TM:]
        o_ref[bb] = jnp.where(first, oa[:, :LANES] / oa[:, LANES:],
                              ob[:, :LANES] / ob[:, LANES:]).astype(o_ref.dtype)

    @pl.when(t == 0)
    def _():
        for bb in range(nb):
            s = _dot_nt(stacked_q(bb), kn_ref[bb, 0:n_ctx, :])
            m = jnp.max(s, axis=-1, keepdims=True)
            finish(bb, _dot(jnp.exp((s - m).astype(BF16)), v1_ref[bb, 0:n_ctx, :]))

    @pl.when(t > 0)
    def _():
        blk = t - 1
        nblk = rows // NA_QROWS
        ks = jnp.clip(blk * NA_QROWS - NA_KR // 2, 0, rows - NA_KROWS)
        start = pl.multiple_of(n_ctx + ks * GRID_W, GRID_W)
        kind = jnp.where(blk == 0, 0, jnp.where(blk == nblk - 1, 2, 1))
        bias = bias_ref[0, kind].reshape(2 * TM, nk)
        for bb in range(nb):
            q2 = stacked_q(bb)
            sw = _dot_nt(q2, kn_ref[bb, pl.ds(start, nk), :]) + bias
            sc = _dot_nt(q2, kn_ref[bb, 0:n_ctx, :])
            m = jnp.maximum(jnp.max(sw, axis=-1, keepdims=True), jnp.max(sc, axis=-1, keepdims=True))
            pw = jnp.exp((sw - m).astype(BF16))
            pc = jnp.exp((sc - m).astype(BF16))
            finish(bb, _dot(pw, v1_ref[bb, pl.ds(start, nk), :]) + _dot(pc, v1_ref[bb, 0:n_ctx, :]))


def _na_attn(qkv, bias, q_gain, k_gain, n_ctx):
    bsz, t, _ = qkv.shape
    d = NA_HEADS * NA_HEAD_DIM
    rows = (t - n_ctx) // GRID_W
    assert n_ctx == TM and NA_QROWS * GRID_W == TM and rows % NA_QROWS == 0 and rows >= NA_KROWS
    hp = NA_HEADS // 2
    nq, nk = bias.shape[-2:]
    gq = jnp.tile(q_gain, 2).reshape(1, LANES)
    gk = jnp.tile(k_gain, 2).reshape(1, LANES)
    nb = NA_NB if bsz % NA_NB == 0 else 1
    return pl.pallas_call(
        functools.partial(_na_attn_kernel, n_ctx=n_ctx, rows=rows),
        grid=(hp, bsz // nb, t // TM),
        in_specs=[pl.BlockSpec((nb, TM, LANES), lambda h, b, i: (b, i, h)),
                  pl.BlockSpec((nb, t, LANES), lambda h, b, i: (b, 0, hp + h)),
                  pl.BlockSpec((nb, t, LANES), lambda h, b, i: (b, 0, 2 * hp + h)),
                  pl.BlockSpec((1, 3, 2, nq, nk), lambda h, b, i: (h, 0, 0, 0, 0)),
                  pl.BlockSpec((1, LANES), lambda h, b, i: (0, 0)),
                  pl.BlockSpec((1, LANES), lambda h, b, i: (0, 0))],
        out_specs=pl.BlockSpec((nb, TM, LANES), lambda h, b, i: (b, i, h)),
        out_shape=jax.ShapeDtypeStruct((bsz, t, d), BF16),
        scratch_shapes=[pltpu.VMEM((nb, t, LANES), BF16), pltpu.VMEM((nb, t, 2 * LANES), BF16)],
        compiler_params=_params("arbitrary", "arbitrary", "arbitrary"),
        name="na_attn",
    )(qkv, qkv, qkv, bias, gq, gk)


def _rope_tables(n_ctx, seq):
    lane = np.arange(LANES)
    dd = lane % ML_DQK
    grp = dd // (ML_DQK // 2)
    j = dd % (ML_DQK // 2)
    n = ML_DQK // 4
    inv = ROPE_THETA ** (-(j % n).astype(np.float64) / n)
    pos = np.arange(seq)
    p = np.where(grp[None, :] == 0, (pos // GRID_W)[:, None], (pos % GRID_W)[:, None]).astype(np.float64)
    ang = p * inv[None, :]
    cos = np.cos(ang)
    sin = np.where(j[None, :] < n, -np.sin(ang), np.sin(ang))
    cos = np.concatenate([np.ones((n_ctx, LANES)), cos], axis=0)
    sin = np.concatenate([np.zeros((n_ctx, LANES)), sin], axis=0)
    return jnp.asarray(cos, F32), jnp.asarray(sin, F32)


def _ml_proj_kernel(x_ref, g_ref, sh_ref, sc_ref, wm_ref, wg_ref, wgt_ref, bgr_ref, bgc_ref, cos_ref, sin_ref,
                    q_ref, kt_ref, v_ref, og_ref, gr_ref, gc_ref, gm_ref):
    h = _norm_mod(x_ref[0], g_ref[...], sh_ref[...], sc_ref[...]).astype(BF16)
    main = _dot(h, wm_ref[...])
    nq = q_ref.shape[-1]
    nv = v_ref.shape[-1]
    cos = cos_ref[...]
    sin = sin_ref[...]
    n = ML_DQK // 4
    lane = lax.broadcasted_iota(I32, (1, LANES), 1)
    low = (lane % (2 * n)) < n

    def rope(z):
        zr = jnp.where(low, pltpu.roll(z, LANES - n, 1), pltpu.roll(z, n, 1))
        return z * cos + zr * sin

    for j in range(nq // LANES):
        sl = slice(j * LANES, (j + 1) * LANES)
        q_ref[0, :, sl] = (rope(main[:, sl]) * (ML_DQK ** -0.5)).astype(BF16)
        kt_ref[0, sl, :] = rope(main[:, nq + j * LANES: nq + (j + 1) * LANES]).T.astype(BF16)
    v_ref[0] = main[:, 2 * nq: 2 * nq + nv].astype(BF16)
    og_ref[0] = main[:, 2 * nq + nv:].astype(BF16)

    nh = ML_HEADS
    ti = lax.broadcasted_iota(I32, (TM, TM), 0)
    si = lax.broadcasted_iota(I32, (TM, TM), 1)
    le = (si <= ti).astype(BF16)
    ge = (si >= ti).astype(BF16)

    def capped(pre):
        return ML_GATE_CAP * jnp.tanh(pre / ML_GATE_CAP)

    def exact_cum(tri, x, nt):
        parts = _split3(x)
        if nt:
            return sum(_dot_nt(p, tri) for p in parts)
        return sum(_dot(tri, p) for p in parts)

    gc = capped(_dot(h, wg_ref[...]) + bgr_ref[...])
    lf = jax.nn.log_sigmoid(gc)
    pre = exact_cum(le, lf, False)
    suf = exact_cum(ge, lf, False)
    cum = jnp.where(lane < 2 * nh, pre, suf)
    r = gc - pltpu.roll(cum, LANES - nh, 1)
    is_i = (lane // nh) % 2 == 0
    gc_ref[0] = jnp.where(is_i, r, cum)[:, :4 * nh]
    row = lax.broadcasted_iota(I32, (TM, 1), 0)
    pmax = r
    smax = r
    step = 1
    while step < TM:
        pmax = jnp.maximum(pmax, jnp.where(row >= step, pltpu.roll(pmax, step, 0), NEG))
        smax = jnp.maximum(smax, jnp.where(row < TM - step, pltpu.roll(smax, TM - step, 0), NEG))
        step *= 2
    gm_ref[0] = jnp.where(lane < 2 * nh, pmax, smax)[:, :4 * nh]

    gr = capped(_dot_nt(wgt_ref[...], h) + bgc_ref[...])
    lfr = jax.nn.log_sigmoid(gr)
    pre_r = exact_cum(le, lfr, True)
    suf_r = exact_cum(ge, lfr, True)
    gr_ref[0, 0 * nh:1 * nh, :] = gr[0 * nh:1 * nh] - pre_r[1 * nh:2 * nh]
    gr_ref[0, 1 * nh:2 * nh, :] = pre_r[1 * nh:2 * nh]
    gr_ref[0, 2 * nh:3 * nh, :] = gr[2 * nh:3 * nh] - suf_r[3 * nh:4 * nh]
    gr_ref[0, 3 * nh:4 * nh, :] = suf_r[3 * nh:4 * nh]


def _ml_proj(s, gain, mod, w_in, b_gate, n_ctx):
    bsz, t, d = s.shape
    nq = ML_HEADS * ML_DQK
    nv = ML_HEADS * ML_DV
    ng = 4 * ML_HEADS
    assert TM == ML_CHUNK
    wm = w_in[:, :2 * nq + nv + d].astype(BF16)
    wg = w_in[:, 2 * nq + nv + d:]
    wg_pad = jnp.pad(wg, ((0, 0), (0, LANES - ng))).astype(BF16)
    wgt = wg.T.astype(BF16)
    bgr = jnp.pad(b_gate, (0, LANES - ng)).reshape(1, LANES)
    bgc = b_gate.reshape(ng, 1)
    cos, sin = _rope_tables(n_ctx, t - n_ctx)
    tile = lambda n: pl.BlockSpec((1, TM, n), lambda b, i: (b, i, 0))
    const = lambda a: pl.BlockSpec(a.shape, lambda b, i: (0, 0))
    return pl.pallas_call(
        _ml_proj_kernel,
        grid=(bsz, t // TM),
        in_specs=[tile(d), pl.BlockSpec((1, d), lambda b, i: (0, 0)),
                  _mod_spec_d(bsz, 0, d), _mod_spec_d(bsz, 1, d),
                  const(wm), const(wg_pad), const(wgt), const(bgr), const(bgc),
                  pl.BlockSpec((TM, LANES), lambda b, i: (i, 0)),
                  pl.BlockSpec((TM, LANES), lambda b, i: (i, 0))],
        out_specs=[tile(nq), pl.BlockSpec((1, nq, TM), lambda b, i: (b, 0, i)), tile(nv), tile(d),
                   pl.BlockSpec((1, ng, TM), lambda b, i: (b, 0, i)),
                   tile(ng), tile(ng)],
        out_shape=[jax.ShapeDtypeStruct((bsz, t, nq), BF16), jax.ShapeDtypeStruct((bsz, nq, t), BF16),
                   jax.ShapeDtypeStruct((bsz, t, nv), BF16), jax.ShapeDtypeStruct((bsz, t, d), BF16),
                   jax.ShapeDtypeStruct((bsz, ng, t), F32), jax.ShapeDtypeStruct((bsz, t, ng), F32),
                   jax.ShapeDtypeStruct((bsz, t, ng), F32)],
        compiler_params=_params("arbitrary", "arbitrary"),
        name="ml_proj",
    )(s, gain.reshape(1, d), mod, mod, wm, wg_pad, wgt, bgr, bgc, cos, sin)


def _mlstm_kernel(q_ref, kt_ref, v_ref, og_ref, gr_ref, gc_ref, gm_ref, mg_ref, o_ref, hf_ref, hb_ref, c_ref):
    L = ML_CHUNK
    nchunk = q_ref.shape[1] // L
    first = lax.broadcasted_iota(I32, (1, LANES), 1) < ML_DQK
    first_row = lax.broadcasted_iota(I32, (LANES, 1), 0) < ML_DQK
    ti = lax.broadcasted_iota(I32, (L, L), 0)
    si = lax.broadcasted_iota(I32, (L, L), 1)
    masks = (si <= ti, si >= ti)
    ones_blk = jnp.ones((L, ML_DV), BF16)

    c_ref[...] = jnp.zeros_like(c_ref)

    def chunk(c, a, bw, m0):
        off = pl.multiple_of(c * L, L)
        q = q_ref[0, pl.ds(off, L), :]
        kt = kt_ref[0, :, pl.ds(off, L)]
        qa = jnp.where(first if a == 0 else jnp.logical_not(first), q, jnp.zeros_like(q))
        kta = jnp.where(first_row if a == 0 else jnp.logical_not(first_row), kt, jnp.zeros_like(kt))
        v = v_ref[0, pl.ds(off, L), a * ML_DV:(a + 1) * ML_DV]
        w = 2 if bw else 0
        r_row = gr_ref[0, 0, 2 * w + a: 2 * w + a + 1, pl.ds(off, L)]
        cum_row = gr_ref[0, 0, 2 * (w + 1) + a: 2 * (w + 1) + a + 1, pl.ds(off, L)]
        cum_col = gc_ref[0, 0, pl.ds(off, L), 2 * (w + 1) + a: 2 * (w + 1) + a + 1]
        tot = cum_row[:, 0:1] if bw else cum_row[:, L - 1:L]
        mask = masks[1 if bw else 0]
        mx = gm_ref[0, 0, pl.ds(off, L), w + a: w + a + 1]
        mm = jnp.broadcast_to(jnp.maximum(m0, mx), (L, LANES))
        cum_b = jnp.broadcast_to(cum_col, (L, LANES))
        e = jnp.exp(jnp.where(mask, r_row - jnp.concatenate([mm] * (L // LANES), axis=1), NEG))
        s = (_dot(qa, kt) * e).astype(BF16)
        v2 = jnp.concatenate([v, ones_blk], axis=1)
        idx = 2 * a + (1 if bw else 0)
        cst = c_ref[idx]
        w_inter = jnp.exp(m0 - mm)
        both = _dot(s, v2) + jnp.concatenate([w_inter, w_inter], axis=1) * _dot(qa, cst.astype(BF16))
        num = both[:, :ML_DV]
        den = both[:, ML_DV:]
        hout = num / jnp.maximum(jnp.abs(den), jnp.exp(-cum_b - mm))
        dst = hb_ref if bw else hf_ref
        dst[pl.ds(off, L), a * ML_DV:(a + 1) * ML_DV] = hout
        m_loc = tot + jnp.max(r_row, axis=1, keepdims=True)
        kte = (kta.astype(F32) * jnp.exp(tot + r_row - m_loc)).astype(BF16)
        u = _dot(kte, v2)
        m_new = jnp.maximum(tot + m0, m_loc)
        c_ref[idx] = jnp.exp(tot + m0 - m_new) * cst + jnp.exp(m_loc - m_new) * u
        return m_new

    def body(i, ms):
        cb = jnp.where(i == 0, 0, nchunk - i)
        return (chunk(i, 0, False, ms[0]), chunk(i, 1, False, ms[1]),
                chunk(cb, 0, True, ms[2]), chunk(cb, 1, True, ms[3]))

    z = jnp.zeros((1, 1), F32)
    lax.fori_loop(0, nchunk, body, (z, z, z, z))

    def epilogue(i, carry):
        off = pl.multiple_of(i * L, L)
        for a in range(2):
            sl = slice(a * ML_DV, (a + 1) * ML_DV)
            hs = hf_ref[pl.ds(off, L), sl] + hb_ref[pl.ds(off, L), sl]
            hn = hs * lax.rsqrt(jnp.mean(hs * hs, axis=-1, keepdims=True) + EPS) * mg_ref[:, sl]
            o_ref[0, pl.ds(off, L), sl] = (hn * jax.nn.sigmoid(og_ref[0, pl.ds(off, L), sl].astype(F32))).astype(BF16)
        return carry

    lax.fori_loop(0, nchunk, epilogue, 0)


def _mlstm(q, kt, v, og, gr, gc, gm, mh_gain):
    bsz, t, _ = q.shape
    hp = ML_HEADS // 2
    nv = ML_HEADS * ML_DV
    w2 = 2 * ML_DV
    gr = gr.reshape(bsz, 4, hp, 2, t).transpose(0, 2, 1, 3, 4).reshape(bsz, hp, 8, t)
    gc = gc.reshape(bsz, t, 4, hp, 2).transpose(0, 3, 1, 2, 4).reshape(bsz, hp, t, 8)
    gm = gm.reshape(bsz, t, 4, hp, 2)[:, :, 0::2].transpose(0, 3, 1, 2, 4).reshape(bsz, hp, t, 4)
    return pl.pallas_call(
        _mlstm_kernel,
        grid=(bsz, hp),
        in_specs=[pl.BlockSpec((1, t, LANES), lambda b, h: (b, 0, h)),
                  pl.BlockSpec((1, LANES, t), lambda b, h: (b, h, 0)),
                  pl.BlockSpec((1, t, w2), lambda b, h: (b, 0, h)),
                  pl.BlockSpec((1, t, w2), lambda b, h: (b, 0, h)),
                  pl.BlockSpec((1, 1, 8, t), lambda b, h: (b, h, 0, 0)),
                  pl.BlockSpec((1, 1, t, 8), lambda b, h: (b, h, 0, 0)),
                  pl.BlockSpec((1, 1, t, 4), lambda b, h: (b, h, 0, 0)),
                  pl.BlockSpec((1, w2), lambda b, h: (0, h))],
        out_specs=pl.BlockSpec((1, t, w2), lambda b, h: (b, 0, h)),
        out_shape=jax.ShapeDtypeStruct((bsz, t, nv), BF16),
        scratch_shapes=[pltpu.VMEM((t, w2), F32), pltpu.VMEM((t, w2), F32),
                        pltpu.VMEM((4, LANES, w2), F32)],
        compiler_params=_params("arbitrary", "arbitrary"),
        name="mlstm",
    )(q, kt, v, og, gr, gc, gm, mh_gain.reshape(1, nv))


def _moe_pre_kernel(y_ref, wo_ref, x_ref, g1_ref, g_ref, sh_ref, sc_ref, wr_ref, br_ref,
                    s_ref, h_ref, id_ref, wt_ref, rk_ref, cnt_ref):
    step = pl.program_id(0) * pl.num_programs(1) + pl.program_id(1)

    @pl.when(step == 0)
    def _():
        cnt_ref[...] = jnp.zeros_like(cnt_ref)

    ti = lax.broadcasted_iota(I32, (TM, TM), 0)
    si = lax.broadcasted_iota(I32, (TM, TM), 1)
    before = (ti < si).astype(BF16)
    ne = br_ref.shape[0]
    wcat = wr_ref[...]
    for bb in range(x_ref.shape[0]):
        x = x_ref[bb] + g1_ref[bb] * _dot(y_ref[bb], wo_ref[...])
        s_ref[bb] = x
        h = _norm_mod(x, g_ref[...], sh_ref[bb], sc_ref[bb])
        h_ref[bb] = _pack_pairs(h)
        hhi = h.astype(BF16)
        hlo = (h - hhi.astype(F32)).astype(BF16)
        both = _dot_nt(wcat, hhi)
        logits = both[:ne] + both[ne:] + _dot_nt(wcat[:ne], hlo) + br_ref[...]
        eidx = lax.broadcasted_iota(I32, logits.shape, 0)
        sels, vals, ids = [], [], []
        l = logits
        for _ in range(TOP_K):
            mval = jnp.max(l, axis=0, keepdims=True)
            idx = jnp.min(jnp.where(l == mval, eidx, ne), axis=0, keepdims=True)
            sel = eidx == idx
            l = jnp.where(sel, -jnp.inf, l)
            sels.append(sel)
            vals.append(mval)
            ids.append(idx)
        ex = [jnp.exp(v - vals[0]) for v in vals]
        tot = ex[0] + ex[1] + ex[2] + ex[3]
        member = sels[0] | sels[1] | sels[2] | sels[3]
        rank_e = _dot(member.astype(BF16), before) + cnt_ref[:, 0:1]
        for kk in range(TOP_K):
            id_ref[bb, kk:kk + 1, :] = ids[kk]
            wt_ref[bb, kk:kk + 1, :] = ex[kk] / tot
            rk_ref[bb, kk:kk + 1, :] = jnp.sum(jnp.where(sels[kk], rank_e, 0.0), axis=0, keepdims=True).astype(I32)
        cnt_ref[...] = cnt_ref[...] + jnp.sum(member.astype(F32), axis=1, keepdims=True)


def _moe_pre(y, w_o, s, gain, mod, w_router, b_router, skip):
    bsz, t, d = s.shape
    nt = t // TM - skip
    nb = PRE_NB if bsz % PRE_NB == 0 else 1
    wt = w_router.T
    whi = wt.astype(BF16)
    wcat = jnp.concatenate([whi, (wt - whi.astype(F32)).astype(BF16)], axis=0)
    tok = lambda dt: jax.ShapeDtypeStruct((bsz, TOP_K, nt * TM), dt)
    tok_spec = pl.BlockSpec((nb, TOP_K, TM), lambda b, i: (b, 0, i))
    tile = lambda w: pl.BlockSpec((nb, TM, w), lambda b, i: (b, i + skip, 0))
    const = lambda a: pl.BlockSpec(a.shape, lambda b, i: (0, 0))
    mods = lambda which: pl.BlockSpec((nb, None, 1, d),
                                      lambda b, i: (jnp.where(i + skip == 0, bsz // nb, b), which, 0, 0))
    return pl.pallas_call(
        _moe_pre_kernel,
        grid=(bsz // nb, nt),
        in_specs=[tile(y.shape[-1]), const(w_o), tile(d), mods(2),
                  pl.BlockSpec((1, d), lambda b, i: (0, 0)), mods(3), mods(4),
                  const(wcat), pl.BlockSpec((N_EXPERTS, 1), lambda b, i: (0, 0))],
        out_specs=[tile(d), pl.BlockSpec((nb, TM, d // 2), lambda b, i: (b, i, 0)), tok_spec, tok_spec, tok_spec,
                   pl.BlockSpec((N_EXPERTS, LANES), lambda b, i: (0, 0))],
        out_shape=[jax.ShapeDtypeStruct(s.shape, F32),
                   jax.ShapeDtypeStruct((bsz, nt * TM, d // 2), I32), tok(I32), tok(F32), tok(I32),
                   jax.ShapeDtypeStruct((N_EXPERTS, LANES), F32)],
        input_output_aliases={2: 0},
        compiler_params=_params("arbitrary", "arbitrary"),
        name="moe_pre",
    )(y, w_o, s, mod, gain.reshape(1, d), mod, mod, wcat, b_router.reshape(N_EXPERTS, 1))


def _sc_mesh():
    return plsc.VectorSubcoreMesh(core_axis_name="c", subcore_axis_name="s",
                                  num_cores=SC_CORES, num_subcores=SC_SUBCORES)


def _sc_worker_chunks(nchunks, body):
    n_workers = SC_CORES * SC_SUBCORES
    per_worker = -(-nchunks // n_workers)
    wid = lax.axis_index("s") * SC_CORES + lax.axis_index("c")

    @pl.loop(0, per_worker)
    def _(i):
        c = wid * per_worker + i

        @pl.when(c < nchunks)
        def _():
            body(c)


def _dispatch(hp, idx3, n_slots):
    n, w = hp.shape
    nchunks, kk, ch = idx3.shape

    @functools.partial(pl.kernel, mesh=_sc_mesh(), out_type=jax.ShapeDtypeStruct((n_slots, w), hp.dtype),
                       scratch_types=[pltpu.VMEM((kk, ch), I32), pltpu.VMEM((ch, w), hp.dtype)],
                       name="moe_dispatch")
    def scatter(src_hbm, idx_hbm, out_hbm, idx_v, rows_v):
        def body(c):
            pltpu.sync_copy(idx_hbm.at[c], idx_v)
            pltpu.sync_copy(src_hbm.at[pl.ds(c * ch, ch)], rows_v)
            for j in range(kk):
                pltpu.sync_copy(rows_v, out_hbm.at[idx_v.at[j]])
        _sc_worker_chunks(nchunks, body)

    return scatter(hp, idx3)


def _gather(ys, idx3):
    nchunks, kk, ch = idx3.shape
    w = ys.shape[1]

    @functools.partial(pl.kernel, mesh=_sc_mesh(), out_type=jax.ShapeDtypeStruct((kk, nchunks * ch, w), ys.dtype),
                       scratch_types=[pltpu.VMEM((kk, ch), I32), pltpu.VMEM((ch, w), ys.dtype)],
                       name="moe_gather")
    def gather(tab_hbm, idx_hbm, out_hbm, idx_v, rows_v):
        def body(c):
            pltpu.sync_copy(idx_hbm.at[c], idx_v)
            for j in range(kk):
                pltpu.sync_copy(tab_hbm.at[idx_v.at[j]], rows_v)
                pltpu.sync_copy(rows_v, out_hbm.at[j, pl.ds(c * ch, ch)])
        _sc_worker_chunks(nchunks, body)

    return gather(ys, idx3)


def _pack_pairs(x):
    w = x.shape[1] // 2
    bits = lax.bitcast_convert_type(x.astype(BF16).astype(F32), jnp.uint32)
    return lax.bitcast_convert_type((bits[:, :w] >> 16) | (bits[:, w:] & jnp.uint32(0xFFFF0000)), I32)


def _unpack_pairs(p):
    u = lax.bitcast_convert_type(p, jnp.uint32)
    lo = lax.bitcast_convert_type(u << 16, F32)
    hi = lax.bitcast_convert_type(u & jnp.uint32(0xFFFF0000), F32)
    return jnp.concatenate([lo, hi], axis=1)


def _ffn_kernel(te_ref, nv_ref, x_ref, w1_ref, b1_ref, w2_ref, b2_ref, y_ref, w1b_ref, w2b_ref):
    i = pl.program_id(0)

    @pl.when(jnp.logical_or(i == 0, te_ref[i] != te_ref[jnp.maximum(i - 1, 0)]))
    def _():
        w1b_ref[...] = w1_ref[0].astype(BF16)
        w2b_ref[...] = w2_ref[0].astype(BF16)

    nv = nv_ref[i]
    half = x_ref.shape[0] // 2

    def expert(rows):
        dff = w2_ref.shape[1]
        row = lax.broadcasted_iota(I32, (rows, 1), 0)
        x = _unpack_pairs(jnp.where(row < nv, x_ref[0:rows, :], 0)).astype(BF16)
        u = _dot(x, w1b_ref[...]) + b1_ref[0]
        glu = jnp.minimum(u[:, :dff], SWIGLU_LIMIT)
        lin = jnp.clip(u[:, dff:], -SWIGLU_LIMIT, SWIGLU_LIMIT)
        act = glu * jax.nn.sigmoid(SWIGLU_ALPHA * glu) * (lin + 1.0)
        y_ref[0:rows, :] = _pack_pairs(_dot(act.astype(BF16), w2b_ref[...]) + b2_ref[0])

    @pl.when(nv > half)
    def _():
        expert(2 * half)

    @pl.when(jnp.logical_and(nv > 0, nv <= half))
    def _():
        expert(half)
        y_ref[half:, :] = jnp.zeros((half, y_ref.shape[1]), y_ref.dtype)

    @pl.when(nv <= 0)
    def _():
        y_ref[...] = jnp.zeros_like(y_ref)


def _ffn(xs, tile_expert, n_valid, layer, w1, b1, w2, b2):
    ns, wd = xs.shape
    depth, ne, d, n1 = w1.shape
    dff = w2.shape[2]
    row = lambda i, te, nv: (i, 0)
    exp4 = lambda i, te, nv: (layer, te[i], 0, 0)
    return pl.pallas_call(
        _ffn_kernel,
        grid_spec=pltpu.PrefetchScalarGridSpec(
            num_scalar_prefetch=2,
            grid=(ns // TMX,),
            in_specs=[pl.BlockSpec((TMX, wd), row),
                      pl.BlockSpec((None, 1, d, n1), exp4), pl.BlockSpec((None, 1, 1, n1), exp4),
                      pl.BlockSpec((None, 1, dff, d), exp4), pl.BlockSpec((None, 1, 1, d), exp4)],
            out_specs=pl.BlockSpec((TMX, wd), row),
            scratch_shapes=[pltpu.VMEM((d, n1), BF16), pltpu.VMEM((dff, d), BF16)]),
        out_shape=jax.ShapeDtypeStruct((ns, wd), I32),
        compiler_params=pltpu.CompilerParams(dimension_semantics=("arbitrary",), vmem_limit_bytes=FFN_VMEM_LIMIT),
        name="moe_ffn",
    )(tile_expert, n_valid, xs, w1, b1.reshape(depth, ne, 1, n1), w2, b2.reshape(depth, ne, 1, d))


def _combine_kernel(wt_ref, s_ref, g_ref, yk_ref, o_ref):
    wt = wt_ref[...]
    f = wt[:, 0:1] * _unpack_pairs(yk_ref[0])
    for kk in range(1, TOP_K):
        f = f + wt[:, kk:kk + 1] * _unpack_pairs(yk_ref[kk])
    o_ref[0] = s_ref[0] + g_ref[...] * f


def _combine(yk, wts, s, mod, which, skip):
    bsz, t, d = s.shape
    nt = t // TM - skip
    return pl.pallas_call(
        _combine_kernel,
        grid=(bsz, nt),
        in_specs=[pl.BlockSpec((TM, TOP_K), lambda b, i: (b * nt + i, 0)),
                  pl.BlockSpec((1, TM, d), lambda b, i: (b, i + skip, 0)),
                  _mod_spec_d(bsz, which, d, skip),
                  pl.BlockSpec((TOP_K, TM, yk.shape[-1]), lambda b, i: (0, b * nt + i, 0))],
        out_specs=pl.BlockSpec((1, TM, d), lambda b, i: (b, i, 0)),
        out_shape=jax.ShapeDtypeStruct((bsz, nt * TM, d), F32),
        input_output_aliases={} if skip else {1: 0},
        compiler_params=_params("arbitrary", "arbitrary"),
        name="moe_combine",
    )(wts, s, mod, yk)


def _moe(y, w_o, s, gain, mod, layer, w_router, b_router, w1, b1, w2, b2, skip=0):
    bsz, t, d = s.shape
    n = bsz * (t - skip * TM)
    assert n % SC_CHUNK == 0
    s, hp, ids, wts, ranks, counts = _moe_pre(y, w_o, s, gain, mod, w_router, b_router, skip)
    cnt = counts[:, 0].astype(I32)
    padded = ((cnt + TMX - 1) // TMX) * TMX
    ends = jnp.cumsum(padded)
    offs = ends - padded
    n_tiles = (TOP_K * n + N_EXPERTS * (TMX - 1) + TMX - 1) // TMX
    tile_start = jnp.arange(n_tiles, dtype=I32) * TMX
    te = jnp.sum(tile_start[:, None] >= ends[None, :], axis=1).astype(I32)
    te_last = jnp.max(jnp.where(cnt > 0, jnp.arange(N_EXPERTS, dtype=I32), 0))
    te = jnp.minimum(te, te_last)
    group_end = jnp.sum(jnp.where(te[:, None] == jnp.arange(N_EXPERTS, dtype=I32), offs + cnt, 0), axis=-1)
    n_valid = jnp.where(tile_start < ends[-1], jnp.clip(group_end - tile_start, 0, TMX), 0).astype(I32)
    slots = ranks + jnp.sum(jnp.where(ids[..., None] == jnp.arange(N_EXPERTS, dtype=I32), offs, 0), axis=-1)
    idx3 = slots.reshape(bsz, TOP_K, -1, SC_CHUNK).transpose(0, 2, 1, 3).reshape(n // SC_CHUNK, TOP_K, SC_CHUNK)
    xs = _dispatch(hp.reshape(n, d // 2), idx3, n_tiles * TMX)
    ys = _ffn(xs, te, n_valid, layer, w1, b1, w2, b2)
    return _combine(_gather(ys, idx3), wts.transpose(0, 2, 1).reshape(n, TOP_K), s, mod, 5, skip)


def kernel(x, c, ctx, c_ctx, w_ada, b_ada, g_norm_mix, g_norm_ffn, na_w_qkv, na_q_gain, na_k_gain, na_rpb,
           na_w_o, ml_w_in, ml_b_gate, ml_mh_gain, ml_w_o, moe_w_router, moe_b_router, moe_w1, moe_b1,
           moe_w2, moe_b2):
    bsz, seq, d = x.shape
    n_ctx = ctx.shape[1]
    depth = w_ada.shape[0]
    rows = seq // GRID_W
    n_mod_rows = ((bsz + PRE_NB + 7) // 8) * 8
    cc = jnp.concatenate([c, jnp.tile(c_ctx[None, :], (PRE_NB, 1)),
                          jnp.zeros((n_mod_rows - bsz - PRE_NB, d), F32)], axis=0)
    mod_all = _ada(cc, w_ada, b_ada).reshape(depth, n_mod_rows, 6, 1, d)
    s = jnp.concatenate([ctx, x], axis=1)
    for i in range(depth):
        j = i // 2
        mod = mod_all[i]
        if i % 2 == 0:
            qkv = _na_qkv(s, g_norm_mix[i], mod, na_w_qkv[j].astype(BF16))
            y = _na_attn(qkv, _na_bias(na_rpb[j], rows), na_q_gain[j], na_k_gain[j], n_ctx)
            w_o = na_w_o[j]
        else:
            q, kt, v, og, gr, gc, gm = _ml_proj(s, g_norm_mix[i], mod, ml_w_in[j], ml_b_gate[j], n_ctx)
            y = _mlstm(q, kt, v, og, gr, gc, gm, ml_mh_gain[j])
            w_o = ml_w_o[j]
        skip = n_ctx // TM if i == depth - 1 else 0
        s = _moe(y, w_o.astype(BF16), s, g_norm_ffn[i], mod, i, moe_w_router[i], moe_b_router[i],
                 moe_w1, moe_b1, moe_w2, moe_b2, skip)
    return s
```

```python
import functools
import math

import numpy as np
import jax
import jax.numpy as jnp
from jax import lax
from jax.experimental import pallas as pl
from jax.experimental.pallas import tpu as pltpu
from jax.experimental.pallas import tpu_sc as plsc

F32 = jnp.float32
BF16 = jnp.bfloat16
I32 = jnp.int32

EPS = 1e-6
GRID_W = 64
NA_HEADS = 16
NA_HEAD_DIM = 64
NA_KR = 8
NA_KC = 16
NA_QROWS = 4
NA_KROWS = NA_QROWS + NA_KR - 1
NA_NB = 4
PRE_NB = 2
ML_HEADS = 8
ML_DV = 128
ML_DQK = 64
ML_CHUNK = 256
ML_GATE_CAP = 15.0
ROPE_THETA = 10000.0
N_EXPERTS = 32
TOP_K = 4
SWIGLU_LIMIT = 7.0
SWIGLU_ALPHA = 1.702

TM = 256
TMX = 512
LANES = 128
NEG = -1e30
VMEM_LIMIT = 48 * 1024 * 1024
SC_CORES = 2
SC_SUBCORES = 16
SC_CHUNK = 128
FFN_VMEM_LIMIT =56 * 1024 * 1024


def _dot(a, b):
    return jnp.dot(a, b, preferred_element_type=F32)


def _dot_nt(a, b):
    return lax.dot_general(a, b, (((1,), (1,)), ((), ())), preferred_element_type=F32)


def _split3(x):
    hi = x.astype(BF16)
    r1 = x - hi.astype(F32)
    mid = r1.astype(BF16)
    lo = (r1 - mid.astype(F32)).astype(BF16)
    return hi, mid, lo


def _params(*sem):
    return pltpu.CompilerParams(dimension_semantics=sem, vmem_limit_bytes=VMEM_LIMIT)


def _norm_mod(x, gain, shift, scale):
    r = lax.rsqrt(jnp.mean(x * x, axis=-1, keepdims=True) + EPS)
    return (x * r * gain) * (1.0 + scale) + shift


def _mod_spec_d(n_batch, which, d, skip=0):
    return pl.BlockSpec((None, None, 1, d), lambda b, t: (jnp.where(t + skip == 0, n_batch, b), which, 0, 0))


def _ada_kernel(c_ref, w_ref, b_ref, o_ref):
    c = c_ref[...]
    s = c * jax.nn.sigmoid(c)
    o_ref[0] = _dot(s.astype(BF16), w_ref[0].astype(BF16)) + b_ref[0]


def _ada(cc, w_ada, b_ada):
    depth, d, n = w_ada.shape
    bn = n // 4
    return pl.pallas_call(
        _ada_kernel,
        grid=(depth, n // bn),
        in_specs=[pl.BlockSpec((cc.shape[0], d), lambda l, j: (0, 0)),
                  pl.BlockSpec((1, d, bn), lambda l, j: (l, 0, j)),
                  pl.BlockSpec((1, 1, bn), lambda l, j: (l, 0, j))],
        out_specs=pl.BlockSpec((1, cc.shape[0], bn), lambda l, j: (l, 0, j)),
        out_shape=jax.ShapeDtypeStruct((depth, cc.shape[0], n), F32),
        compiler_params=_params("arbitrary", "arbitrary"),
        name="ada",
    )(cc, w_ada, b_ada.reshape(depth, 1, n))


def _na_qkv_kernel(x_ref, g_ref, sh_ref, sc_ref, w_ref, o_ref):
    h = _norm_mod(x_ref[0], g_ref[...], sh_ref[...], sc_ref[...])
    o_ref[0] = _dot(h.astype(BF16), w_ref[...]).astype(BF16)


def _na_qkv(s, gain, mod, w):
    bsz, t, d = s.shape
    n = w.shape[1]
    return pl.pallas_call(
        _na_qkv_kernel,
        grid=(bsz, t // TM),
        in_specs=[pl.BlockSpec((1, TM, d), lambda b, i: (b, i, 0)),
                  pl.BlockSpec((1, d), lambda b, i: (0, 0)),
                  _mod_spec_d(bsz, 0, d), _mod_spec_d(bsz, 1, d),
                  pl.BlockSpec((d, n), lambda b, i: (0, 0))],
        out_specs=pl.BlockSpec((1, TM, n), lambda b, i: (b, i, 0)),
        out_shape=jax.ShapeDtypeStruct((bsz, t, n), BF16),
        compiler_params=_params("arbitrary", "arbitrary"),
        name="na_qkv",
    )(s, gain.reshape(1, d), mod, mod, w)


def _na_bias(rpb, rows):
    nblk = rows // NA_QROWS
    kmax = rows - NA_KROWS
    nrr, ncc = 2 * NA_KR - 1, 2 * NA_KC - 1
    qc = np.arange(GRID_W)[:, None]
    kc = np.arange(GRID_W)[None, :]
    c0 = np.clip(qc - NA_KC // 2, 0, GRID_W - NA_KC)
    col_ok = (kc >= c0) & (kc < c0 + NA_KC)
    col_pick = np.eye(ncc)[np.clip(kc - qc + NA_KC - 1, 0, ncc - 1)]
    row_pick, valid = [], []
    for blk in (0, 1, nblk - 1):
        rs = blk * NA_QROWS
        ks = min(max(rs - NA_KR // 2, 0), kmax)
        qr = rs + np.arange(NA_QROWS)[:, None]
        kr = ks + np.arange(NA_KROWS)[None, :]
        r0 = np.clip(qr - NA_KR // 2, 0, rows - NA_KR)
        row_ok = (kr >= r0) & (kr < r0 + NA_KR)
        row_pick.append(np.eye(nrr)[np.clip(kr - qr + NA_KR - 1, 0, nrr - 1)])
        valid.append(row_ok[:, None, :, None] & col_ok[None, :, None, :])
    bias = jnp.einsum('vrka,pxab,qcb->pvxrqkc', jnp.asarray(np.stack(row_pick), F32),
                      rpb.reshape(NA_HEADS // 2, 2, nrr, ncc), jnp.asarray(col_pick, F32),
                      precision=lax.Precision.HIGHEST)
    bias = jnp.where(np.stack(valid)[None, :, None], bias, NEG)
    return bias.reshape(NA_HEADS // 2, 3, 2, NA_QROWS * GRID_W, NA_KROWS * GRID_W).astype(F32)


def _na_attn_kernel(q_ref, k_ref, v_ref, bias_ref, qg_ref, kg_ref, o_ref, kn_ref, v1_ref, *, n_ctx, rows):
    t = pl.program_id(2)
    first = lax.broadcasted_iota(I32, (1, LANES), 1) < NA_HEAD_DIM
    nk = NA_KROWS * GRID_W

    def head_rms(z, gain):
        z2 = z * z
        sa = jnp.sum(jnp.where(first, z2, 0.0), axis=-1, keepdims=True)
        sb = jnp.sum(jnp.where(first, 0.0, z2), axis=-1, keepdims=True)
        r = jnp.where(first, lax.rsqrt(sa / NA_HEAD_DIM + EPS), lax.rsqrt(sb / NA_HEAD_DIM + EPS))
        return z * r * gain

    nb = q_ref.shape[0]

    @pl.when(t == 0)
    def _():
        def body(i, carry):
            off = pl.multiple_of(i * TM, TM)
            for bb in range(nb):
                kn_ref[bb, pl.ds(off, TM), :] = head_rms(k_ref[bb, pl.ds(off, TM), :].astype(F32),
                                                         kg_ref[...]).astype(BF16)
                v = v_ref[bb, pl.ds(off, TM), :]
                v1_ref[bb, pl.ds(off, TM), :] = jnp.concatenate([v, jnp.ones_like(v)], axis=1)
            return carry
        lax.fori_loop(0, k_ref.shape[1] // TM, body, 0)

    def stacked_q(bb):
        qn = (head_rms(q_ref[bb].astype(F32), qg_ref[...]) * (NA_HEAD_DIM ** -0.5)).astype(BF16)
        zero = jnp.zeros_like(qn)
        return jnp.concatenate([jnp.where(first, qn, zero), jnp.where(first, zero, qn)], axis=0)

    def finish(bb, o):
        oa, ob = o[:TM], o[TM:]
        o_ref[bb] = jnp.where(first, oa[:, :LANES] / oa[:, LANES:],
                              ob[:, :LANES] / ob[:, LANES:]).astype(o_ref.dtype)

    @pl.when(t == 0)
    def _():
        for bb in range(nb):
            s = _dot_nt(stacked_q(bb), kn_ref[bb, 0:n_ctx, :])
            m = jnp.max(s, axis=-1, keepdims=True)
            finish(bb, _dot(jnp.exp((s - m).astype(BF16)), v1_ref[bb, 0:n_ctx, :]))

    @pl.when(t > 0)
    def _():
        blk = t - 1
        nblk = rows // NA_QROWS
        ks = jnp.clip(blk * NA_QROWS - NA_KR // 2, 0, rows - NA_KROWS)
        start = pl.multiple_of(n_ctx + ks * GRID_W, GRID_W)
        kind = jnp.where(blk == 0, 0, jnp.where(blk == nblk - 1, 2, 1))
        bias = bias_ref[0, kind].reshape(2 * TM, nk)
        for bb in range(nb):
            q2 = stacked_q(bb)
            sw = _dot_nt(q2, kn_ref[bb, pl.ds(start, nk), :]) + bias
            sc = _dot_nt(q2, kn_ref[bb, 0:n_ctx, :])
            m = jnp.maximum(jnp.max(sw, axis=-1, keepdims=True), jnp.max(sc, axis=-1, keepdims=True))
            pw = jnp.exp((sw - m).astype(BF16))
            pc = jnp.exp((sc - m).astype(BF16))
            finish(bb, _dot(pw, v1_ref[bb, pl.ds(start, nk), :]) + _dot(pc, v1_ref[bb, 0:n_ctx, :]))


def _na_attn(qkv, bias, q_gain, k_gain, n_ctx):
    bsz, t, _ = qkv.shape
    d = NA_HEADS * NA_HEAD_DIM
    rows = (t - n_ctx) // GRID_W
    assert n_ctx == TM and NA_QROWS * GRID_W == TM and rows % NA_QROWS == 0 and rows >= NA_KROWS
    hp = NA_HEADS // 2
    nq, nk = bias.shape[-2:]
    gq = jnp.tile(q_gain, 2).reshape(1, LANES)
    gk = jnp.tile(k_gain, 2).reshape(1, LANES)
    nb = NA_NB if bsz % NA_NB == 0 else 1
    return pl.pallas_call(
        functools.partial(_na_attn_kernel, n_ctx=n_ctx, rows=rows),
        grid=(hp, bsz // nb, t // TM),
        in_specs=[pl.BlockSpec((nb, TM, LANES), lambda h, b, i: (b, i, h)),
                  pl.BlockSpec((nb, t, LANES), lambda h, b, i: (b, 0, hp + h)),
                  pl.BlockSpec((nb, t, LANES), lambda h, b, i: (b, 0, 2 * hp + h)),
                  pl.BlockSpec((1, 3, 2, nq, nk), lambda h, b, i: (h, 0, 0, 0, 0)),
                  pl.BlockSpec((1, LANES), lambda h, b, i: (0, 0)),
                  pl.BlockSpec((1, LANES), lambda h, b, i: (0, 0))],
        out_specs=pl.BlockSpec((nb, TM, LANES), lambda h, b, i: (b, i, h)),
        out_shape=jax.ShapeDtypeStruct((bsz, t, d), BF16),
        scratch_shapes=[pltpu.VMEM((nb, t, LANES), BF16), pltpu.VMEM((nb, t, 2 * LANES), BF16)],
        compiler_params=_params("arbitrary", "arbitrary", "arbitrary"),
        name="na_attn",
    )(qkv, qkv, qkv, bias, gq, gk)


def _rope_tables(n_ctx, seq):
    lane = np.arange(LANES)
    dd = lane % ML_DQK
    grp = dd // (ML_DQK // 2)
    j = dd % (ML_DQK // 2)
    n = ML_DQK // 4
    inv = ROPE_THETA ** (-(j % n).astype(np.float64) / n)
    pos = np.arange(seq)
    p = np.where(grp[None, :] == 0, (pos // GRID_W)[:, None], (pos % GRID_W)[:, None]).astype(np.float64)
    ang = p * inv[None, :]
    cos = np.cos(ang)
    sin = np.where(j[None, :] < n, -np.sin(ang), np.sin(ang))
    cos = np.concatenate([np.ones((n_ctx, LANES)), cos], axis=0)
    sin = np.concatenate([np.zeros((n_ctx, LANES)), sin], axis=0)
    return jnp.asarray(cos, F32), jnp.asarray(sin, F32)


def _ml_proj_kernel(x_ref, g_ref, sh_ref, sc_ref, wm_ref, wg_ref, wgt_ref, bgr_ref, bgc_ref, cos_ref, sin_ref,
                    q_ref, kt_ref, v_ref, og_ref, gr_ref, gc_ref, gm_ref):
    h = _norm_mod(x_ref[0], g_ref[...], sh_ref[...], sc_ref[...]).astype(BF16)
    main = _dot(h, wm_ref[...])
    nq = q_ref.shape[-1]
    nv = v_ref.shape[-1]
    cos = cos_ref[...]
    sin = sin_ref[...]
    n = ML_DQK // 4
    lane = lax.broadcasted_iota(I32, (1, LANES), 1)
    low = (lane % (2 * n)) < n

    def rope(z):
        zr = jnp.where(low, pltpu.roll(z, LANES - n, 1), pltpu.roll(z, n, 1))
        return z * cos + zr * sin

    for j in range(nq // LANES):
        sl = slice(j * LANES, (j + 1) * LANES)
        q_ref[0, :, sl] = (rope(main[:, sl]) * (ML_DQK ** -0.5)).astype(BF16)
        kt_ref[0, sl, :] = rope(main[:, nq + j * LANES: nq + (j + 1) * LANES]).T.astype(BF16)
    v_ref[0] = main[:, 2 * nq: 2 * nq + nv].astype(BF16)
    og_ref[0] = main[:, 2 * nq + nv:].astype(BF16)

    nh = ML_HEADS
    ti = lax.broadcasted_iota(I32, (TM, TM), 0)
    si = lax.broadcasted_iota(I32, (TM, TM), 1)
    le = (si <= ti).astype(BF16)
    ge = (si >= ti).astype(BF16)

    def capped(pre):
        return ML_GATE_CAP * jnp.tanh(pre / ML_GATE_CAP)

    def exact_cum(tri, x, nt):
        parts = _split3(x)
        if nt:
            n = x.shape[0]
            p = _dot_nt(jnp.concatenate(parts, axis=0), tri)
            return p[:n] + p[n:2 * n] + p[2 * n:]
        n = x.shape[1]
        p = _dot(tri, jnp.concatenate(parts, axis=1))
        return p[:, :n] + p[:, n:2 * n] + p[:, 2 * n:]

    gc = capped(_dot(h, wg_ref[...]) + bgr_ref[...])
    lf = jax.nn.log_sigmoid(gc)
    pre = exact_cum(le, lf, False)
    suf = exact_cum(ge, lf, False)
    cum = jnp.where(lane < 2 * nh, pre, suf)
    r = gc - pltpu.roll(cum, LANES - nh, 1)
    is_i = (lane // nh) % 2 == 0
    gc_ref[0] = jnp.where(is_i, r, cum)[:, :4 * nh]
    row = lax.broadcasted_iota(I32, (TM, 1), 0)
    pmax = r
    smax = r
    step = 1
    while step < TM:
        pmax = jnp.maximum(pmax, jnp.where(row >= step, pltpu.roll(pmax, step, 0), NEG))
        smax = jnp.maximum(smax, jnp.where(row < TM - step, pltpu.roll(smax, TM - step, 0), NEG))
        step *= 2
    gm_ref[0] = jnp.where(lane < 2 * nh, pmax, smax)[:, :4 * nh]

    gr = capped(_dot_nt(wgt_ref[...], h) + bgc_ref[...])
    lfr = jax.nn.log_sigmoid(gr)
    pre_r = exact_cum(le, lfr, True)
    suf_r = exact_cum(ge, lfr, True)
    gr_ref[0, 0 * nh:1 * nh, :] = gr[0 * nh:1 * nh] - pre_r[1 * nh:2 * nh]
    gr_ref[0, 1 * nh:2 * nh, :] = pre_r[1 * nh:2 * nh]
    gr_ref[0, 2 * nh:3 * nh, :] = gr[2 * nh:3 * nh] - suf_r[3 * nh:4 * nh]
    gr_ref[0, 3 * nh:4 * nh, :] = suf_r[3 * nh:4 * nh]


def _ml_proj(s, gain, mod, w_in, b_gate, n_ctx):
    bsz, t, d = s.shape
    nq = ML_HEADS * ML_DQK
    nv = ML_HEADS * ML_DV
    ng = 4 * ML_HEADS
    assert TM == ML_CHUNK
    wm = w_in[:, :2 * nq + nv + d].astype(BF16)
    wg = w_in[:, 2 * nq + nv + d:]
    wg_pad = jnp.pad(wg, ((0, 0), (0, LANES - ng))).astype(BF16)
    wgt = wg.T.astype(BF16)
    bgr = jnp.pad(b_gate, (0, LANES - ng)).reshape(1, LANES)
    bgc = b_gate.reshape(ng, 1)
    cos, sin = _rope_tables(n_ctx, t - n_ctx)
    tile = lambda n: pl.BlockSpec((1, TM, n), lambda b, i: (b, i, 0))
    const = lambda a: pl.BlockSpec(a.shape, lambda b, i: (0, 0))
    return pl.pallas_call(
        _ml_proj_kernel,
        grid=(bsz, t // TM),
        in_specs=[tile(d), pl.BlockSpec((1, d), lambda b, i: (0, 0)),
                  _mod_spec_d(bsz, 0, d), _mod_spec_d(bsz, 1, d),
                  const(wm), const(wg_pad), const(wgt), const(bgr), const(bgc),
                  pl.BlockSpec((TM, LANES), lambda b, i: (i, 0)),
                  pl.BlockSpec((TM, LANES), lambda b, i: (i, 0))],
        out_specs=[tile(nq), pl.BlockSpec((1, nq, TM), lambda b, i: (b, 0, i)), tile(nv), tile(d),
                   pl.BlockSpec((1, ng, TM), lambda b, i: (b, 0, i)),
                   tile(ng), tile(ng)],
        out_shape=[jax.ShapeDtypeStruct((bsz, t, nq), BF16), jax.ShapeDtypeStruct((bsz, nq, t), BF16),
                   jax.ShapeDtypeStruct((bsz, t, nv), BF16), jax.ShapeDtypeStruct((bsz, t, d), BF16),
                   jax.ShapeDtypeStruct((bsz, ng, t), F32), jax.ShapeDtypeStruct((bsz, t, ng), F32),
                   jax.ShapeDtypeStruct((bsz, t, ng), F32)],
        compiler_params=_params("arbitrary", "arbitrary"),
        name="ml_proj",
    )(s, gain.reshape(1, d), mod, mod, wm, wg_pad, wgt, bgr, bgc, cos, sin)


def _mlstm_kernel(q_ref, kt_ref, v_ref, og_ref, gr_ref, gc_ref, gm_ref, mg_ref, o_ref, hf_ref, hb_ref, c_ref):
    L = ML_CHUNK
    nchunk = q_ref.shape[1] // L
    first = lax.broadcasted_iota(I32, (1, LANES), 1) < ML_DQK
    first_row = lax.broadcasted_iota(I32, (LANES, 1), 0) < ML_DQK
    ti = lax.broadcasted_iota(I32, (L, L), 0)
    si = lax.broadcasted_iota(I32, (L, L), 1)
    masks = (si <= ti, si >= ti)
    ones_blk = jnp.ones((L, ML_DV), BF16)

    c_ref[...] = jnp.zeros_like(c_ref)

    def chunk(c, a, bw, m0):
        off = pl.multiple_of(c * L, L)
        q = q_ref[0, pl.ds(off, L), :]
        kt = kt_ref[0, :, pl.ds(off, L)]
        qa = jnp.where(first if a == 0 else jnp.logical_not(first), q, jnp.zeros_like(q))
        kta = jnp.where(first_row if a == 0 else jnp.logical_not(first_row), kt, jnp.zeros_like(kt))
        v = v_ref[0, pl.ds(off, L), a * ML_DV:(a + 1) * ML_DV]
        w = 2 if bw else 0
        r_row = gr_ref[0, 0, 2 * w + a: 2 * w + a + 1, pl.ds(off, L)]
        cum_row = gr_ref[0, 0, 2 * (w + 1) + a: 2 * (w + 1) + a + 1, pl.ds(off, L)]
        cum_col = gc_ref[0, 0, pl.ds(off, L), 2 * (w + 1) + a: 2 * (w + 1) + a + 1]
        tot = cum_row[:, 0:1] if bw else cum_row[:, L - 1:L]
        mask = masks[1 if bw else 0]
        mx = gm_ref[0, 0, pl.ds(off, L), w + a: w + a + 1]
        mm = jnp.broadcast_to(jnp.maximum(m0, mx), (L, LANES))
        cum_b = jnp.broadcast_to(cum_col, (L, LANES))
        e = jnp.exp(jnp.where(mask, r_row - jnp.concatenate([mm] * (L // LANES), axis=1), NEG))
        s = (_dot(qa, kt) * e).astype(BF16)
        v2 = jnp.concatenate([v, ones_blk], axis=1)
        idx = 2 * a + (1 if bw else 0)
        cst = c_ref[idx]
        w_inter = jnp.exp(m0 - mm)
        both = _dot(s, v2) + jnp.concatenate([w_inter, w_inter], axis=1) * _dot(qa, cst.astype(BF16))
        num = both[:, :ML_DV]
        den = both[:, ML_DV:]
        hout = num / jnp.maximum(jnp.abs(den), jnp.exp(-cum_b - mm))
        dst = hb_ref if bw else hf_ref
        dst[pl.ds(off, L), a * ML_DV:(a + 1) * ML_DV] = hout
        m_loc = tot + jnp.max(r_row, axis=1, keepdims=True)
        kte = (kta.astype(F32) * jnp.exp(tot + r_row - m_loc)).astype(BF16)
        u = _dot(kte, v2)
        m_new = jnp.maximum(tot + m0, m_loc)
        c_ref[idx] = jnp.exp(tot + m0 - m_new) * cst + jnp.exp(m_loc - m_new) * u
        return m_new

    def body(i, ms):
        cb = jnp.where(i == 0, 0, nchunk - i)
        return (chunk(i, 0, False, ms[0]), chunk(i, 1, False, ms[1]),
                chunk(cb, 0, True, ms[2]), chunk(cb, 1, True, ms[3]))

    z = jnp.zeros((1, 1), F32)
    lax.fori_loop(0, nchunk, body, (z, z, z, z))

    def epilogue(i, carry):
        off = pl.multiple_of(i * L, L)
        for a in range(2):
            sl = slice(a * ML_DV, (a + 1) * ML_DV)
            hs = hf_ref[pl.ds(off, L), sl] + hb_ref[pl.ds(off, L), sl]
            hn = hs * lax.rsqrt(jnp.mean(hs * hs, axis=-1, keepdims=True) + EPS) * mg_ref[:, sl]
            o_ref[0, pl.ds(off, L), sl] = (hn * jax.nn.sigmoid(og_ref[0, pl.ds(off, L), sl].astype(F32))).astype(BF16)
        return carry

    lax.fori_loop(0, nchunk, epilogue, 0)


def _mlstm(q, kt, v, og, gr, gc, gm, mh_gain):
    bsz, t, _ = q.shape
    hp = ML_HEADS // 2
    nv = ML_HEADS * ML_DV
    w2 = 2 * ML_DV
    gr = gr.reshape(bsz, 4, hp, 2, t).transpose(0, 2, 1, 3, 4).reshape(bsz, hp, 8, t)
    gc = gc.reshape(bsz, t, 4, hp, 2).transpose(0, 3, 1, 2, 4).reshape(bsz, hp, t, 8)
    gm = gm.reshape(bsz, t, 4, hp, 2)[:, :, 0::2].transpose(0, 3, 1, 2, 4).reshape(bsz, hp, t, 4)
    return pl.pallas_call(
        _mlstm_kernel,
        grid=(bsz, hp),
        in_specs=[pl.BlockSpec((1, t, LANES), lambda b, h: (b, 0, h)),
                  pl.BlockSpec((1, LANES, t), lambda b, h: (b, h, 0)),
                  pl.BlockSpec((1, t, w2), lambda b, h: (b, 0, h)),
                  pl.BlockSpec((1, t, w2), lambda b, h: (b, 0, h)),
                  pl.BlockSpec((1, 1, 8, t), lambda b, h: (b, h, 0, 0)),
                  pl.BlockSpec((1, 1, t, 8), lambda b, h: (b, h, 0, 0)),
                  pl.BlockSpec((1, 1, t, 4), lambda b, h: (b, h, 0, 0)),
                  pl.BlockSpec((1, w2), lambda b, h: (0, h))],
        out_specs=pl.BlockSpec((1, t, w2), lambda b, h: (b, 0, h)),
        out_shape=jax.ShapeDtypeStruct((bsz, t, nv), BF16),
        scratch_shapes=[pltpu.VMEM((t, w2), F32), pltpu.VMEM((t, w2), F32),
                        pltpu.VMEM((4, LANES, w2), F32)],
        compiler_params=_params("arbitrary", "arbitrary"),
        name="mlstm",
    )(q, kt, v, og, gr, gc, gm, mh_gain.reshape(1, nv))


def _moe_pre_kernel(y_ref, wo_ref, x_ref, g1_ref, g_ref, sh_ref, sc_ref, wr_ref, br_ref,
                    s_ref, h_ref, id_ref, wt_ref, rk_ref, cnt_ref):
    step = pl.program_id(0) * pl.num_programs(1) + pl.program_id(1)

    @pl.when(step == 0)
    def _():
        cnt_ref[...] = jnp.zeros_like(cnt_ref)

    ti = lax.broadcasted_iota(I32, (TM, TM), 0)
    si = lax.broadcasted_iota(I32, (TM, TM), 1)
    before = (ti < si).astype(BF16)
    ne = br_ref.shape[0]
    wcat = wr_ref[...]
    for bb in range(x_ref.shape[0]):
        x = x_ref[bb] + g1_ref[bb] * _dot(y_ref[bb], wo_ref[...])
        s_ref[bb] = x
        h = _norm_mod(x, g_ref[...], sh_ref[bb], sc_ref[bb])
        h_ref[bb] = _pack_pairs(h)
        hhi = h.astype(BF16)
        hlo = (h - hhi.astype(F32)).astype(BF16)
        both = _dot_nt(wcat, hhi)
        logits = both[:ne] + both[ne:] + _dot_nt(wcat[:ne], hlo) + br_ref[...]
        eidx = lax.broadcasted_iota(I32, logits.shape, 0)
        sels, vals, ids = [], [], []
        l = logits
        for _ in range(TOP_K):
            mval = jnp.max(l, axis=0, keepdims=True)
            idx = jnp.min(jnp.where(l == mval, eidx, ne), axis=0, keepdims=True)
            sel = eidx == idx
            l = jnp.where(sel, -jnp.inf, l)
            sels.append(sel)
            vals.append(mval)
            ids.append(idx)
        ex = [jnp.exp(v - vals[0]) for v in vals]
        tot = ex[0] + ex[1] + ex[2] + ex[3]
        member = sels[0] | sels[1] | sels[2] | sels[3]
        rank_e = _dot(member.astype(BF16), before) + cnt_ref[:, 0:1]
        for kk in range(TOP_K):
            id_ref[bb, kk:kk + 1, :] = ids[kk]
            wt_ref[bb, kk:kk + 1, :] = ex[kk] / tot
            rk_ref[bb, kk:kk + 1, :] = jnp.sum(jnp.where(sels[kk], rank_e, 0.0), axis=0, keepdims=True).astype(I32)
        cnt_ref[...] = cnt_ref[...] + jnp.sum(member.astype(F32), axis=1, keepdims=True)


def _moe_pre(y, w_o, s, gain, mod, w_router, b_router, skip):
    bsz, t, d = s.shape
    nt = t // TM - skip
    nb = PRE_NB if bsz % PRE_NB == 0 else 1
    wt = w_router.T
    whi = wt.astype(BF16)
    wcat = jnp.concatenate([whi, (wt - whi.astype(F32)).astype(BF16)], axis=0)
    tok = lambda dt: jax.ShapeDtypeStruct((bsz, TOP_K, nt * TM), dt)
    tok_spec = pl.BlockSpec((nb, TOP_K, TM), lambda b, i: (b, 0, i))
    tile = lambda w: pl.BlockSpec((nb, TM, w), lambda b, i: (b, i + skip, 0))
    const = lambda a: pl.BlockSpec(a.shape, lambda b, i: (0, 0))
    mods = lambda which: pl.BlockSpec((nb, None, 1, d),
                                      lambda b, i: (jnp.where(i + skip == 0, bsz // nb, b), which, 0, 0))
    return pl.pallas_call(
        _moe_pre_kernel,
        grid=(bsz // nb, nt),
        in_specs=[tile(y.shape[-1]), const(w_o), tile(d), mods(2),
                  pl.BlockSpec((1, d), lambda b, i: (0, 0)), mods(3), mods(4),
                  const(wcat), pl.BlockSpec((N_EXPERTS, 1), lambda b, i: (0, 0))],
        out_specs=[tile(d), pl.BlockSpec((nb, TM, d // 2), lambda b, i: (b, i, 0)), tok_spec, tok_spec, tok_spec,
                   pl.BlockSpec((N_EXPERTS, LANES), lambda b, i: (0, 0))],
        out_shape=[jax.ShapeDtypeStruct(s.shape, F32),
                   jax.ShapeDtypeStruct((bsz, nt * TM, d // 2), I32), tok(I32), tok(F32), tok(I32),
                   jax.ShapeDtypeStruct((N_EXPERTS, LANES), F32)],
        input_output_aliases={2: 0},
        compiler_params=_params("arbitrary", "arbitrary"),
        name="moe_pre",
    )(y, w_o, s, mod, gain.reshape(1, d), mod, mod, wcat, b_router.reshape(N_EXPERTS, 1))


def _sc_mesh():
    return plsc.VectorSubcoreMesh(core_axis_name="c", subcore_axis_name="s",
                                  num_cores=SC_CORES, num_subcores=SC_SUBCORES)


def _sc_worker_chunks(nchunks, body):
    n_workers = SC_CORES * SC_SUBCORES
    per_worker = -(-nchunks // n_workers)
    wid = lax.axis_index("s") * SC_CORES + lax.axis_index("c")

    @pl.loop(0, per_worker)
    def _(i):
        c = wid * per_worker + i

        @pl.when(c < nchunks)
        def _():
            body(c)


def _dispatch(hp, idx3, n_slots):
    n, w = hp.shape
    nchunks, kk, ch = idx3.shape

    @functools.partial(pl.kernel, mesh=_sc_mesh(), out_type=jax.ShapeDtypeStruct((n_slots, w), hp.dtype),
                       scratch_types=[pltpu.VMEM((kk, ch), I32), pltpu.VMEM((ch, w), hp.dtype)],
                       name="moe_dispatch")
    def scatter(src_hbm, idx_hbm, out_hbm, idx_v, rows_v):
        def body(c):
            pltpu.sync_copy(idx_hbm.at[c], idx_v)
            pltpu.sync_copy(src_hbm.at[pl.ds(c * ch, ch)], rows_v)
            for j in range(kk):
                pltpu.sync_copy(rows_v, out_hbm.at[idx_v.at[j]])
        _sc_worker_chunks(nchunks, body)

    return scatter(hp, idx3)


def _gather(ys, idx3):
    nchunks, kk, ch = idx3.shape
    w = ys.shape[1]

    @functools.partial(pl.kernel, mesh=_sc_mesh(), out_type=jax.ShapeDtypeStruct((kk, nchunks * ch, w), ys.dtype),
                       scratch_types=[pltpu.VMEM((kk, ch), I32), pltpu.VMEM((ch, w), ys.dtype)],
                       name="moe_gather")
    def gather(tab_hbm, idx_hbm, out_hbm, idx_v, rows_v):
        def body(c):
            pltpu.sync_copy(idx_hbm.at[c], idx_v)
            for j in range(kk):
                pltpu.sync_copy(tab_hbm.at[idx_v.at[j]], rows_v)
                pltpu.sync_copy(rows_v, out_hbm.at[j, pl.ds(c * ch, ch)])
        _sc_worker_chunks(nchunks, body)

    return gather(ys, idx3)


def _pack_pairs(x):
    w = x.shape[1] // 2
    bits = lax.bitcast_convert_type(x.astype(BF16).astype(F32), jnp.uint32)
    return lax.bitcast_convert_type((bits[:, :w] >> 16) | (bits[:, w:] & jnp.uint32(0xFFFF0000)), I32)


def _unpack_pairs(p):
    u = lax.bitcast_convert_type(p, jnp.uint32)
    lo = lax.bitcast_convert_type(u << 16, F32)
    hi = lax.bitcast_convert_type(u & jnp.uint32(0xFFFF0000), F32)
    return jnp.concatenate([lo, hi], axis=1)


def _ffn_kernel(te_ref, nv_ref, x_ref, w1_ref, b1_ref, w2_ref, b2_ref, y_ref, w1b_ref, w2b_ref):
    i = pl.program_id(0)

    @pl.when(jnp.logical_or(i == 0, te_ref[i] != te_ref[jnp.maximum(i - 1, 0)]))
    def _():
        w1b_ref[...] = w1_ref[0].astype(BF16)
        w2b_ref[...] = w2_ref[0].astype(BF16)

    nv = nv_ref[i]
    half = x_ref.shape[0] // 2

    def expert(rows):
        dff = w2_ref.shape[1]
        row = lax.broadcasted_iota(I32, (rows, 1), 0)
        x = _unpack_pairs(jnp.where(row < nv, x_ref[0:rows, :], 0)).astype(BF16)
        u = _dot(x, w1b_ref[...]) + b1_ref[0]
        glu = jnp.minimum(u[:, :dff], SWIGLU_LIMIT)
        lin = jnp.clip(u[:, dff:], -SWIGLU_LIMIT, SWIGLU_LIMIT)
        act = glu * jax.nn.sigmoid(SWIGLU_ALPHA * glu) * (lin + 1.0)
        y_ref[0:rows, :] = _pack_pairs(_dot(act.astype(BF16), w2b_ref[...]) + b2_ref[0])

    @pl.when(nv > half)
    def _():
        expert(2 * half)

    @pl.when(jnp.logical_and(nv > 0, nv <= half))
    def _():
        expert(half)
        y_ref[half:, :] = jnp.zeros((half, y_ref.shape[1]), y_ref.dtype)

    @pl.when(nv <= 0)
    def _():
        y_ref[...] = jnp.zeros_like(y_ref)


def _ffn(xs, tile_expert, n_valid, layer, w1, b1, w2, b2):
    ns, wd = xs.shape
    depth, ne, d, n1 = w1.shape
    dff = w2.shape[2]
    row = lambda i, te, nv: (i, 0)
    exp4 = lambda i, te, nv: (layer, te[i], 0, 0)
    return pl.pallas_call(
        _ffn_kernel,
        grid_spec=pltpu.PrefetchScalarGridSpec(
            num_scalar_prefetch=2,
            grid=(ns // TMX,),
            in_specs=[pl.BlockSpec((TMX, wd), row),
                      pl.BlockSpec((None, 1, d, n1), exp4), pl.BlockSpec((None, 1, 1, n1), exp4),
                      pl.BlockSpec((None, 1, dff, d), exp4), pl.BlockSpec((None, 1, 1, d), exp4)],
            out_specs=pl.BlockSpec((TMX, wd), row),
            scratch_shapes=[pltpu.VMEM((d, n1), BF16), pltpu.VMEM((dff, d), BF16)]),
        out_shape=jax.ShapeDtypeStruct((ns, wd), I32),
        compiler_params=pltpu.CompilerParams(dimension_semantics=("arbitrary",), vmem_limit_bytes=FFN_VMEM_LIMIT),
        name="moe_ffn",
    )(tile_expert, n_valid, xs, w1, b1.reshape(depth, ne, 1, n1), w2, b2.reshape(depth, ne, 1, d))


def _combine_kernel(wt_ref, s_ref, g_ref, yk_ref, o_ref):
    wt = wt_ref[...]
    f = wt[:, 0:1] * _unpack_pairs(yk_ref[0])
    for kk in range(1, TOP_K):
        f = f + wt[:, kk:kk + 1] * _unpack_pairs(yk_ref[kk])
    o_ref[0] = s_ref[0] + g_ref[...] * f


def _combine(yk, wts, s, mod, which, skip):
    bsz, t, d = s.shape
    nt = t // TM - skip
    return pl.pallas_call(
        _combine_kernel,
        grid=(bsz, nt),
        in_specs=[pl.BlockSpec((TM, TOP_K), lambda b, i: (b * nt + i, 0)),
                  pl.BlockSpec((1, TM, d), lambda b, i: (b, i + skip, 0)),
                  _mod_spec_d(bsz, which, d, skip),
                  pl.BlockSpec((TOP_K, TM, yk.shape[-1]), lambda b, i: (0, b * nt + i, 0))],
        out_specs=pl.BlockSpec((1, TM, d), lambda b, i: (b, i, 0)),
        out_shape=jax.ShapeDtypeStruct((bsz, nt * TM, d), F32),
        input_output_aliases={} if skip else {1: 0},
        compiler_params=_params("arbitrary", "arbitrary"),
        name="moe_combine",
    )(wts, s, mod, yk)


def _moe(y, w_o, s, gain, mod, layer, w_router, b_router, w1, b1, w2, b2, skip=0):
    bsz, t, d = s.shape
    n = bsz * (t - skip * TM)
    assert n % SC_CHUNK == 0
    s, hp, ids, wts, ranks, counts = _moe_pre(y, w_o, s, gain, mod, w_router, b_router, skip)
    cnt = counts[:, 0].astype(I32)
    padded = ((cnt + TMX - 1) // TMX) * TMX
    ends = jnp.cumsum(padded)
    offs = ends - padded
    n_tiles = (TOP_K * n + N_EXPERTS * (TMX - 1) + TMX - 1) // TMX
    tile_start = jnp.arange(n_tiles, dtype=I32) * TMX
    te = jnp.sum(tile_start[:, None] >= ends[None, :], axis=1).astype(I32)
    te_last = jnp.max(jnp.where(cnt > 0, jnp.arange(N_EXPERTS, dtype=I32), 0))
    te = jnp.minimum(te, te_last)
    group_end = jnp.sum(jnp.where(te[:, None] == jnp.arange(N_EXPERTS, dtype=I32), offs + cnt, 0), axis=-1)
    n_valid = jnp.where(tile_start < ends[-1], jnp.clip(group_end - tile_start, 0, TMX), 0).astype(I32)
    slots = ranks + jnp.sum(jnp.where(ids[..., None] == jnp.arange(N_EXPERTS, dtype=I32), offs, 0), axis=-1)
    idx3 = slots.reshape(bsz, TOP_K, -1, SC_CHUNK).transpose(0, 2, 1, 3).reshape(n // SC_CHUNK, TOP_K, SC_CHUNK)
    xs = _dispatch(hp.reshape(n, d // 2), idx3, n_tiles * TMX)
    ys = _ffn(xs, te, n_valid, layer, w1, b1, w2, b2)
    return _combine(_gather(ys, idx3), wts.transpose(0, 2, 1).reshape(n, TOP_K), s, mod, 5, skip)


def kernel(x, c, ctx, c_ctx, w_ada, b_ada, g_norm_mix, g_norm_ffn, na_w_qkv, na_q_gain, na_k_gain, na_rpb,
           na_w_o, ml_w_in, ml_b_gate, ml_mh_gain, ml_w_o, moe_w_router, moe_b_router, moe_w1, moe_b1,
           moe_w2, moe_b2):
    bsz, seq, d = x.shape
    n_ctx = ctx.shape[1]
    depth = w_ada.shape[0]
    rows = seq // GRID_W
    n_mod_rows = ((bsz + PRE_NB + 7) // 8) * 8
    cc = jnp.concatenate([c, jnp.tile(c_ctx[None, :], (PRE_NB, 1)),
                          jnp.zeros((n_mod_rows - bsz - PRE_NB, d), F32)], axis=0)
    mod_all = _ada(cc, w_ada, b_ada).reshape(depth, n_mod_rows, 6, 1, d)
    s = jnp.concatenate([ctx, x], axis=1)
    for i in range(depth):
        j = i // 2
        mod = mod_all[i]
        if i % 2 == 0:
            qkv = _na_qkv(s, g_norm_mix[i], mod, na_w_qkv[j].astype(BF16))
            y = _na_attn(qkv, _na_bias(na_rpb[j], rows), na_q_gain[j], na_k_gain[j], n_ctx)
            w_o = na_w_o[j]
        else:
            q, kt, v, og, gr, gc, gm = _ml_proj(s, g_norm_mix[i], mod, ml_w_in[j], ml_b_gate[j], n_ctx)
            y = _mlstm(q, kt, v, og, gr, gc, gm, ml_mh_gain[j])
            w_o = ml_w_o[j]
        skip = n_ctx // TM if i == depth - 1 else 0
        s = _moe(y, w_o.astype(BF16), s, g_norm_ffn[i], mod, i, moe_w_router[i], moe_b_router[i],
                 moe_w1, moe_b1, moe_w2, moe_b2, skip)
    return s
```
